```python
import math
import jax, jax.numpy as jnp
from jax import lax
import numpy as np

D_MODEL = 1024
BATCH = 2
SEQ = 8192
DEPTH = 1

D_MIX = D_MODEL
D_ATT = D_MIX // 2
D_HYENA = D_MIX - D_ATT
DIFF_HEAD_DIM = 64
V_HEAD_DIM = 2 * DIFF_HEAD_DIM
N_DIFF_HEADS = D_ATT // V_HEAD_DIM
N_HYENA_GROUPS = 8
HYENA_GROUP_DIM = D_HYENA // N_HYENA_GROUPS
D_IN = 3 * D_ATT + 3 * D_HYENA
HYENA_SHORT_CONV = 3
HYENA_EMB_DIM = 33
HYENA_FILTER_ORDER = 64
HYENA_DECAY_TARGET = 1e-2
HYENA_FAST_DECAY = 0.3
HYENA_SLOW_DECAY = 1.5
D_FF = 2816
REL_BUCKETS = 32
REL_MAX_DIST = 128
Q_BLOCK = 128
RMS_EPS = 1e-6

kernel_name = "hymba_diffattn_hyena_macaron_encoder"

F32 = jnp.float32


def rmsnorm(x, g):
    xf = x.astype(F32)
    y = xf * lax.rsqrt(jnp.mean(xf * xf, axis=-1, keepdims=True) + RMS_EPS)
    return (y * g.astype(F32)).astype(x.dtype)


def swiglu(x, w_gate, w_up, w_down):
    return (jax.nn.silu(x @ w_gate) * (x @ w_up)) @ w_down


def t5_bucket(rel):
    half = REL_BUCKETS // 2
    max_exact = half // 2
    ret = jnp.where(rel > 0, half, 0)
    n = jnp.abs(rel)
    nf = jnp.maximum(n, 1).astype(F32)
    large = max_exact + (jnp.log(nf / max_exact) / math.log(REL_MAX_DIST / max_exact)
                         * (half - max_exact)).astype(jnp.int32)
    large = jnp.minimum(large, half - 1)
    return ret + jnp.where(n < max_exact, n, large)


def diff_attention(q, k, v, lam, lambda_init, subln_g, rel_table):
    B, S = q.shape[0], q.shape[1]
    nblk = S // Q_BLOCK
    qb = q.reshape(B, nblk, Q_BLOCK, N_DIFF_HEADS, 2, DIFF_HEAD_DIM).swapaxes(0, 1)
    kpos = jnp.arange(S, dtype=jnp.int32)
    scale = DIFF_HEAD_DIM ** -0.5

    def block(args):
        qblk, start = args
        qpos = start + jnp.arange(Q_BLOCK, dtype=jnp.int32)
        bucket = t5_bucket(kpos[None, :] - qpos[:, None])
        bias = jnp.take(rel_table, bucket, axis=0).transpose(2, 0, 1).astype(F32)
        logits = jnp.einsum('bqhcd,bkhcd->bhcqk', qblk, k,
                            preferred_element_type=F32) * scale + bias[None, :, None]
        p = jax.nn.softmax(logits, axis=-1)
        attn = p[:, :, 0] - lam * p[:, :, 1]
        return jnp.einsum('bhqk,bkhe->bqhe', attn.astype(v.dtype), v)

    starts = jnp.arange(nblk, dtype=jnp.int32) * Q_BLOCK
    out = lax.map(block, (qb, starts))
    out = out.swapaxes(0, 1).reshape(B, S, N_DIFF_HEADS, V_HEAD_DIM)
    out = rmsnorm(out, subln_g) * (1.0 - lambda_init)
    return out.reshape(B, S, N_DIFF_HEADS * V_HEAD_DIM)


def hyena_filters(L, w1, b1, w2, b2, w3, b3, w4, freq):
    t = jnp.linspace(0.0, 1.0, L, dtype=F32)[:, None]
    bands = (HYENA_EMB_DIM - 1) // 2
    w = 2.0 * math.pi * jnp.arange(L, dtype=F32)[:, None] / L
    f = jnp.linspace(1e-4, bands - 1, bands, dtype=F32)[None, :]
    fw = f * w
    z = jnp.concatenate([t, jnp.cos(fw), -jnp.sin(fw)], axis=-1)
    fr = freq.astype(F32)
    a = jnp.sin(fr * (z @ w1.astype(F32) + b1.astype(F32)))
    a = jnp.sin(fr * (a @ w2.astype(F32) + b2.astype(F32)))
    a = jnp.sin(fr * (a @ w3.astype(F32) + b3.astype(F32)))
    h = a @ w4.astype(F32)
    max_decay = math.log(HYENA_DECAY_TARGET) / HYENA_FAST_DECAY
    min_decay = math.log(HYENA_DECAY_TARGET) / HYENA_SLOW_DECAY
    deltas = jnp.linspace(min_decay, max_decay, D_HYENA, dtype=F32)
    decay = jnp.exp(-t * jnp.abs(deltas)[None, :])
    h = h.reshape(L, 2, D_HYENA) * decay[:, None, :]
    return h[:, 0], h[:, 1]


def bidirectional_fftconv(u, h_fwd, h_bwd):
    L = u.shape[1]
    n = 2 * L
    kern = jnp.concatenate([h_fwd, jnp.zeros((1, h_fwd.shape[1]), F32), h_bwd[1:][::-1]], axis=0)
    uf = jnp.fft.rfft(u.astype(F32), n=n, axis=1)
    kf = jnp.fft.rfft(kern, n=n, axis=0)
    y = jnp.fft.irfft(uf * kf[None], n=n, axis=1)[:, :L]
    return y.astype(u.dtype)


def short_conv(u, w, b):
    S = u.shape[1]
    pad = HYENA_SHORT_CONV // 2
    up = jnp.pad(u, ((0, 0), (pad, HYENA_SHORT_CONV - 1 - pad), (0, 0)))
    out = b
    for i in range(HYENA_SHORT_CONV):
        out = out + up[:, i:i + S] * w[i]
    return out


def hyena(z, conv_w, conv_b, f_w1, f_b1, f_w2, f_b2, f_w3, f_b3, f_w4, f_freq, filt_bias, out_g):
    B, S = z.shape[0], z.shape[1]
    u = short_conv(z, conv_w, conv_b)
    x0, x1, v = jnp.split(u, 3, axis=-1)
    h_fwd, h_bwd = hyena_filters(S, f_w1, f_b1, f_w2, f_b2, f_w3, f_b3, f_w4, f_freq)
    v = v * x1
    v = bidirectional_fftconv(v, h_fwd, h_bwd) + v * filt_bias
    y = v * x0
    y = rmsnorm(y.reshape(B, S, N_HYENA_GROUPS, HYENA_GROUP_DIM),
                out_g.reshape(N_HYENA_GROUPS, HYENA_GROUP_DIM))
    return y.reshape(B, S, D_HYENA)


def setup_inputs(seed: int = 0) -> dict:
    key = jax.random.key(seed)
    ks = iter(jax.random.split(key, 48))

    def nrm(shape, scale):
        return jax.random.normal(next(ks), shape, F32) * scale

    def gain(shape):
        return 1.0 + nrm(shape, 0.02)

    D, L, FO, E = D_MODEL, DEPTH, HYENA_FILTER_ORDER, HYENA_EMB_DIM
    return {
        "x": nrm((BATCH, SEQ, D), 1.0),
        "rel_bias": nrm((REL_BUCKETS, N_DIFF_HEADS), 0.2),
        "ffn1_norm": gain((L, D)),
        "ffn1_w_gate": nrm((L, D, D_FF), D ** -0.5),
        "ffn1_w_up": nrm((L, D, D_FF), D ** -0.5),
        "ffn1_w_down": nrm((L, D_FF, D), D_FF ** -0.5),
        "mix_norm": gain((L, D)),
        "w_in": nrm((L, D, D_IN), D ** -0.5),
        "lambda_q1": nrm((L, DIFF_HEAD_DIM), 0.1),
        "lambda_k1": nrm((L, DIFF_HEAD_DIM), 0.1),
        "lambda_q2": nrm((L, DIFF_HEAD_DIM), 0.1),
        "lambda_k2": nrm((L, DIFF_HEAD_DIM), 0.1),
        "diff_subln": gain((L, V_HEAD_DIM)),
        "hy_conv_w": nrm((L, HYENA_SHORT_CONV, 3 * D_HYENA), HYENA_SHORT_CONV ** -0.5),
        "hy_conv_b": nrm((L, 3 * D_HYENA), 0.02),
        "hy_f_w1": nrm((L, E, FO), E ** -0.5),
        "hy_f_b1": nrm((L, FO), 0.02),
        "hy_f_w2": nrm((L, FO, FO), FO ** -0.5),
        "hy_f_b2": nrm((L, FO), 0.02),
        "hy_f_w3": nrm((L, FO, FO), FO ** -0.5),
        "hy_f_b3": nrm((L, FO), 0.02),
        "hy_f_w4": nrm((L, FO, 2 * D_HYENA), FO ** -0.5),
        "hy_f_freq": gain((L, FO)),
        "hy_bias": nrm((L, D_HYENA), 1.0),
        "hy_out_norm": gain((L, D_HYENA)),
        "w_out": nrm((L, D_MIX, D), D_MIX ** -0.5),
        "ffn2_norm": gain((L, D)),
        "ffn2_w_gate": nrm((L, D, D_FF), D ** -0.5),
        "ffn2_w_up": nrm((L, D, D_FF), D ** -0.5),
        "ffn2_w_down": nrm((L, D_FF, D), D_FF ** -0.5),
        "final_norm": gain((D,)),
    }


def reference(x, rel_bias, ffn1_norm, ffn1_w_gate, ffn1_w_up, ffn1_w_down, mix_norm, w_in,
              lambda_q1, lambda_k1, lambda_q2, lambda_k2, diff_subln,
              hy_conv_w, hy_conv_b, hy_f_w1, hy_f_b1, hy_f_w2, hy_f_b2, hy_f_w3, hy_f_b3,
              hy_f_w4, hy_f_freq, hy_bias, hy_out_norm, w_out,
              ffn2_norm, ffn2_w_gate, ffn2_w_up, ffn2_w_down, final_norm):
    B, S = x.shape[0], x.shape[1]
    for l in range(DEPTH):
        x = x + 0.5 * swiglu(rmsnorm(x, ffn1_norm[l]), ffn1_w_gate[l], ffn1_w_up[l], ffn1_w_down[l])

        h = rmsnorm(x, mix_norm[l])
        p = h @ w_in[l]
        q = p[..., :D_ATT].reshape(B, S, N_DIFF_HEADS, 2, DIFF_HEAD_DIM)
        k = p[..., D_ATT:2 * D_ATT].reshape(B, S, N_DIFF_HEADS, 2, DIFF_HEAD_DIM)
        v = p[..., 2 * D_ATT:3 * D_ATT].reshape(B, S, N_DIFF_HEADS, V_HEAD_DIM)
        z_hy = p[..., 3 * D_ATT:]

        lambda_init = 0.8 - 0.6 * math.exp(-0.3 * l)
        lam = (jnp.exp(jnp.sum((lambda_q1[l] * lambda_k1[l]).astype(F32)))
               - jnp.exp(jnp.sum((lambda_q2[l] * lambda_k2[l]).astype(F32)))
               + lambda_init)
        att_out = diff_attention(q, k, v, lam, lambda_init, diff_subln[l], rel_bias)
        hy_out = hyena(z_hy, hy_conv_w[l], hy_conv_b[l], hy_f_w1[l], hy_f_b1[l], hy_f_w2[l],
                       hy_f_b2[l], hy_f_w3[l], hy_f_b3[l], hy_f_w4[l], hy_f_freq[l],
                       hy_bias[l], hy_out_norm[l])
        x = x + jnp.concatenate([att_out, hy_out], axis=-1) @ w_out[l]

        x = x + 0.5 * swiglu(rmsnorm(x, ffn2_norm[l]), ffn2_w_gate[l], ffn2_w_up[l], ffn2_w_down[l])
    return rmsnorm(x, final_norm)
```

```python
import functools
import math

import numpy as np
import jax
import jax.numpy as jnp
from jax import lax
from jax.experimental import pallas as pl
from jax.experimental.pallas import tpu as pltpu

F32 = jnp.float32
BF16 = jnp.bfloat16

RMS_EPS = 1e-6
LANES = 128
DIFF_HEAD_DIM = 64
V_HEAD_DIM = 2 * DIFF_HEAD_DIM
REL_BUCKETS = 32
REL_MAX_DIST = 128
N_HYENA_GROUPS = 8
HYENA_EMB_DIM = 33
HYENA_DECAY_TARGET = 1e-2
HYENA_FAST_DECAY = 0.3
HYENA_SLOW_DECAY = 1.5
N_LO = 128
VMEM_LIMIT = 56 * 1024 * 1024


def _params(*sem):
    return pltpu.CompilerParams(dimension_semantics=sem, vmem_limit_bytes=VMEM_LIMIT)


def _resident(shape):
    return pl.BlockSpec(shape, lambda *_: (0,) * len(shape), pipeline_mode=pl.Buffered(1))


def _rms(x, g):
    return x * lax.rsqrt(jnp.mean(x * x, axis=-1, keepdims=True) + RMS_EPS) * g


def _dot(a, b):
    return jnp.dot(a, b, preferred_element_type=F32)


def _ffn_kernel(*refs, ff_chunk, n_chunk, pre, post):
    if pre:
        x_ref, att_ref, hy_ref, woa_ref, woh_ref = refs[:5]
        refs = refs[5:]
    else:
        x_ref = refs[0]
        refs = refs[1:]
    g_ref, wg_ref, wu_ref, wd_ref = refs[:4]
    refs = refs[4:]
    if post:
        fg_ref, o_ref = refs
    else:
        (o_ref,) = refs

    x = x_ref[...]
    if pre:
        x = x + _dot(att_ref[...], woa_ref[...]) + _dot(hy_ref[...], woh_ref[...])
    xn = _rms(x, g_ref[...]).astype(BF16)
    acc = jnp.zeros(x.shape, F32)
    for c in range(n_chunk):
        sl = slice(c * ff_chunk, (c + 1) * ff_chunk)
        gate = _dot(xn, wg_ref[:, sl])
        up = _dot(xn, wu_ref[:, sl])
        h = (jax.nn.silu(gate) * up).astype(BF16)
        acc = acc + _dot(h, wd_ref[sl, :])
    y = x + 0.5 * acc
    if post:
        y = _rms(y, fg_ref[...])
    o_ref[...] = y


def _ffn(x, norm_g, wg, wu, wd, *, pre=None, final_g=None, tm=512, ff_chunk=256):
    m, d = x.shape
    dff = wg.shape[1]
    n_chunk = dff // ff_chunk
    assert n_chunk * ff_chunk == dff and m % tm == 0
    row = lambda w: pl.BlockSpec((tm, w), lambda i: (i, 0))
    args, specs = [x], [row(d)]
    if pre is not None:
        att, hy, woa, woh = pre
        args += [att, hy, woa, woh]
        specs += [row(att.shape[1]), row(hy.shape[1]), _resident(woa.shape), _resident(woh.shape)]
    args += [norm_g.reshape(1, d), wg, wu, wd]
    specs += [_resident((1, d)), _resident(wg.shape), _resident(wu.shape), _resident(wd.shape)]
    if final_g is not None:
        args.append(final_g.reshape(1, d))
        specs.append(_resident((1, d)))
    kern = functools.partial(_ffn_kernel, ff_chunk=ff_chunk, n_chunk=n_chunk,
                             pre=pre is not None, post=final_g is not None)
    return pl.pallas_call(
        kern, grid=(m // tm,), in_specs=specs, out_specs=row(d),
        out_shape=jax.ShapeDtypeStruct((m, d), F32),
        compiler_params=_params("parallel"), name="ffn_pre" if pre is not None else "ffn",
    )(*args)


def _mix_kernel(x_ref, g_ref, w_ref, qkv_ref, z_ref, *, d_att, col_chunk, scale):
    xn = _rms(x_ref[...], g_ref[...]).astype(BF16)
    n_att = 3 * d_att
    for c0 in range(0, w_ref.shape[1], col_chunk):
        p = _dot(xn, w_ref[:, c0:c0 + col_chunk])
        if c0 < n_att:
            if c0 < d_att:
                p = p * scale
            qkv_ref[:, c0:c0 + col_chunk] = p.astype(BF16)
        else:
            z_ref[:, c0 - n_att:c0 - n_att + col_chunk] = p


def _mix(x, norm_g, w_in, *, d_att, tm=512, col_chunk=512):
    m, d = x.shape
    d_in = w_in.shape[1]
    n_att = 3 * d_att
    assert d_att % col_chunk == 0 and d_in % col_chunk == 0 and m % tm == 0
    kern = functools.partial(_mix_kernel, d_att=d_att, col_chunk=col_chunk,
                             scale=DIFF_HEAD_DIM ** -0.5)
    return pl.pallas_call(
        kern, grid=(m // tm,),
        in_specs=[pl.BlockSpec((tm, d), lambda i: (i, 0)), _resident((1, d)), _resident(w_in.shape)],
        out_specs=[pl.BlockSpec((tm, n_att), lambda i: (i, 0)),
                   pl.BlockSpec((tm, d_in - n_att), lambda i: (i, 0))],
        out_shape=[jax.ShapeDtypeStruct((m, n_att), BF16),
                   jax.ShapeDtypeStruct((m, d_in - n_att), F32)],
        compiler_params=_params("parallel"), name="mix",
    )(x, norm_g.reshape(1, d), w_in)


def _bias_kernel(tab_ref, o_ref, *, t):
    h = pl.program_id(0)
    d = pl.program_id(1) - 1
    half = REL_BUCKETS // 2
    max_exact = half // 2
    rel = (lax.broadcasted_iota(jnp.int32, (t, t), 1) - lax.broadcasted_iota(jnp.int32, (t, t), 0)
           + d * t)
    ret = jnp.where(rel > 0, half, 0)
    n = jnp.abs(rel)
    nf = jnp.maximum(n, 1).astype(F32)
    large = max_exact + (jnp.log(nf / max_exact) / math.log(REL_MAX_DIST / max_exact)
                         * (half - max_exact)).astype(jnp.int32)
    large = jnp.minimum(large, half - 1)
    bucket = ret + jnp.where(n < max_exact, n, large)
    out = jnp.zeros((t, t), F32)
    for b in range(REL_BUCKETS):
        out = jnp.where(bucket == b, tab_ref[b, h], out)
    o_ref[...] = out


def _bias_tiles(rel_bias, t):
    nh = rel_bias.shape[1]
    return pl.pallas_call(
        functools.partial(_bias_kernel, t=t), grid=(nh, 3),
        in_specs=[pl.BlockSpec(memory_space=pltpu.SMEM)],
        out_specs=pl.BlockSpec((None, None, t, t), lambda h, d: (h, d, 0, 0)),
        out_shape=jax.ShapeDtypeStruct((nh, 3, t, t), F32),
        compiler_params=_params("parallel", "parallel"), name="rel_bias_tiles",
    )(rel_bias)


def _attn_kernel(tab_ref, q_ref, k_ref, v_ref, bt_ref, lq1_ref, lk1_ref, lq2_ref, lk2_ref, sg_ref,
                 o_ref, q2_ref, m_ref, l_ref, acc_ref, *, t, nk, lambda_init):
    h = pl.program_id(1)
    i = pl.program_id(2)
    q = q_ref[...]
    lane = lax.broadcasted_iota(jnp.int32, q.shape, 1)
    zero = jnp.zeros_like(q)
    q2_ref[0:t, :] = jnp.where(lane < DIFF_HEAD_DIM, q, zero)
    q2_ref[t:2 * t, :] = jnp.where(lane >= DIFF_HEAD_DIM, q, zero)
    m_ref[...] = jnp.full(m_ref.shape, -jnp.inf, F32)
    l_ref[...] = jnp.zeros(l_ref.shape, F32)
    acc_ref[...] = jnp.zeros(acc_ref.shape, F32)

    def step(j, c, bias):
        row0 = pl.multiple_of(j * t, t)
        k = k_ref[pl.ds(row0, t), :]
        v = v_ref[pl.ds(row0, t), :]
        s = lax.dot_general(q2_ref[...], k, (((1,), (1,)), ((), ())), preferred_element_type=F32)
        if bias is not None:
            s = s + jnp.concatenate([bias, bias], axis=0)
        cols = [s[:, c0:c0 + LANES] for c0 in range(0, t, LANES)]
        rm = functools.reduce(jnp.maximum, cols)
        m_prev = m_ref[...]
        m_new = jnp.maximum(m_prev, jnp.max(rm, axis=1, keepdims=True) + c)
        alpha = jnp.exp(m_prev - m_new)
        shift = m_new - c
        ps = [jnp.exp(col - shift) for col in cols]
        l_ref[...] = alpha * l_ref[...] + functools.reduce(jnp.add, ps)
        p = jnp.concatenate([x.astype(BF16) for x in ps], axis=1)
        acc_ref[...] = alpha * acc_ref[...] + _dot(p, v)
        m_ref[...] = m_new

    for d in (-1, 0, 1):
        j = i + d

        @pl.when(jnp.logical_and(j >= 0, j < nk))
        def _(j=j, d=d):
            step(j, 0.0, bt_ref[d + 1])

    c_left = tab_ref[REL_BUCKETS // 2 - 1, h]
    c_right = tab_ref[REL_BUCKETS - 1, h]
    lax.fori_loop(0, i - 1, lambda j, _: step(j, c_left, None), None)
    lax.fori_loop(i + 2, nk, lambda j, _: step(j, c_right, None), None)

    lam = (jnp.exp(jnp.sum(lq1_ref[...] * lk1_ref[...])) - jnp.exp(jnp.sum(lq2_ref[...] * lk2_ref[...]))
           + lambda_init)
    l = jnp.sum(l_ref[...], axis=1, keepdims=True)
    o = acc_ref[...] / l
    o = o[0:t] - lam * o[t:2 * t]
    o = _rms(o, sg_ref[...]) * (1.0 - lambda_init)
    o_ref[...] = o.astype(o_ref.dtype)


def _attention(qkv, bias_tiles, rel_bias, lq1, lk1, lq2, lk2, subln, *, n_heads, lambda_init, t):
    b, s, _ = qkv.shape
    nk = s // t
    assert nk * t == s and t % LANES == 0 and t + 1 >= 91
    kern = functools.partial(_attn_kernel, t=t, nk=nk, lambda_init=lambda_init)
    vec = lambda n: pl.BlockSpec((1, n), lambda b_, h, i: (0, 0))
    return pl.pallas_call(
        kern, grid=(b, n_heads, nk),
        in_specs=[
            pl.BlockSpec(memory_space=pltpu.SMEM),
            pl.BlockSpec((None, t, V_HEAD_DIM), lambda b_, h, i: (b_, i, h)),
            pl.BlockSpec((None, s, V_HEAD_DIM), lambda b_, h, i: (b_, 0, n_heads + h)),
            pl.BlockSpec((None, s, V_HEAD_DIM), lambda b_, h, i: (b_, 0, 2 * n_heads + h)),
            pl.BlockSpec((None, 3, t, t), lambda b_, h, i: (h, 0, 0, 0)),
            vec(DIFF_HEAD_DIM), vec(DIFF_HEAD_DIM), vec(DIFF_HEAD_DIM), vec(DIFF_HEAD_DIM),
            vec(V_HEAD_DIM),
        ],
        out_specs=pl.BlockSpec((None, t, V_HEAD_DIM), lambda b_, h, i: (b_, i, h)),
        out_shape=jax.ShapeDtypeStruct((b, s, n_heads * V_HEAD_DIM), BF16),
        scratch_shapes=[pltpu.VMEM((2 * t, V_HEAD_DIM), BF16), pltpu.VMEM((2 * t, LANES), F32),
                        pltpu.VMEM((2 * t, LANES), F32), pltpu.VMEM((2 * t, V_HEAD_DIM), F32)],
        compiler_params=_params("parallel", "parallel", "parallel"), name="diff_attention",
    )(rel_bias, qkv, qkv, qkv, bias_tiles, lq1.reshape(1, -1), lk1.reshape(1, -1),
      lq2.reshape(1, -1), lk2.reshape(1, -1), subln.reshape(1, -1))


def _dft_tables(n_hi):
    n = n_hi * N_LO
    half = n_hi // 2
    hi = np.arange(n_hi)
    lo = np.arange(N_LO)
    f_hi = np.exp(-2j * np.pi * ((np.outer(hi, hi) % n_hi) / n_hi))
    fr, fi = f_hi.real, f_hi.imag
    s1_pair = np.block([[fr[:, :half], -fi[:, :half]], [fi[:, :half], fr[:, :half]]])
    s1_real = np.concatenate([fr, fi], axis=0)
    ph = (hi[:, None, None] * lo[None, None, :] + n_hi * lo[None, :, None] * lo[None, None, :]) % n
    g = np.exp(-2j * np.pi * ph / n)
    g2 = np.concatenate([np.concatenate([g.real, -g.imag], axis=2),
                         np.concatenate([g.imag, g.real], axis=2)], axis=1)
    h2 = np.swapaxes(g2, 1, 2)
    s4 = np.block([[fr[:half], fi[:half]], [-fi[:half], fr[:half]]]) / n
    cast = lambda a: jnp.asarray(a.astype(np.float32)).astype(BF16)
    return cast(s1_pair), cast(s1_real), cast(g2), cast(h2), cast(s4)


def _hy_pre_kernel(z_ref, zp_ref, zn_ref, w_ref, b_ref, vb_ref, vv_ref, x0_ref, *, c, n_blk):
    i = pl.program_id(1)
    z = z_ref[...]
    rows = z.shape[0]
    r = lax.broadcasted_iota(jnp.int32, z.shape, 0)
    prev_row = jnp.where(i > 0, zp_ref[7:8, :], 0.0)
    next_row = jnp.where(i < n_blk - 1, zn_ref[0:1, :], 0.0)
    z_m1 = jnp.where(r == 0, prev_row, pltpu.roll(z, 1, 0))
    z_p1 = jnp.where(r == rows - 1, next_row, pltpu.roll(z, rows - 1, 0))
    u = b_ref[...] + z_m1 * w_ref[0:1, :] + z * w_ref[1:2, :] + z_p1 * w_ref[2:3, :]
    x0 = u[:, 0:c]
    vv = u[:, 2 * c:3 * c] * u[:, c:2 * c]
    vb_ref[...] = vv.astype(vb_ref.dtype)
    vv_ref[...] = vv
    x0_ref[...] = x0


def _hy_pre(z, conv_w, conv_b, *, tt=512):
    b, s, c3 = z.shape
    c = c3 // 3
    n_blk = s // tt
    assert n_blk * tt == s
    kern = functools.partial(_hy_pre_kernel, c=c, n_blk=n_blk)
    sub = tt // 8
    row = pl.BlockSpec((None, tt, c), lambda b_, i: (b_, i, 0))
    return pl.pallas_call(
        kern, grid=(b, n_blk),
        in_specs=[
            pl.BlockSpec((None, tt, c3), lambda b_, i: (b_, i, 0)),
            pl.BlockSpec((None, 8, c3), lambda b_, i: (b_, jnp.maximum(i * sub - 1, 0), 0)),
            pl.BlockSpec((None, 8, c3), lambda b_, i: (b_, jnp.minimum((i + 1) * sub, s // 8 - 1), 0)),
            pl.BlockSpec((3, c3), lambda b_, i: (0, 0)),
            pl.BlockSpec((1, c3), lambda b_, i: (0, 0)),
        ],
        out_specs=[row, row, row],
        out_shape=[jax.ShapeDtypeStruct((b, s, c), BF16),
                   jax.ShapeDtypeStruct((b, s, c), F32),
                   jax.ShapeDtypeStruct((b, s, c), F32)],
        compiler_params=_params("parallel", "parallel"), name="hyena_pre",
    )(z, z, z, conv_w, conv_b.reshape(1, c3))


def _hdot(a, b):
    return jnp.dot(a, b, preferred_element_type=F32, precision=lax.Precision.HIGHEST)


def _filter_kernel(z_ref, w1_ref, b1_ref, w2_ref, b2_ref, w3_ref, b3_ref, w4_ref, fr_ref, dl_ref,
                   o_ref, *, c, n_hi):
    i = pl.program_id(0)
    z = z_ref[...]
    fr = fr_ref[...]
    a = jnp.sin(fr * (_hdot(z, w1_ref[...]) + b1_ref[...]))
    a = jnp.sin(fr * (_hdot(a, w2_ref[...]) + b2_ref[...]))
    a = jnp.sin(fr * (_hdot(a, w3_ref[...]) + b3_ref[...]))
    hh = _hdot(a, w4_ref[...])
    decay = jnp.exp(-z[:, 0:1] * jnp.abs(dl_ref[...]))
    fwd = i < n_hi // 2
    hsel = jnp.where(fwd, hh[:, 0:c], hh[:, c:2 * c]) * decay
    r = lax.broadcasted_iota(jnp.int32, hsel.shape, 0)
    gap = jnp.logical_and(i == n_hi // 2, r == 0)
    o_ref[...] = jnp.where(gap, 0.0, hsel).astype(o_ref.dtype)


def _filter_taps(zk, w1, b1, w2, b2, w3, b3, w4, freq, deltas, *, n_hi):
    c = deltas.shape[0]
    fo = LANES
    pad2 = lambda a, r, cc: jnp.pad(a, ((0, r - a.shape[0]), (0, cc - a.shape[1])))
    rowp = lambda v: jnp.pad(v, (0, fo - v.shape[0])).reshape(1, fo)
    args = [pad2(zk, zk.shape[0], LANES), pad2(w1, LANES, fo), rowp(b1), pad2(w2, fo, fo), rowp(b2),
            pad2(w3, fo, fo), rowp(b3), pad2(w4, fo, 2 * c), rowp(freq), deltas.reshape(1, c)]
    specs = [pl.BlockSpec((N_LO, LANES), lambda i: (i, 0))] + [
        pl.BlockSpec(a.shape, lambda i: (0, 0)) for a in args[1:]]
    return pl.pallas_call(
        functools.partial(_filter_kernel, c=c, n_hi=n_hi), grid=(n_hi,),
        in_specs=specs,
        out_specs=pl.BlockSpec((None, N_LO, c), lambda i: (i, 0, 0)),
        out_shape=jax.ShapeDtypeStruct((n_hi, N_LO, c), BF16),
        compiler_params=_params("parallel"), name="hyena_filter_taps",
    )(*args)


def _dft1_kernel(*refs, n_in):
    x_refs, f_ref, ar_ref, ai_ref = refs[:n_in], refs[n_in], refs[n_in + 1], refs[n_in + 2]
    n_out = ar_ref.shape[0]
    for j in range(ar_ref.shape[1]):
        rows = [r[:, j, :] for r in x_refs]
        x = jnp.concatenate(rows, axis=0) if n_in > 1 else rows[0]
        a = _dot(f_ref[...], x)
        ar_ref[:, j, :] = a[0:n_out].astype(ar_ref.dtype)
        ai_ref[:, j, :] = a[n_out:2 * n_out].astype(ai_ref.dtype)


def _dft1_pair(xv, f_mat, *, n_hi, t_lo=16):
    b, half, _, c = xv.shape
    spec = lambda off: pl.BlockSpec((None, half, t_lo, c), lambda p, j: (2 * p + off, 0, j, 0))
    out = pl.BlockSpec((None, n_hi, t_lo, c), lambda p, j: (p, 0, j, 0))
    shp = jax.ShapeDtypeStruct((b // 2, n_hi, N_LO, c), BF16)
    return pl.pallas_call(
        functools.partial(_dft1_kernel, n_in=2), grid=(b // 2, N_LO // t_lo),
        in_specs=[spec(0), spec(1), pl.BlockSpec(f_mat.shape, lambda p, j: (0, 0))],
        out_specs=[out, out], out_shape=[shp, shp],
        compiler_params=_params("parallel", "parallel"), name="hyena_dft1",
    )(xv, xv, f_mat)


def _dft1_real(taps, f_mat, *, n_hi, t_lo=16):
    _, _, c = taps.shape
    blk = pl.BlockSpec((n_hi, t_lo, c), lambda j: (0, j, 0))
    shp = jax.ShapeDtypeStruct((n_hi, N_LO, c), BF16)
    return pl.pallas_call(
        functools.partial(_dft1_kernel, n_in=1), grid=(N_LO // t_lo,),
        in_specs=[blk, pl.BlockSpec(f_mat.shape, lambda j: (0, 0))],
        out_specs=[blk, blk], out_shape=[shp, shp],
        compiler_params=_params("parallel"), name="hyena_filter_dft1",
    )(taps, f_mat)


def _dft2_filter_kernel(ar_ref, ai_ref, g_ref, kr_ref, ki_ref):
    a = jnp.concatenate([ar_ref[...], ai_ref[...]], axis=0)
    x = _dot(g_ref[...], a)
    kr_ref[...] = x[0:N_LO]
    ki_ref[...] = x[N_LO:2 * N_LO]


def _dft2_filter(ar, ai, g2, *, n_hi):
    c = ar.shape[-1]
    blk = pl.BlockSpec((None, N_LO, c), lambda k: (k, 0, 0))
    shp = jax.ShapeDtypeStruct((n_hi, N_LO, c), F32)
    return pl.pallas_call(
        _dft2_filter_kernel, grid=(n_hi,),
        in_specs=[blk, blk, pl.BlockSpec((None, 2 * N_LO, 2 * N_LO), lambda k: (k, 0, 0))],
        out_specs=[blk, blk], out_shape=[shp, shp],
        compiler_params=_params("parallel"), name="hyena_filter_dft2",
    )(ar, ai, g2)


def _spectral_kernel(ar_ref, ai_ref, g_ref, h_ref, kr_ref, ki_ref, cr_ref, ci_ref):
    a = jnp.concatenate([ar_ref[...], ai_ref[...]], axis=0)
    x = _dot(g_ref[...], a)
    xr, xi = x[0:N_LO], x[N_LO:2 * N_LO]
    kr, ki = kr_ref[...], ki_ref[...]
    y = jnp.concatenate([xr * kr - xi * ki, xr * ki + xi * kr], axis=0).astype(BF16)
    cc = _dot(h_ref[...], y)
    cr_ref[...] = cc[0:N_LO].astype(cr_ref.dtype)
    ci_ref[...] = cc[N_LO:2 * N_LO].astype(ci_ref.dtype)


def _spectral(ar, ai, g2, h2, kfr, kfi, *, n_hi):
    p, _, _, c = ar.shape
    blk = pl.BlockSpec((None, None, N_LO, c), lambda q, k: (q, k, 0, 0))
    mat = pl.BlockSpec((None, 2 * N_LO, 2 * N_LO), lambda q, k: (k, 0, 0))
    kf = pl.BlockSpec((None, N_LO, c), lambda q, k: (k, 0, 0))
    shp = jax.ShapeDtypeStruct((p, n_hi, N_LO, c), BF16)
    return pl.pallas_call(
        _spectral_kernel, grid=(p, n_hi),
        in_specs=[blk, blk, mat, mat, kf, kf],
        out_specs=[blk, blk], out_shape=[shp, shp],
        compiler_params=_params("parallel", "parallel"), name="hyena_spectral",
    )(ar, ai, g2, h2, kfr, kfi)


def _idft_kernel(cr_ref, ci_ref, m_ref, o_ref):
    half = o_ref.shape[1]
    for j in range(o_ref.shape[2]):
        cc = jnp.concatenate([cr_ref[:, j, :], ci_ref[:, j, :]], axis=0)
        y = _dot(m_ref[...], cc)
        o_ref[0, :, j, :] = y[0:half]
        o_ref[1, :, j, :] = y[half:2 * half]


def _idft(cr, ci, s4, *, n_hi, t_lo=16):
    p, _, _, c = cr.shape
    half = n_hi // 2
    spec = pl.BlockSpec((None, n_hi, t_lo, c), lambda q, j: (q, 0, j, 0))
    return pl.pallas_call(
        _idft_kernel, grid=(p, N_LO // t_lo),
        in_specs=[spec, spec, pl.BlockSpec(s4.shape, lambda q, j: (0, 0))],
        out_specs=pl.BlockSpec((2, half, t_lo, c), lambda q, j: (q, 0, j, 0)),
        out_shape=jax.ShapeDtypeStruct((2 * p, half, N_LO, c), F32),
        compiler_params=_params("parallel", "parallel"), name="hyena_idft",
    )(cr, ci, s4)


def _hy_post_kernel(y_ref, vv_ref, x0_ref, fb_ref, og_ref, mg_ref, o_ref):
    vv = vv_ref[...]
    y = (y_ref[...] + vv * fb_ref[...]) * x0_ref[...]
    ms = _hdot(y * y, mg_ref[...])
    o_ref[...] = (y * lax.rsqrt(ms + RMS_EPS) * og_ref[...]).astype(o_ref.dtype)


def _hy_post(y, vv, x0, filt_bias, out_g, *, tt=512):
    b, s, c = vv.shape
    gdim = c // N_HYENA_GROUPS
    grp = np.arange(c) // gdim
    mg = jnp.asarray((grp[:, None] == grp[None, :]).astype(np.float32) / gdim)
    row = pl.BlockSpec((None, tt, c), lambda b_, i: (b_, i, 0))
    vec = pl.BlockSpec((1, c), lambda b_, i: (0, 0))
    return pl.pallas_call(
        _hy_post_kernel, grid=(b, s // tt),
        in_specs=[row, row, row, vec, vec, pl.BlockSpec((c, c), lambda b_, i: (0, 0))],
        out_specs=row, out_shape=jax.ShapeDtypeStruct((b, s, c), BF16),
        compiler_params=_params("parallel", "parallel"), name="hyena_post",
    )(y, vv, x0, filt_bias.reshape(1, c), out_g.reshape(1, c), mg)


def _filter_positions(s):
    t = jnp.linspace(0.0, 1.0, s, dtype=F32)[:, None]
    bands = (HYENA_EMB_DIM - 1) // 2
    w = 2.0 * math.pi * jnp.arange(s, dtype=F32)[:, None] / s
    f = jnp.linspace(1e-4, bands - 1, bands, dtype=F32)[None, :]
    fw = f * w
    z = jnp.concatenate([t, jnp.cos(fw), -jnp.sin(fw)], axis=-1)
    order = np.concatenate([np.arange(s), [0], np.arange(s - 1, 0, -1)])
    return z[order]


def _hyena(z_hy, conv_w, conv_b, w1, b1, w2, b2, w3, b3, w4, freq, filt_bias, out_g):
    b, s, c3 = z_hy.shape
    c = c3 // 3
    n_hi = 2 * s // N_LO
    half = n_hi // 2
    assert b % 2 == 0 and n_hi * N_LO == 2 * s
    s1_pair, s1_real, g2, h2, s4 = _dft_tables(n_hi)

    max_decay = math.log(HYENA_DECAY_TARGET) / HYENA_FAST_DECAY
    min_decay = math.log(HYENA_DECAY_TARGET) / HYENA_SLOW_DECAY
    deltas = jnp.linspace(min_decay, max_decay, c, dtype=F32)
    taps = _filter_taps(_filter_positions(s), w1, b1, w2, b2, w3, b3, w4, freq, deltas, n_hi=n_hi)
    far, fai = _dft1_real(taps, s1_real, n_hi=n_hi)
    kfr, kfi = _dft2_filter(far, fai, g2, n_hi=n_hi)

    vb, vv, x0 = _hy_pre(z_hy, conv_w, conv_b)
    ar, ai = _dft1_pair(vb.reshape(b, half, N_LO, c), s1_pair, n_hi=n_hi)
    cr, ci = _spectral(ar, ai, g2, h2, kfr, kfi, n_hi=n_hi)
    y = _idft(cr, ci, s4, n_hi=n_hi).reshape(b, s, c)
    return _hy_post(y, vv, x0, filt_bias, out_g)


def kernel(x, rel_bias, ffn1_norm, ffn1_w_gate, ffn1_w_up, ffn1_w_down, mix_norm, w_in,
           lambda_q1, lambda_k1, lambda_q2, lambda_k2, diff_subln,
           hy_conv_w, hy_conv_b, hy_f_w1, hy_f_b1, hy_f_w2, hy_f_b2, hy_f_w3, hy_f_b3,
           hy_f_w4, hy_f_freq, hy_bias, hy_out_norm, w_out,
           ffn2_norm, ffn2_w_gate, ffn2_w_up, ffn2_w_down, final_norm):
    b, s, d = x.shape
    depth = w_in.shape[0]
    d_att = diff_subln.shape[1] * rel_bias.shape[1]
    n_heads = rel_bias.shape[1]
    attn_tile = 256
    bf = lambda a: a.astype(BF16)

    xf = x.reshape(b * s, d)
    for l in range(depth):
        last = l == depth - 1
        xf = _ffn(xf, ffn1_norm[l], bf(ffn1_w_gate[l]), bf(ffn1_w_up[l]), bf(ffn1_w_down[l]))
        qkv, z_hy = _mix(xf, mix_norm[l], bf(w_in[l]), d_att=d_att)
        lambda_init = 0.8 - 0.6 * math.exp(-0.3 * l)
        att = _attention(qkv.reshape(b, s, -1), _bias_tiles(rel_bias, attn_tile), rel_bias,
                         lambda_q1[l], lambda_k1[l], lambda_q2[l], lambda_k2[l], diff_subln[l],
                         n_heads=n_heads, lambda_init=lambda_init, t=attn_tile)
        hy = _hyena(z_hy.reshape(b, s, -1), hy_conv_w[l], hy_conv_b[l], hy_f_w1[l], hy_f_b1[l],
                    hy_f_w2[l], hy_f_b2[l], hy_f_w3[l], hy_f_b3[l], hy_f_w4[l], hy_f_freq[l],
                    hy_bias[l], hy_out_norm[l])
        wo = bf(w_out[l])
        xf = _ffn(xf, ffn2_norm[l], bf(ffn2_w_gate[l]), bf(ffn2_w_up[l]), bf(ffn2_w_down[l]),
                  pre=(att.reshape(b * s, -1), hy.reshape(b * s, -1), wo[:d_att], wo[d_att:]),
                  final_g=final_norm if last else None)
    if depth == 0:
        raise ValueError("depth must be positive")
    return xf.reshape(b, s, d)
```

```python
import functools
import math

import numpy as np
import jax
import jax.numpy as jnp
from jax import lax
from jax.experimental import pallas as pl
from jax.experimental.pallas import tpu as pltpu

F32 = jnp.float32
BF16 = jnp.bfloat16

RMS_EPS = 1e-6
LANES = 128
DIFF_HEAD_DIM = 64
V_HEAD_DIM = 2 * DIFF_HEAD_DIM
REL_BUCKETS = 32
REL_MAX_DIST = 128
N_HYENA_GROUPS = 8
HYENA_EMB_DIM = 33
HYENA_DECAY_TARGET = 1e-2
HYENA_FAST_DECAY = 0.3
HYENA_SLOW_DECAY = 1.5
N_LO = 128
VMEM_LIMIT = 56 * 1024 * 1024


def _params(*sem):
    return pltpu.CompilerParams(dimension_semantics=sem, vmem_limit_bytes=VMEM_LIMIT)


def _resident(shape):
    return pl.BlockSpec(shape, lambda *_: (0,) * len(shape), pipeline_mode=pl.Buffered(1))


def _rms(x, g):
    return x * lax.rsqrt(jnp.mean(x * x, axis=-1, keepdims=True) + RMS_EPS) * g


def _dot(a, b):
    return jnp.dot(a, b, preferred_element_type=F32)


def _ffn_kernel(*refs, ff_chunk, n_chunk, pre, post):
    if pre:
        x_ref, att_ref, hy_ref, woa_ref, woh_ref = refs[:5]
        refs = refs[5:]
    else:
        x_ref = refs[0]
        refs = refs[1:]
    g_ref, wg_ref, wu_ref, wd_ref = refs[:4]
    refs = refs[4:]
    if post:
        fg_ref, o_ref = refs
    else:
        (o_ref,) = refs

    x = x_ref[...]
    if pre:
        x = x + _dot(att_ref[...], woa_ref[...]) + _dot(hy_ref[...], woh_ref[...])
    xn = _rms(x, g_ref[...]).astype(BF16)
    acc = jnp.zeros(x.shape, F32)
    for c in range(n_chunk):
        sl = slice(c * ff_chunk, (c + 1) * ff_chunk)
        gate = _dot(xn, wg_ref[:, sl])
        up = _dot(xn, wu_ref[:, sl])
        h = (jax.nn.silu(gate) * up).astype(BF16)
        acc = acc + _dot(h, wd_ref[sl, :])
    y = x + 0.5 * acc
    if post:
        y = _rms(y, fg_ref[...])
    o_ref[...] = y


def _ffn(x, norm_g, wg, wu, wd, *, pre=None, final_g=None, tm=512, ff_chunk=256):
    m, d = x.shape
    dff = wg.shape[1]
    n_chunk = dff // ff_chunk
    assert n_chunk * ff_chunk == dff and m % tm == 0
    row = lambda w: pl.BlockSpec((tm, w), lambda i: (i, 0))
    args, specs = [x], [row(d)]
    if pre is not None:
        att, hy, woa, woh = pre
        args += [att, hy, woa, woh]
        specs += [row(att.shape[1]), row(hy.shape[1]), _resident(woa.shape), _resident(woh.shape)]
    args += [norm_g.reshape(1, d), wg, wu, wd]
    specs += [_resident((1, d)), _resident(wg.shape), _resident(wu.shape), _resident(wd.shape)]
    if final_g is not None:
        args.append(final_g.reshape(1, d))
        specs.append(_resident((1, d)))
    kern = functools.partial(_ffn_kernel, ff_chunk=ff_chunk, n_chunk=n_chunk,
                             pre=pre is not None, post=final_g is not None)
    return pl.pallas_call(
        kern, grid=(m // tm,), in_specs=specs, out_specs=row(d),
        out_shape=jax.ShapeDtypeStruct((m, d), F32),
        compiler_params=_params("parallel"), name="ffn_pre" if pre is not None else "ffn",
    )(*args)


def _mix_kernel(x_ref, g_ref, w_ref, qkv_ref, z_ref, *, d_att, col_chunk, scale):
    xn = _rms(x_ref[...], g_ref[...]).astype(BF16)
    n_att = 3 * d_att
    for c0 in range(0, w_ref.shape[1], col_chunk):
        p = _dot(xn, w_ref[:, c0:c0 + col_chunk])
        if c0 < n_att:
            if c0 < d_att:
                p = p * scale
            qkv_ref[:, c0:c0 + col_chunk] = p.astype(BF16)
        else:
            z_ref[:, c0 - n_att:c0 - n_att + col_chunk] = p


def _mix(x, norm_g, w_in, *, d_att, tm=512, col_chunk=512):
    m, d = x.shape
    d_in = w_in.shape[1]
    n_att = 3 * d_att
    assert d_att % col_chunk == 0 and d_in % col_chunk == 0 and m % tm == 0
    kern = functools.partial(_mix_kernel, d_att=d_att, col_chunk=col_chunk,
                             scale=DIFF_HEAD_DIM ** -0.5)
    return pl.pallas_call(
        kern, grid=(m // tm,),
        in_specs=[pl.BlockSpec((tm, d), lambda i: (i, 0)), _resident((1, d)), _resident(w_in.shape)],
        out_specs=[pl.BlockSpec((tm, n_att), lambda i: (i, 0)),
                   pl.BlockSpec((tm, d_in - n_att), lambda i: (i, 0))],
        out_shape=[jax.ShapeDtypeStruct((m, n_att), BF16),
                   jax.ShapeDtypeStruct((m, d_in - n_att), F32)],
        compiler_params=_params("parallel"), name="mix",
    )(x, norm_g.reshape(1, d), w_in)


N_BIAS_TILES = 5


def _bias_kernel(tab_ref, o_ref, *, t):
    h = pl.program_id(0)
    d = pl.program_id(1) - N_BIAS_TILES // 2
    half = REL_BUCKETS // 2
    max_exact = half // 2
    rel = (lax.broadcasted_iota(jnp.int32, (t, t), 1) - lax.broadcasted_iota(jnp.int32, (t, t), 0)
           + d * t)
    ret = jnp.where(rel > 0, half, 0)
    n = jnp.abs(rel)
    nf = jnp.maximum(n, 1).astype(F32)
    large = max_exact + (jnp.log(nf / max_exact) / math.log(REL_MAX_DIST / max_exact)
                         * (half - max_exact)).astype(jnp.int32)
    large = jnp.minimum(large, half - 1)
    bucket = ret + jnp.where(n < max_exact, n, large)
    out = jnp.zeros((t, t), F32)
    for b in range(REL_BUCKETS):
        out = jnp.where(bucket == b, tab_ref[b, h], out)
    o_ref[...] = out


def _bias_tiles(rel_bias, t):
    nh = rel_bias.shape[1]
    return pl.pallas_call(
        functools.partial(_bias_kernel, t=t), grid=(nh, N_BIAS_TILES),
        in_specs=[pl.BlockSpec(memory_space=pltpu.SMEM)],
        out_specs=pl.BlockSpec((None, None, t, t), lambda h, d: (h, d, 0, 0)),
        out_shape=jax.ShapeDtypeStruct((nh, N_BIAS_TILES, t, t), F32),
        compiler_params=_params("parallel", "parallel"), name="rel_bias_tiles",
    )(rel_bias)


def _attn_kernel(q_ref, k_ref, v_ref, bt_ref, lq1_ref, lk1_ref, lq2_ref, lk2_ref, sg_ref,
                 o_ref, q2_ref, m_ref, l_ref, acc_ref, *, t, nk, unroll, lambda_init):
    i = pl.program_id(2)
    q = q_ref[...]
    lane = lax.broadcasted_iota(jnp.int32, q.shape, 1)
    zero = jnp.zeros_like(q)
    q2_ref[0:t, :] = jnp.where(lane < DIFF_HEAD_DIM, q, zero)
    q2_ref[t:2 * t, :] = jnp.where(lane >= DIFF_HEAD_DIM, q, zero)
    m_ref[...] = jnp.full(m_ref.shape, -jnp.inf, F32)
    l_ref[...] = jnp.zeros(l_ref.shape, F32)
    acc_ref[...] = jnp.zeros(acc_ref.shape, F32)
    far = N_BIAS_TILES // 2

    def tile(j, m_prev, l_prev, acc_prev):
        row0 = pl.multiple_of(j * t, t)
        k = k_ref[pl.ds(row0, t), :]
        v = v_ref[pl.ds(row0, t), :]
        s = lax.dot_general(q2_ref[...], k, (((1,), (1,)), ((), ())), preferred_element_type=F32)
        bias = bt_ref[jnp.clip(j - i, -far, far) + far]
        cols = []
        for c0 in range(0, t, LANES):
            b = bias[:, c0:c0 + LANES]
            cols.append(s[:, c0:c0 + LANES] + jnp.concatenate([b, b], axis=0))
        rm = functools.reduce(jnp.maximum, cols)
        m_new = jnp.maximum(m_prev, jnp.max(rm, axis=1, keepdims=True))
        alpha = jnp.exp(m_prev - m_new)
        ps = [jnp.exp(col - m_new) for col in cols]
        l_new = alpha * l_prev + functools.reduce(jnp.add, ps)
        p = jnp.concatenate([x.astype(BF16) for x in ps], axis=1)
        return m_new, l_new, alpha * acc_prev + _dot(p, v)

    def trip(jj, _):
        state = (m_ref[...], l_ref[...], acc_ref[...])
        for u in range(unroll):
            state = tile(jj * unroll + u, *state)
        m_ref[...], l_ref[...], acc_ref[...] = state

    lax.fori_loop(0, nk // unroll, trip, None)

    lam = (jnp.exp(jnp.sum(lq1_ref[...] * lk1_ref[...])) - jnp.exp(jnp.sum(lq2_ref[...] * lk2_ref[...]))
           + lambda_init)
    l = jnp.sum(l_ref[...], axis=1, keepdims=True)
    o = acc_ref[...] / l
    o = o[0:t] - lam * o[t:2 * t]
    o = _rms(o, sg_ref[...]) * (1.0 - lambda_init)
    o_ref[...] = o.astype(o_ref.dtype)


def _attention(qkv, bias_tiles, lq1, lk1, lq2, lk2, subln, *, n_heads, lambda_init, t, unroll):
    b, s, _ = qkv.shape
    nk = s // t
    assert nk * t == s and t % LANES == 0 and t + 1 >= 91 and nk % unroll == 0
    kern = functools.partial(_attn_kernel, t=t, nk=nk, unroll=unroll, lambda_init=lambda_init)
    vec = lambda n: pl.BlockSpec((1, n), lambda b_, h, i: (0, 0))
    return pl.pallas_call(
        kern, grid=(b, n_heads, nk),
        in_specs=[
            pl.BlockSpec((None, t, V_HEAD_DIM), lambda b_, h, i: (b_, i, h)),
            pl.BlockSpec((None, s, V_HEAD_DIM), lambda b_, h, i: (b_, 0, n_heads + h)),
            pl.BlockSpec((None, s, V_HEAD_DIM), lambda b_, h, i: (b_, 0, 2 * n_heads + h)),
            pl.BlockSpec((None, N_BIAS_TILES, t, t), lambda b_, h, i: (h, 0, 0, 0)),
            vec(DIFF_HEAD_DIM), vec(DIFF_HEAD_DIM), vec(DIFF_HEAD_DIM), vec(DIFF_HEAD_DIM),
            vec(V_HEAD_DIM),
        ],
        out_specs=pl.BlockSpec((None, t, V_HEAD_DIM), lambda b_, h, i: (b_, i, h)),
        out_shape=jax.ShapeDtypeStruct((b, s, n_heads * V_HEAD_DIM), BF16),
        scratch_shapes=[pltpu.VMEM((2 * t, V_HEAD_DIM), BF16), pltpu.VMEM((2 * t, LANES), F32),
                        pltpu.VMEM((2 * t, LANES), F32), pltpu.VMEM((2 * t, V_HEAD_DIM), F32)],
        compiler_params=_params("parallel", "parallel", "parallel"), name="diff_attention",
    )(qkv, qkv, qkv, bias_tiles, lq1.reshape(1, -1), lk1.reshape(1, -1),
      lq2.reshape(1, -1), lk2.reshape(1, -1), subln.reshape(1, -1))


def _dft_tables(n_hi):
    n = n_hi * N_LO
    half = n_hi // 2
    hi = np.arange(n_hi)
    lo = np.arange(N_LO)
    f_hi = np.exp(-2j * np.pi * ((np.outer(hi, hi) % n_hi) / n_hi))
    fr, fi = f_hi.real, f_hi.imag
    s1_pair = np.block([[fr[:, :half], -fi[:, :half]], [fi[:, :half], fr[:, :half]]])
    s1_real = np.concatenate([fr, fi], axis=0)
    ph = (hi[:, None, None] * lo[None, None, :] + n_hi * lo[None, :, None] * lo[None, None, :]) % n
    g = np.exp(-2j * np.pi * ph / n)
    g2 = np.concatenate([np.concatenate([g.real, -g.imag], axis=2),
                         np.concatenate([g.imag, g.real], axis=2)], axis=1)
    h2 = np.swapaxes(g2, 1, 2)
    s4 = np.block([[fr[:half], fi[:half]], [-fi[:half], fr[:half]]]) / n
    cast = lambda a: jnp.asarray(a.astype(np.float32)).astype(BF16)
    return cast(s1_pair), cast(s1_real), cast(g2), cast(h2), cast(s4)


def _hy_pre_kernel(z_ref, zp_ref, zn_ref, w_ref, b_ref, vb_ref, vv_ref, x0_ref, *, c, n_blk):
    i = pl.program_id(1)
    z = z_ref[...]
    rows = z.shape[0]
    r = lax.broadcasted_iota(jnp.int32, z.shape, 0)
    prev_row = jnp.where(i > 0, zp_ref[7:8, :], 0.0)
    next_row = jnp.where(i < n_blk - 1, zn_ref[0:1, :], 0.0)
    z_m1 = jnp.where(r == 0, prev_row, pltpu.roll(z, 1, 0))
    z_p1 = jnp.where(r == rows - 1, next_row, pltpu.roll(z, rows - 1, 0))
    u = b_ref[...] + z_m1 * w_ref[0:1, :] + z * w_ref[1:2, :] + z_p1 * w_ref[2:3, :]
    x0 = u[:, 0:c]
    vv = u[:, 2 * c:3 * c] * u[:, c:2 * c]
    vb_ref[...] = vv.astype(vb_ref.dtype)
    vv_ref[...] = vv
    x0_ref[...] = x0


def _hy_pre(z, conv_w, conv_b, *, tt=512):
    b, s, c3 = z.shape
    c = c3 // 3
    n_blk = s // tt
    assert n_blk * tt == s
    kern = functools.partial(_hy_pre_kernel, c=c, n_blk=n_blk)
    sub = tt // 8
    row = pl.BlockSpec((None, tt, c), lambda b_, i: (b_, i, 0))
    return pl.pallas_call(
        kern, grid=(b, n_blk),
        in_specs=[
            pl.BlockSpec((None, tt, c3), lambda b_, i: (b_, i, 0)),
            pl.BlockSpec((None, 8, c3), lambda b_, i: (b_, jnp.maximum(i * sub - 1, 0), 0)),
            pl.BlockSpec((None, 8, c3), lambda b_, i: (b_, jnp.minimum((i + 1) * sub, s // 8 - 1), 0)),
            pl.BlockSpec((3, c3), lambda b_, i: (0, 0)),
            pl.BlockSpec((1, c3), lambda b_, i: (0, 0)),
        ],
        out_specs=[row, row, row],
        out_shape=[jax.ShapeDtypeStruct((b, s, c), BF16),
                   jax.ShapeDtypeStruct((b, s, c), F32),
                   jax.ShapeDtypeStruct((b, s, c), F32)],
        compiler_params=_params("parallel", "parallel"), name="hyena_pre",
    )(z, z, z, conv_w, conv_b.reshape(1, c3))


def _hdot(a, b):
    return jnp.dot(a, b, preferred_element_type=F32, precision=lax.Precision.HIGHEST)


def _filter_kernel(z_ref, w1_ref, b1_ref, w2_ref, b2_ref, w3_ref, b3_ref, w4_ref, fr_ref, dl_ref,
                   o_ref, *, c, n_hi):
    i = pl.program_id(0)
    z = z_ref[...]
    fr = fr_ref[...]
    a = jnp.sin(fr * (_hdot(z, w1_ref[...]) + b1_ref[...]))
    a = jnp.sin(fr * (_hdot(a, w2_ref[...]) + b2_ref[...]))
    a = jnp.sin(fr * (_hdot(a, w3_ref[...]) + b3_ref[...]))
    hh = _hdot(a, w4_ref[...])
    decay = jnp.exp(-z[:, 0:1] * jnp.abs(dl_ref[...]))
    fwd = i < n_hi // 2
    hsel = jnp.where(fwd, hh[:, 0:c], hh[:, c:2 * c]) * decay
    r = lax.broadcasted_iota(jnp.int32, hsel.shape, 0)
    gap = jnp.logical_and(i == n_hi // 2, r == 0)
    o_ref[...] = jnp.where(gap, 0.0, hsel).astype(o_ref.dtype)


def _filter_taps(zk, w1, b1, w2, b2, w3, b3, w4, freq, deltas, *, n_hi):
    c = deltas.shape[0]
    fo = LANES
    pad2 = lambda a, r, cc: jnp.pad(a, ((0, r - a.shape[0]), (0, cc - a.shape[1])))
    rowp = lambda v: jnp.pad(v, (0, fo - v.shape[0])).reshape(1, fo)
    args = [pad2(zk, zk.shape[0], LANES), pad2(w1, LANES, fo), rowp(b1), pad2(w2, fo, fo), rowp(b2),
            pad2(w3, fo, fo), rowp(b3), pad2(w4, fo, 2 * c), rowp(freq), deltas.reshape(1, c)]
    specs = [pl.BlockSpec((N_LO, LANES), lambda i: (i, 0))] + [
        pl.BlockSpec(a.shape, lambda i: (0, 0)) for a in args[1:]]
    return pl.pallas_call(
        functools.partial(_filter_kernel, c=c, n_hi=n_hi), grid=(n_hi,),
        in_specs=specs,
        out_specs=pl.BlockSpec((None, N_LO, c), lambda i: (i, 0, 0)),
        out_shape=jax.ShapeDtypeStruct((n_hi, N_LO, c), BF16),
        compiler_params=_params("parallel"), name="hyena_filter_taps",
    )(*args)


def _dft1_kernel(*refs, n_in):
    x_refs, f_ref, ar_ref, ai_ref = refs[:n_in], refs[n_in], refs[n_in + 1], refs[n_in + 2]
    n_out = ar_ref.shape[0]
    for j in range(ar_ref.shape[1]):
        rows = [r[:, j, :] for r in x_refs]
        x = jnp.concatenate(rows, axis=0) if n_in > 1 else rows[0]
        a = _dot(f_ref[...], x)
        ar_ref[:, j, :] = a[0:n_out].astype(ar_ref.dtype)
        ai_ref[:, j, :] = a[n_out:2 * n_out].astype(ai_ref.dtype)


def _dft1_pair(xv, f_mat, *, n_hi, t_lo=16):
    b, half, _, c = xv.shape
    spec = lambda off: pl.BlockSpec((None, half, t_lo, c), lambda p, j: (2 * p + off, 0, j, 0))
    out = pl.BlockSpec((None, n_hi, t_lo, c), lambda p, j: (p, 0, j, 0))
    shp = jax.ShapeDtypeStruct((b // 2, n_hi, N_LO, c), BF16)
    return pl.pallas_call(
        functools.partial(_dft1_kernel, n_in=2), grid=(b // 2, N_LO // t_lo),
        in_specs=[spec(0), spec(1), pl.BlockSpec(f_mat.shape, lambda p, j: (0, 0))],
        out_specs=[out, out], out_shape=[shp, shp],
        compiler_params=_params("parallel", "parallel"), name="hyena_dft1",
    )(xv, xv, f_mat)


def _dft1_real(taps, f_mat, *, n_hi, t_lo=16):
    _, _, c = taps.shape
    blk = pl.BlockSpec((n_hi, t_lo, c), lambda j: (0, j, 0))
    shp = jax.ShapeDtypeStruct((n_hi, N_LO, c), BF16)
    return pl.pallas_call(
        functools.partial(_dft1_kernel, n_in=1), grid=(N_LO // t_lo,),
        in_specs=[blk, pl.BlockSpec(f_mat.shape, lambda j: (0, 0))],
        out_specs=[blk, blk], out_shape=[shp, shp],
        compiler_params=_params("parallel"), name="hyena_filter_dft1",
    )(taps, f_mat)


def _dft2_filter_kernel(ar_ref, ai_ref, g_ref, kr_ref, ki_ref):
    a = jnp.concatenate([ar_ref[...], ai_ref[...]], axis=0)
    x = _dot(g_ref[...], a)
    kr_ref[...] = x[0:N_LO]
    ki_ref[...] = x[N_LO:2 * N_LO]


def _dft2_filter(ar, ai, g2, *, n_hi):
    c = ar.shape[-1]
    blk = pl.BlockSpec((None, N_LO, c), lambda k: (k, 0, 0))
    shp = jax.ShapeDtypeStruct((n_hi, N_LO, c), F32)
    return pl.pallas_call(
        _dft2_filter_kernel, grid=(n_hi,),
        in_specs=[blk, blk, pl.BlockSpec((None, 2 * N_LO, 2 * N_LO), lambda k: (k, 0, 0))],
        out_specs=[blk, blk], out_shape=[shp, shp],
        compiler_params=_params("parallel"), name="hyena_filter_dft2",
    )(ar, ai, g2)


def _spectral_kernel(ar_ref, ai_ref, g_ref, h_ref, kr_ref, ki_ref, cr_ref, ci_ref):
    a = jnp.concatenate([ar_ref[...], ai_ref[...]], axis=0)
    x = _dot(g_ref[...], a)
    xr, xi = x[0:N_LO], x[N_LO:2 * N_LO]
    kr, ki = kr_ref[...], ki_ref[...]
    y = jnp.concatenate([xr * kr - xi * ki, xr * ki + xi * kr], axis=0).astype(BF16)
    cc = _dot(h_ref[...], y)
    cr_ref[...] = cc[0:N_LO].astype(cr_ref.dtype)
    ci_ref[...] = cc[N_LO:2 * N_LO].astype(ci_ref.dtype)


def _spectral(ar, ai, g2, h2, kfr, kfi, *, n_hi):
    p, _, _, c = ar.shape
    blk = pl.BlockSpec((None, None, N_LO, c), lambda q, k: (q, k, 0, 0))
    mat = pl.BlockSpec((None, 2 * N_LO, 2 * N_LO), lambda q, k: (k, 0, 0))
    kf = pl.BlockSpec((None, N_LO, c), lambda q, k: (k, 0, 0))
    shp = jax.ShapeDtypeStruct((p, n_hi, N_LO, c), BF16)
    return pl.pallas_call(
        _spectral_kernel, grid=(p, n_hi),
        in_specs=[blk, blk, mat, mat, kf, kf],
        out_specs=[blk, blk], out_shape=[shp, shp],
        compiler_params=_params("parallel", "parallel"), name="hyena_spectral",
    )(ar, ai, g2, h2, kfr, kfi)


def _idft_kernel(cr_ref, ci_ref, m_ref, o_ref):
    half = o_ref.shape[1]
    for j in range(o_ref.shape[2]):
        cc = jnp.concatenate([cr_ref[:, j, :], ci_ref[:, j, :]], axis=0)
        y = _dot(m_ref[...], cc)
        o_ref[0, :, j, :] = y[0:half]
        o_ref[1, :, j, :] = y[half:2 * half]


def _idft(cr, ci, s4, *, n_hi, t_lo=16):
    p, _, _, c = cr.shape
    half = n_hi // 2
    spec = pl.BlockSpec((None, n_hi, t_lo, c), lambda q, j: (q, 0, j, 0))
    return pl.pallas_call(
        _idft_kernel, grid=(p, N_LO // t_lo),
        in_specs=[spec, spec, pl.BlockSpec(s4.shape, lambda q, j: (0, 0))],
        out_specs=pl.BlockSpec((2, half, t_lo, c), lambda q, j: (q, 0, j, 0)),
        out_shape=jax.ShapeDtypeStruct((2 * p, half, N_LO, c), F32),
        compiler_params=_params("parallel", "parallel"), name="hyena_idft",
    )(cr, ci, s4)


def _hy_post_kernel(y_ref, vv_ref, x0_ref, fb_ref, og_ref, mg_ref, o_ref):
    vv = vv_ref[...]
    y = (y_ref[...] + vv * fb_ref[...]) * x0_ref[...]
    ms = _hdot(y * y, mg_ref[...])
    o_ref[...] = (y * lax.rsqrt(ms + RMS_EPS) * og_ref[...]).astype(o_ref.dtype)


def _hy_post(y, vv, x0, filt_bias, out_g, *, tt=512):
    b, s, c = vv.shape
    gdim = c // N_HYENA_GROUPS
    grp = np.arange(c) // gdim
    mg = jnp.asarray((grp[:, None] == grp[None, :]).astype(np.float32) / gdim)
    row = pl.BlockSpec((None, tt, c), lambda b_, i: (b_, i, 0))
    vec = pl.BlockSpec((1, c), lambda b_, i: (0, 0))
    return pl.pallas_call(
        _hy_post_kernel, grid=(b, s // tt),
        in_specs=[row, row, row, vec, vec, pl.BlockSpec((c, c), lambda b_, i: (0, 0))],
        out_specs=row, out_shape=jax.ShapeDtypeStruct((b, s, c), BF16),
        compiler_params=_params("parallel", "parallel"), name="hyena_post",
    )(y, vv, x0, filt_bias.reshape(1, c), out_g.reshape(1, c), mg)


def _filter_positions(s):
    t = jnp.linspace(0.0, 1.0, s, dtype=F32)[:, None]
    bands = (HYENA_EMB_DIM - 1) // 2
    w = 2.0 * math.pi * jnp.arange(s, dtype=F32)[:, None] / s
    f = jnp.linspace(1e-4, bands - 1, bands, dtype=F32)[None, :]
    fw = f * w
    z = jnp.concatenate([t, jnp.cos(fw), -jnp.sin(fw)], axis=-1)
    order = np.concatenate([np.arange(s), [0], np.arange(s - 1, 0, -1)])
    return z[order]


def _hyena(z_hy, conv_w, conv_b, w1, b1, w2, b2, w3, b3, w4, freq, filt_bias, out_g):
    b, s, c3 = z_hy.shape
    c = c3 // 3
    n_hi = 2 * s // N_LO
    half = n_hi // 2
    assert b % 2 == 0 and n_hi * N_LO == 2 * s
    s1_pair, s1_real, g2, h2, s4 = _dft_tables(n_hi)

    max_decay = math.log(HYENA_DECAY_TARGET) / HYENA_FAST_DECAY
    min_decay = math.log(HYENA_DECAY_TARGET) / HYENA_SLOW_DECAY
    deltas = jnp.linspace(min_decay, max_decay, c, dtype=F32)
    taps = _filter_taps(_filter_positions(s), w1, b1, w2, b2, w3, b3, w4, freq, deltas, n_hi=n_hi)
    far, fai = _dft1_real(taps, s1_real, n_hi=n_hi)
    kfr, kfi = _dft2_filter(far, fai, g2, n_hi=n_hi)

    vb, vv, x0 = _hy_pre(z_hy, conv_w, conv_b)
    ar, ai = _dft1_pair(vb.reshape(b, half, N_LO, c), s1_pair, n_hi=n_hi)
    cr, ci = _spectral(ar, ai, g2, h2, kfr, kfi, n_hi=n_hi)
    y = _idft(cr, ci, s4, n_hi=n_hi).reshape(b, s, c)
    return _hy_post(y, vv, x0, filt_bias, out_g)


def kernel(x, rel_bias, ffn1_norm, ffn1_w_gate, ffn1_w_up, ffn1_w_down, mix_norm, w_in,
           lambda_q1, lambda_k1, lambda_q2, lambda_k2, diff_subln,
           hy_conv_w, hy_conv_b, hy_f_w1, hy_f_b1, hy_f_w2, hy_f_b2, hy_f_w3, hy_f_b3,
           hy_f_w4, hy_f_freq, hy_bias, hy_out_norm, w_out,
           ffn2_norm, ffn2_w_gate, ffn2_w_up, ffn2_w_down, final_norm):
    b, s, d = x.shape
    depth = w_in.shape[0]
    d_att = diff_subln.shape[1] * rel_bias.shape[1]
    n_heads = rel_bias.shape[1]
    attn_tile = min(512, s)
    attn_unroll = 2
    bf = lambda a: a.astype(BF16)

    xf = x.reshape(b * s, d)
    for l in range(depth):
        last = l == depth - 1
        xf = _ffn(xf, ffn1_norm[l], bf(ffn1_w_gate[l]), bf(ffn1_w_up[l]), bf(ffn1_w_down[l]))
        qkv, z_hy = _mix(xf, mix_norm[l], bf(w_in[l]), d_att=d_att)
        lambda_init = 0.8 - 0.6 * math.exp(-0.3 * l)
        att = _attention(qkv.reshape(b, s, -1), _bias_tiles(rel_bias, attn_tile),
                         lambda_q1[l], lambda_k1[l], lambda_q2[l], lambda_k2[l], diff_subln[l],
                         n_heads=n_heads, lambda_init=lambda_init, t=attn_tile, unroll=attn_unroll)
        hy = _hyena(z_hy.reshape(b, s, -1), hy_conv_w[l], hy_conv_b[l], hy_f_w1[l], hy_f_b1[l],
                    hy_f_w2[l], hy_f_b2[l], hy_f_w3[l], hy_f_b3[l], hy_f_w4[l], hy_f_freq[l],
                    hy_bias[l], hy_out_norm[l])
        wo = bf(w_out[l])
        xf = _ffn(xf, ffn2_norm[l], bf(ffn2_w_gate[l]), bf(ffn2_w_up[l]), bf(ffn2_w_down[l]),
                  pre=(att.reshape(b * s, -1), hy.reshape(b * s, -1), wo[:d_att], wo[d_att:]),
                  final_g=final_norm if last else None)
    if depth == 0:
        raise ValueError("depth must be positive")
    return xf.reshape(b, s, d)
```

```python
import functools
import math

import numpy as np
import jax
import jax.numpy as jnp
from jax import lax
from jax.experimental import pallas as pl
from jax.experimental.pallas import tpu as pltpu

F32 = jnp.float32
BF16 = jnp.bfloat16

RMS_EPS = 1e-6
LANES = 128
DIFF_HEAD_DIM = 64
V_HEAD_DIM = 2 * DIFF_HEAD_DIM
REL_BUCKETS = 32
REL_MAX_DIST = 128
N_HYENA_GROUPS = 8
HYENA_EMB_DIM = 33
HYENA_DECAY_TARGET = 1e-2
HYENA_FAST_DECAY = 0.3
HYENA_SLOW_DECAY = 1.5
N_LO = 128
LOG2_E = math.log2(math.e)
VMEM_LIMIT = 56 * 1024 * 1024


def _params(*sem):
    return pltpu.CompilerParams(dimension_semantics=sem, vmem_limit_bytes=VMEM_LIMIT)


def _resident(shape):
    return pl.BlockSpec(shape, lambda *_: (0,) * len(shape), pipeline_mode=pl.Buffered(1))


def _rms(x, g):
    return x * lax.rsqrt(jnp.mean(x * x, axis=-1, keepdims=True) + RMS_EPS) * g


def _dot(a, b):
    return jnp.dot(a, b, preferred_element_type=F32)


def _ffn_kernel(*refs, ff_chunk, n_chunk, pre, post):
    if pre:
        x_ref, att_ref, hy_ref, woa_ref, woh_ref = refs[:5]
        refs = refs[5:]
    else:
        x_ref = refs[0]
        refs = refs[1:]
    g_ref, wg_ref, wu_ref, wd_ref = refs[:4]
    refs = refs[4:]
    if post:
        fg_ref, o_ref = refs
    else:
        (o_ref,) = refs

    x = x_ref[...]
    if pre:
        x = x + _dot(att_ref[...], woa_ref[...]) + _dot(hy_ref[...], woh_ref[...])
    xn = _rms(x, g_ref[...]).astype(BF16)
    acc = jnp.zeros(x.shape, F32)
    for c in range(n_chunk):
        sl = slice(c * ff_chunk, (c + 1) * ff_chunk)
        gate = _dot(xn, wg_ref[:, sl])
        up = _dot(xn, wu_ref[:, sl])
        h = (jax.nn.silu(gate) * up).astype(BF16)
        acc = acc + _dot(h, wd_ref[sl, :])
    y = x + 0.5 * acc
    if post:
        y = _rms(y, fg_ref[...])
    o_ref[...] = y


def _ffn(x, norm_g, wg, wu, wd, *, pre=None, final_g=None, tm=512, ff_chunk=256):
    m, d = x.shape
    dff = wg.shape[1]
    n_chunk = dff // ff_chunk
    assert n_chunk * ff_chunk == dff and m % tm == 0
    row = lambda w: pl.BlockSpec((tm, w), lambda i: (i, 0))
    args, specs = [x], [row(d)]
    if pre is not None:
        att, hy, woa, woh = pre
        args += [att, hy, woa, woh]
        specs += [row(att.shape[1]), row(hy.shape[1]), _resident(woa.shape), _resident(woh.shape)]
    args += [norm_g.reshape(1, d), wg, wu, wd]
    specs += [_resident((1, d)), _resident(wg.shape), _resident(wu.shape), _resident(wd.shape)]
    if final_g is not None:
        args.append(final_g.reshape(1, d))
        specs.append(_resident((1, d)))
    kern = functools.partial(_ffn_kernel, ff_chunk=ff_chunk, n_chunk=n_chunk,
                             pre=pre is not None, post=final_g is not None)
    return pl.pallas_call(
        kern, grid=(m // tm,), in_specs=specs, out_specs=row(d),
        out_shape=jax.ShapeDtypeStruct((m, d), F32),
        compiler_params=_params("parallel"), name="ffn_pre" if pre is not None else "ffn",
    )(*args)


def _mix_kernel(x_ref, g_ref, w_ref, qkv_ref, z_ref, *, d_att, col_chunk, scale):
    xn = _rms(x_ref[...], g_ref[...]).astype(BF16)
    n_att = 3 * d_att
    for c0 in range(0, w_ref.shape[1], col_chunk):
        p = _dot(xn, w_ref[:, c0:c0 + col_chunk])
        if c0 < n_att:
            if c0 < d_att:
                p = p * scale
            qkv_ref[:, c0:c0 + col_chunk] = p.astype(BF16)
        else:
            z_ref[:, c0 - n_att:c0 - n_att + col_chunk] = p.astype(z_ref.dtype)


def _mix(x, norm_g, w_in, *, d_att, tm=512, col_chunk=512):
    m, d = x.shape
    d_in = w_in.shape[1]
    n_att = 3 * d_att
    assert d_att % col_chunk == 0 and d_in % col_chunk == 0 and m % tm == 0
    kern = functools.partial(_mix_kernel, d_att=d_att, col_chunk=col_chunk,
                             scale=DIFF_HEAD_DIM ** -0.5 * LOG2_E)
    return pl.pallas_call(
        kern, grid=(m // tm,),
        in_specs=[pl.BlockSpec((tm, d), lambda i: (i, 0)), _resident((1, d)), _resident(w_in.shape)],
        out_specs=[pl.BlockSpec((tm, n_att), lambda i: (i, 0)),
                   pl.BlockSpec((tm, d_in - n_att), lambda i: (i, 0))],
        out_shape=[jax.ShapeDtypeStruct((m, n_att), BF16),
                   jax.ShapeDtypeStruct((m, d_in - n_att), BF16)],
        compiler_params=_params("parallel"), name="mix",
    )(x, norm_g.reshape(1, d), w_in)


N_BIAS_TILES = 5


def _bias_kernel(tab_ref, o_ref, *, t):
    h = pl.program_id(0)
    d = pl.program_id(1) - N_BIAS_TILES // 2
    half = REL_BUCKETS // 2
    max_exact = half // 2
    rel = (lax.broadcasted_iota(jnp.int32, (t, t), 1) - lax.broadcasted_iota(jnp.int32, (t, t), 0)
           + d * t)
    ret = jnp.where(rel > 0, half, 0)
    n = jnp.abs(rel)
    nf = jnp.maximum(n, 1).astype(F32)
    large = max_exact + (jnp.log(nf / max_exact) / math.log(REL_MAX_DIST / max_exact)
                         * (half - max_exact)).astype(jnp.int32)
    large = jnp.minimum(large, half - 1)
    bucket = ret + jnp.where(n < max_exact, n, large)
    out = jnp.zeros((t, t), F32)
    for b in range(REL_BUCKETS):
        out = jnp.where(bucket == b, tab_ref[b, h], out)
    o_ref[...] = out * LOG2_E


def _bias_tiles(rel_bias, t):
    nh = rel_bias.shape[1]
    return pl.pallas_call(
        functools.partial(_bias_kernel, t=t), grid=(nh, N_BIAS_TILES),
        in_specs=[pl.BlockSpec(memory_space=pltpu.SMEM)],
        out_specs=pl.BlockSpec((None, None, t, t), lambda h, d: (h, d, 0, 0)),
        out_shape=jax.ShapeDtypeStruct((nh, N_BIAS_TILES, t, t), F32),
        compiler_params=_params("parallel", "parallel"), name="rel_bias_tiles",
    )(rel_bias)


def _attn_kernel(q_ref, k_ref, v_ref, bt_ref, lq1_ref, lk1_ref, lq2_ref, lk2_ref, sg_ref,
                 o_ref, q2_ref, m_ref, acc_ref, vx_ref, *, t, nk, unroll, lambda_init):
    i = pl.program_id(2)
    dv = v_ref.shape[1]

    @pl.when(i == 0)
    def _():
        vx_ref[:, 0:dv] = v_ref[...]
        ones_lane = lax.broadcasted_iota(jnp.int32, (v_ref.shape[0], dv), 1) == 0
        vx_ref[:, dv:2 * dv] = jnp.where(ones_lane, 1.0, 0.0).astype(vx_ref.dtype)

    q = q_ref[...]
    lane = lax.broadcasted_iota(jnp.int32, q.shape, 1)
    zero = jnp.zeros_like(q)
    q2_ref[0:t, :] = jnp.where(lane < DIFF_HEAD_DIM, q, zero)
    q2_ref[t:2 * t, :] = jnp.where(lane >= DIFF_HEAD_DIM, q, zero)
    m_ref[...] = jnp.full(m_ref.shape, -jnp.inf, F32)
    acc_ref[...] = jnp.zeros(acc_ref.shape, F32)
    far = N_BIAS_TILES // 2

    def tile(j, rows, m_prev, acc_prev):
        row0 = pl.multiple_of(j * t, t)
        k = k_ref[pl.ds(row0, t), :]
        vx = vx_ref[pl.ds(row0, t), :]
        s = lax.dot_general(q2_ref[rows, :], k, (((1,), (1,)), ((), ())), preferred_element_type=F32)
        bias = bt_ref[jnp.clip(j - i, -far, far) + far]
        cols = [s[:, c0:c0 + LANES] + bias[:, c0:c0 + LANES] for c0 in range(0, t, LANES)]
        rm = functools.reduce(jnp.maximum, cols)
        m_new = jnp.maximum(m_prev, jnp.max(rm, axis=1, keepdims=True))
        alpha = jnp.exp2(m_prev - m_new)
        p = jnp.concatenate([jnp.exp2(col - m_new).astype(BF16) for col in cols], axis=1)
        pv = _dot(p, vx)
        acc_new = jnp.concatenate([alpha * acc_prev[:, 0:dv] + pv[:, 0:dv],
                                   alpha * acc_prev[:, dv:2 * dv] + pv[:, dv:2 * dv]], axis=1)
        return m_new, acc_new

    halves = (slice(0, t), slice(t, 2 * t))

    def trip(jj, _):
        state = [(m_ref[rows, :], acc_ref[rows, :]) for rows in halves]
        for u in range(unroll):
            state = [tile(jj * unroll + u, rows, *st) for rows, st in zip(halves, state)]
        for rows, (m_new, acc_new) in zip(halves, state):
            m_ref[rows, :] = m_new
            acc_ref[rows, :] = acc_new

    lax.fori_loop(0, nk // unroll, trip, None)

    lam = (jnp.exp(jnp.sum(lq1_ref[...] * lk1_ref[...])) - jnp.exp(jnp.sum(lq2_ref[...] * lk2_ref[...]))
           + lambda_init)
    o = acc_ref[:, 0:dv] / acc_ref[:, dv:dv + 1]
    o = o[0:t] - lam * o[t:2 * t]
    o = _rms(o, sg_ref[...]) * (1.0 - lambda_init)
    o_ref[...] = o.astype(o_ref.dtype)


def _attention(qkv, bias_tiles, lq1, lk1, lq2, lk2, subln, *, n_heads, lambda_init, t, unroll):
    b, s, _ = qkv.shape
    nk = s // t
    assert nk * t == s and t % LANES == 0 and t + 1 >= 91 and nk % unroll == 0
    kern = functools.partial(_attn_kernel, t=t, nk=nk, unroll=unroll, lambda_init=lambda_init)
    vec = lambda n: pl.BlockSpec((1, n), lambda b_, h, i: (0, 0))
    return pl.pallas_call(
        kern, grid=(b, n_heads, nk),
        in_specs=[
            pl.BlockSpec((None, t, V_HEAD_DIM), lambda b_, h, i: (b_, i, h)),
            pl.BlockSpec((None, s, V_HEAD_DIM), lambda b_, h, i: (b_, 0, n_heads + h)),
            pl.BlockSpec((None, s, V_HEAD_DIM), lambda b_, h, i: (b_, 0, 2 * n_heads + h)),
            pl.BlockSpec((None, N_BIAS_TILES, t, t), lambda b_, h, i: (h, 0, 0, 0)),
            vec(DIFF_HEAD_DIM), vec(DIFF_HEAD_DIM), vec(DIFF_HEAD_DIM), vec(DIFF_HEAD_DIM),
            vec(V_HEAD_DIM),
        ],
        out_specs=pl.BlockSpec((None, t, V_HEAD_DIM), lambda b_, h, i: (b_, i, h)),
        out_shape=jax.ShapeDtypeStruct((b, s, n_heads * V_HEAD_DIM), BF16),
        scratch_shapes=[pltpu.VMEM((2 * t, V_HEAD_DIM), BF16), pltpu.VMEM((2 * t, LANES), F32),
                        pltpu.VMEM((2 * t, 2 * V_HEAD_DIM), F32), pltpu.VMEM((s, 2 * V_HEAD_DIM), BF16)],
        compiler_params=_params("parallel", "parallel", "arbitrary"), name="diff_attention",
    )(qkv, qkv, qkv, bias_tiles, lq1.reshape(1, -1), lk1.reshape(1, -1),
      lq2.reshape(1, -1), lk2.reshape(1, -1), subln.reshape(1, -1))


def _dft_tables(n_hi):
    n = n_hi * N_LO
    half = n_hi // 2
    hi = np.arange(n_hi)
    lo = np.arange(N_LO)
    f_hi = np.exp(-2j * np.pi * ((np.outer(hi, hi) % n_hi) / n_hi))
    fr, fi = f_hi.real, f_hi.imag
    s1_pair = np.block([[fr[:, :half], -fi[:, :half]], [fi[:, :half], fr[:, :half]]])
    s1_real = np.concatenate([fr, fi], axis=0)
    ph = (hi[:, None, None] * lo[None, None, :] + n_hi * lo[None, :, None] * lo[None, None, :]) % n
    g = np.exp(-2j * np.pi * ph / n)
    g2 = np.concatenate([np.concatenate([g.real, -g.imag], axis=2),
                         np.concatenate([g.imag, g.real], axis=2)], axis=1)
    h2 = np.swapaxes(g2, 1, 2)
    s4 = np.block([[fr[:half], fi[:half]], [-fi[:half], fr[:half]]]) / n
    cast = lambda a: jnp.asarray(a.astype(np.float32)).astype(BF16)
    return cast(s1_pair), cast(s1_real), cast(g2), cast(h2), cast(s4)


HALO_ROWS = 16


def _hy_pre_kernel(z_ref, zp_ref, zn_ref, w_ref, b_ref, vv_ref, x0_ref, *, c, n_blk):
    i = pl.program_id(1)
    z = z_ref[...].astype(F32)
    rows = z.shape[0]
    r = lax.broadcasted_iota(jnp.int32, z.shape, 0)
    prev_row = jnp.where(i > 0, zp_ref[HALO_ROWS - 1:HALO_ROWS, :].astype(F32), 0.0)
    next_row = jnp.where(i < n_blk - 1, zn_ref[0:1, :].astype(F32), 0.0)
    z_m1 = jnp.where(r == 0, prev_row, pltpu.roll(z, 1, 0))
    z_p1 = jnp.where(r == rows - 1, next_row, pltpu.roll(z, rows - 1, 0))
    u = b_ref[...] + z_m1 * w_ref[0:1, :] + z * w_ref[1:2, :] + z_p1 * w_ref[2:3, :]
    vv_ref[...] = (u[:, 2 * c:3 * c] * u[:, c:2 * c]).astype(vv_ref.dtype)
    x0_ref[...] = u[:, 0:c].astype(x0_ref.dtype)


def _hy_pre(z, conv_w, conv_b, *, tt=512):
    b, s, c3 = z.shape
    c = c3 // 3
    n_blk = s // tt
    assert n_blk * tt == s and tt % HALO_ROWS == 0
    kern = functools.partial(_hy_pre_kernel, c=c, n_blk=n_blk)
    sub = tt // HALO_ROWS
    last = s // HALO_ROWS - 1
    row = pl.BlockSpec((None, tt, c), lambda b_, i: (b_, i, 0))
    return pl.pallas_call(
        kern, grid=(b, n_blk),
        in_specs=[
            pl.BlockSpec((None, tt, c3), lambda b_, i: (b_, i, 0)),
            pl.BlockSpec((None, HALO_ROWS, c3), lambda b_, i: (b_, jnp.maximum(i * sub - 1, 0), 0)),
            pl.BlockSpec((None, HALO_ROWS, c3), lambda b_, i: (b_, jnp.minimum((i + 1) * sub, last), 0)),
            pl.BlockSpec((3, c3), lambda b_, i: (0, 0)),
            pl.BlockSpec((1, c3), lambda b_, i: (0, 0)),
        ],
        out_specs=[row, row],
        out_shape=[jax.ShapeDtypeStruct((b, s, c), BF16), jax.ShapeDtypeStruct((b, s, c), BF16)],
        compiler_params=_params("parallel", "parallel"), name="hyena_pre",
    )(z, z, z, conv_w, conv_b.reshape(1, c3))


def _hdot(a, b):
    return jnp.dot(a, b, preferred_element_type=F32, precision=lax.Precision.HIGHEST)


def _filter_kernel(z_ref, w1_ref, b1_ref, w2_ref, b2_ref, w3_ref, b3_ref, w4_ref, fr_ref, dl_ref,
                   o_ref, *, c, n_hi):
    i = pl.program_id(0)
    z = z_ref[...]
    fr = fr_ref[...]
    a = jnp.sin(fr * (_hdot(z, w1_ref[...]) + b1_ref[...]))
    a = jnp.sin(fr * (_hdot(a, w2_ref[...]) + b2_ref[...]))
    a = jnp.sin(fr * (_hdot(a, w3_ref[...]) + b3_ref[...]))
    hh = _hdot(a, w4_ref[...])
    decay = jnp.exp(-z[:, 0:1] * jnp.abs(dl_ref[...]))
    fwd = i < n_hi // 2
    hsel = jnp.where(fwd, hh[:, 0:c], hh[:, c:2 * c]) * decay
    r = lax.broadcasted_iota(jnp.int32, hsel.shape, 0)
    gap = jnp.logical_and(i == n_hi // 2, r == 0)
    o_ref[...] = jnp.where(gap, 0.0, hsel).astype(o_ref.dtype)


def _filter_taps(zk, w1, b1, w2, b2, w3, b3, w4, freq, deltas, *, n_hi):
    c = deltas.shape[0]
    fo = LANES
    pad2 = lambda a, r, cc: jnp.pad(a, ((0, r - a.shape[0]), (0, cc - a.shape[1])))
    rowp = lambda v: jnp.pad(v, (0, fo - v.shape[0])).reshape(1, fo)
    args = [pad2(zk, zk.shape[0], LANES), pad2(w1, LANES, fo), rowp(b1), pad2(w2, fo, fo), rowp(b2),
            pad2(w3, fo, fo), rowp(b3), pad2(w4, fo, 2 * c), rowp(freq), deltas.reshape(1, c)]
    specs = [pl.BlockSpec((N_LO, LANES), lambda i: (i, 0))] + [
        pl.BlockSpec(a.shape, lambda i: (0, 0)) for a in args[1:]]
    return pl.pallas_call(
        functools.partial(_filter_kernel, c=c, n_hi=n_hi), grid=(n_hi,),
        in_specs=specs,
        out_specs=pl.BlockSpec((None, N_LO, c), lambda i: (i, 0, 0)),
        out_shape=jax.ShapeDtypeStruct((n_hi, N_LO, c), BF16),
        compiler_params=_params("parallel"), name="hyena_filter_taps",
    )(*args)


def _dft1_kernel(*refs, n_in):
    x_refs, f_ref, ar_ref, ai_ref = refs[:n_in], refs[n_in], refs[n_in + 1], refs[n_in + 2]
    n_out = ar_ref.shape[0]
    for j in range(ar_ref.shape[1]):
        rows = [r[:, j, :] for r in x_refs]
        x = jnp.concatenate(rows, axis=0) if n_in > 1 else rows[0]
        a = _dot(f_ref[...], x)
        ar_ref[:, j, :] = a[0:n_out].astype(ar_ref.dtype)
        ai_ref[:, j, :] = a[n_out:2 * n_out].astype(ai_ref.dtype)


def _dft1_pair(xv, f_mat, *, n_hi, t_lo=16):
    b, half, _, c = xv.shape
    spec = lambda off: pl.BlockSpec((None, half, t_lo, c), lambda p, j: (2 * p + off, 0, j, 0))
    out = pl.BlockSpec((None, n_hi, t_lo, c), lambda p, j: (p, 0, j, 0))
    shp = jax.ShapeDtypeStruct((b // 2, n_hi, N_LO, c), BF16)
    return pl.pallas_call(
        functools.partial(_dft1_kernel, n_in=2), grid=(b // 2, N_LO // t_lo),
        in_specs=[spec(0), spec(1), pl.BlockSpec(f_mat.shape, lambda p, j: (0, 0))],
        out_specs=[out, out], out_shape=[shp, shp],
        compiler_params=_params("parallel", "parallel"), name="hyena_dft1",
    )(xv, xv, f_mat)


def _dft1_real(taps, f_mat, *, n_hi, t_lo=16):
    _, _, c = taps.shape
    blk = pl.BlockSpec((n_hi, t_lo, c), lambda j: (0, j, 0))
    shp = jax.ShapeDtypeStruct((n_hi, N_LO, c), BF16)
    return pl.pallas_call(
        functools.partial(_dft1_kernel, n_in=1), grid=(N_LO // t_lo,),
        in_specs=[blk, pl.BlockSpec(f_mat.shape, lambda j: (0, 0))],
        out_specs=[blk, blk], out_shape=[shp, shp],
        compiler_params=_params("parallel"), name="hyena_filter_dft1",
    )(taps, f_mat)


def _dft2_filter_kernel(ar_ref, ai_ref, g_ref, kr_ref, ki_ref):
    a = jnp.concatenate([ar_ref[...], ai_ref[...]], axis=0)
    x = _dot(g_ref[...], a)
    kr_ref[...] = x[0:N_LO]
    ki_ref[...] = x[N_LO:2 * N_LO]


def _dft2_filter(ar, ai, g2, *, n_hi):
    c = ar.shape[-1]
    blk = pl.BlockSpec((None, N_LO, c), lambda k: (k, 0, 0))
    shp = jax.ShapeDtypeStruct((n_hi, N_LO, c), F32)
    return pl.pallas_call(
        _dft2_filter_kernel, grid=(n_hi,),
        in_specs=[blk, blk, pl.BlockSpec((None, 2 * N_LO, 2 * N_LO), lambda k: (k, 0, 0))],
        out_specs=[blk, blk], out_shape=[shp, shp],
        compiler_params=_params("parallel"), name="hyena_filter_dft2",
    )(ar, ai, g2)


def _spectral_kernel(ar_ref, ai_ref, g_ref, h_ref, kr_ref, ki_ref, cr_ref, ci_ref):
    a = jnp.concatenate([ar_ref[...], ai_ref[...]], axis=0)
    x = _dot(g_ref[...], a)
    xr, xi = x[0:N_LO], x[N_LO:2 * N_LO]
    kr, ki = kr_ref[...], ki_ref[...]
    y = jnp.concatenate([xr * kr - xi * ki, xr * ki + xi * kr], axis=0).astype(BF16)
    cc = _dot(h_ref[...], y)
    cr_ref[...] = cc[0:N_LO].astype(cr_ref.dtype)
    ci_ref[...] = cc[N_LO:2 * N_LO].astype(ci_ref.dtype)


def _spectral(ar, ai, g2, h2, kfr, kfi, *, n_hi):
    p, _, _, c = ar.shape
    blk = pl.BlockSpec((None, None, N_LO, c), lambda q, k: (q, k, 0, 0))
    mat = pl.BlockSpec((None, 2 * N_LO, 2 * N_LO), lambda q, k: (k, 0, 0))
    kf = pl.BlockSpec((None, N_LO, c), lambda q, k: (k, 0, 0))
    shp = jax.ShapeDtypeStruct((p, n_hi, N_LO, c), BF16)
    return pl.pallas_call(
        _spectral_kernel, grid=(p, n_hi),
        in_specs=[blk, blk, mat, mat, kf, kf],
        out_specs=[blk, blk], out_shape=[shp, shp],
        compiler_params=_params("parallel", "parallel"), name="hyena_spectral",
    )(ar, ai, g2, h2, kfr, kfi)


def _idft_kernel(cr_ref, ci_ref, m_ref, o_ref):
    half = o_ref.shape[1]
    for j in range(o_ref.shape[2]):
        cc = jnp.concatenate([cr_ref[:, j, :], ci_ref[:, j, :]], axis=0)
        y = _dot(m_ref[...], cc)
        o_ref[0, :, j, :] = y[0:half]
        o_ref[1, :, j, :] = y[half:2 * half]


def _idft(cr, ci, s4, *, n_hi, t_lo=16):
    p, _, _, c = cr.shape
    half = n_hi // 2
    spec = pl.BlockSpec((None, n_hi, t_lo, c), lambda q, j: (q, 0, j, 0))
    return pl.pallas_call(
        _idft_kernel, grid=(p, N_LO // t_lo),
        in_specs=[spec, spec, pl.BlockSpec(s4.shape, lambda q, j: (0, 0))],
        out_specs=pl.BlockSpec((2, half, t_lo, c), lambda q, j: (q, 0, j, 0)),
        out_shape=jax.ShapeDtypeStruct((2 * p, half, N_LO, c), F32),
        compiler_params=_params("parallel", "parallel"), name="hyena_idft",
    )(cr, ci, s4)


def _hy_post_kernel(y_ref, vv_ref, x0_ref, fb_ref, og_ref, mg_ref, o_ref):
    vv = vv_ref[...].astype(F32)
    y = (y_ref[...] + vv * fb_ref[...]) * x0_ref[...].astype(F32)
    ms = _hdot(y * y, mg_ref[...])
    o_ref[...] = (y * lax.rsqrt(ms + RMS_EPS) * og_ref[...]).astype(o_ref.dtype)


def _hy_post(y, vv, x0, filt_bias, out_g, *, tt=512):
    b, s, c = vv.shape
    gdim = c // N_HYENA_GROUPS
    grp = np.arange(c) // gdim
    mg = jnp.asarray((grp[:, None] == grp[None, :]).astype(np.float32) / gdim)
    row = pl.BlockSpec((None, tt, c), lambda b_, i: (b_, i, 0))
    vec = pl.BlockSpec((1, c), lambda b_, i: (0, 0))
    return pl.pallas_call(
        _hy_post_kernel, grid=(b, s // tt),
        in_specs=[row, row, row, vec, vec, pl.BlockSpec((c, c), lambda b_, i: (0, 0))],
        out_specs=row, out_shape=jax.ShapeDtypeStruct((b, s, c), BF16),
        compiler_params=_params("parallel", "parallel"), name="hyena_post",
    )(y, vv, x0, filt_bias.reshape(1, c), out_g.reshape(1, c), mg)


def _filter_positions(s):
    t = jnp.linspace(0.0, 1.0, s, dtype=F32)[:, None]
    bands = (HYENA_EMB_DIM - 1) // 2
    w = 2.0 * math.pi * jnp.arange(s, dtype=F32)[:, None] / s
    f = jnp.linspace(1e-4, bands - 1, bands, dtype=F32)[None, :]
    fw = f * w
    z = jnp.concatenate([t, jnp.cos(fw), -jnp.sin(fw)], axis=-1)
    order = np.concatenate([np.arange(s), [0], np.arange(s - 1, 0, -1)])
    return z[order]


def _hyena(z_hy, conv_w, conv_b, w1, b1, w2, b2, w3, b3, w4, freq, filt_bias, out_g):
    b, s, c3 = z_hy.shape
    c = c3 // 3
    n_hi = 2 * s // N_LO
    half = n_hi // 2
    assert b % 2 == 0 and n_hi * N_LO == 2 * s
    s1_pair, s1_real, g2, h2, s4 = _dft_tables(n_hi)

    max_decay = math.log(HYENA_DECAY_TARGET) / HYENA_FAST_DECAY
    min_decay = math.log(HYENA_DECAY_TARGET) / HYENA_SLOW_DECAY
    deltas = jnp.linspace(min_decay, max_decay, c, dtype=F32)
    taps = _filter_taps(_filter_positions(s), w1, b1, w2, b2, w3, b3, w4, freq, deltas, n_hi=n_hi)
    far, fai = _dft1_real(taps, s1_real, n_hi=n_hi)
    kfr, kfi = _dft2_filter(far, fai, g2, n_hi=n_hi)

    vv, x0 = _hy_pre(z_hy, conv_w, conv_b)
    ar, ai = _dft1_pair(vv.reshape(b, half, N_LO, c), s1_pair, n_hi=n_hi)
    cr, ci = _spectral(ar, ai, g2, h2, kfr, kfi, n_hi=n_hi)
    y = _idft(cr, ci, s4, n_hi=n_hi).reshape(b, s, c)
    return _hy_post(y, vv, x0, filt_bias, out_g)


def kernel(x, rel_bias, ffn1_norm, ffn1_w_gate, ffn1_w_up, ffn1_w_down, mix_norm, w_in,
           lambda_q1, lambda_k1, lambda_q2, lambda_k2, diff_subln,
           hy_conv_w, hy_conv_b, hy_f_w1, hy_f_b1, hy_f_w2, hy_f_b2, hy_f_w3, hy_f_b3,
           hy_f_w4, hy_f_freq, hy_bias, hy_out_norm, w_out,
           ffn2_norm, ffn2_w_gate, ffn2_w_up, ffn2_w_down, final_norm):
    b, s, d = x.shape
    depth = w_in.shape[0]
    d_att = diff_subln.shape[1] * rel_bias.shape[1]
    n_heads = rel_bias.shape[1]
    attn_tile = min(512, s)
    attn_unroll = 4
    bf = lambda a: a.astype(BF16)

    xf = x.reshape(b * s, d)
    for l in range(depth):
        last = l == depth - 1
        xf = _ffn(xf, ffn1_norm[l], bf(ffn1_w_gate[l]), bf(ffn1_w_up[l]), bf(ffn1_w_down[l]))
        qkv, z_hy = _mix(xf, mix_norm[l], bf(w_in[l]), d_att=d_att)
        lambda_init = 0.8 - 0.6 * math.exp(-0.3 * l)
        att = _attention(qkv.reshape(b, s, -1), _bias_tiles(rel_bias, attn_tile),
                         lambda_q1[l], lambda_k1[l], lambda_q2[l], lambda_k2[l], diff_subln[l],
                         n_heads=n_heads, lambda_init=lambda_init, t=attn_tile, unroll=attn_unroll)
        hy = _hyena(z_hy.reshape(b, s, -1), hy_conv_w[l], hy_conv_b[l], hy_f_w1[l], hy_f_b1[l],
                    hy_f_w2[l], hy_f_b2[l], hy_f_w3[l], hy_f_b3[l], hy_f_w4[l], hy_f_freq[l],
                    hy_bias[l], hy_out_norm[l])
        wo = bf(w_out[l])
        xf = _ffn(xf, ffn2_norm[l], bf(ffn2_w_gate[l]), bf(ffn2_w_up[l]), bf(ffn2_w_down[l]),
                  pre=(att.reshape(b * s, -1), hy.reshape(b * s, -1), wo[:d_att], wo[d_att:]),
                  final_g=final_norm if last else None)
    if depth == 0:
        raise ValueError("depth must be positive")
    return xf.reshape(b, s, d)
```

```python
import functools
import math

import numpy as np
import jax
import jax.numpy as jnp
from jax import lax
from jax.experimental import pallas as pl
from jax.experimental.pallas import tpu as pltpu

F32 = jnp.float32
BF16 = jnp.bfloat16

RMS_EPS = 1e-6
LANES = 128
DIFF_HEAD_DIM = 64
V_HEAD_DIM = 2 * DIFF_HEAD_DIM
REL_BUCKETS = 32
REL_MAX_DIST = 128
N_HYENA_GROUPS = 8
HYENA_EMB_DIM = 33
HYENA_DECAY_TARGET = 1e-2
HYENA_FAST_DECAY = 0.3
HYENA_SLOW_DECAY = 1.5
N_LO = 128
LOG2_E = math.log2(math.e)
VMEM_LIMIT = 56 * 1024 * 1024


def _params(*sem):
    return pltpu.CompilerParams(dimension_semantics=sem, vmem_limit_bytes=VMEM_LIMIT)


def _resident(shape):
    return pl.BlockSpec(shape, lambda *_: (0,) * len(shape), pipeline_mode=pl.Buffered(1))


def _rms(x, g):
    return x * lax.rsqrt(jnp.mean(x * x, axis=-1, keepdims=True) + RMS_EPS) * g


def _dot(a, b):
    return jnp.dot(a, b, preferred_element_type=F32)


def _ffn_kernel(*refs, ff_chunk, n_chunk, pre, post):
    if pre:
        x_ref, att_ref, hy_ref, woa_ref, woh_ref = refs[:5]
        refs = refs[5:]
    else:
        x_ref = refs[0]
        refs = refs[1:]
    g_ref, wg_ref, wu_ref, wd_ref = refs[:4]
    refs = refs[4:]
    if post:
        fg_ref, o_ref = refs
    else:
        (o_ref,) = refs

    x = x_ref[...]
    if pre:
        x = x + _dot(att_ref[...], woa_ref[...]) + _dot(hy_ref[...], woh_ref[...])
    xn = _rms(x, g_ref[...]).astype(BF16)
    acc = jnp.zeros(x.shape, F32)
    for c in range(n_chunk):
        sl = slice(c * ff_chunk, (c + 1) * ff_chunk)
        gate = _dot(xn, wg_ref[:, sl])
        up = _dot(xn, wu_ref[:, sl])
        h = (jax.nn.silu(gate) * up).astype(BF16)
        acc = acc + _dot(h, wd_ref[sl, :])
    y = x + 0.5 * acc
    if post:
        y = _rms(y, fg_ref[...])
    o_ref[...] = y


def _ffn(x, norm_g, wg, wu, wd, *, pre=None, final_g=None, tm=512, ff_chunk=256):
    m, d = x.shape
    dff = wg.shape[1]
    n_chunk = dff // ff_chunk
    assert n_chunk * ff_chunk == dff and m % tm == 0
    row = lambda w: pl.BlockSpec((tm, w), lambda i: (i, 0))
    args, specs = [x], [row(d)]
    if pre is not None:
        att, hy, woa, woh = pre
        args += [att, hy, woa, woh]
        specs += [row(att.shape[1]), row(hy.shape[1]), _resident(woa.shape), _resident(woh.shape)]
    args += [norm_g.reshape(1, d), wg, wu, wd]
    specs += [_resident((1, d)), _resident(wg.shape), _resident(wu.shape), _resident(wd.shape)]
    if final_g is not None:
        args.append(final_g.reshape(1, d))
        specs.append(_resident((1, d)))
    kern = functools.partial(_ffn_kernel, ff_chunk=ff_chunk, n_chunk=n_chunk,
                             pre=pre is not None, post=final_g is not None)
    return pl.pallas_call(
        kern, grid=(m // tm,), in_specs=specs, out_specs=row(d),
        out_shape=jax.ShapeDtypeStruct((m, d), F32),
        compiler_params=_params("parallel"), name="ffn_pre" if pre is not None else "ffn",
    )(*args)


def _mix_kernel(x_ref, g_ref, w_ref, qkv_ref, z_ref, *, d_att, col_chunk, scale):
    xn = _rms(x_ref[...], g_ref[...]).astype(BF16)
    n_att = 3 * d_att
    for c0 in range(0, w_ref.shape[1], col_chunk):
        p = _dot(xn, w_ref[:, c0:c0 + col_chunk])
        if c0 < n_att:
            if c0 < d_att:
                p = p * scale
            qkv_ref[:, c0:c0 + col_chunk] = p.astype(BF16)
        else:
            z_ref[:, c0 - n_att:c0 - n_att + col_chunk] = p.astype(z_ref.dtype)


def _mix(x, norm_g, w_in, *, d_att, tm=512, col_chunk=512):
    m, d = x.shape
    d_in = w_in.shape[1]
    n_att = 3 * d_att
    assert d_att % col_chunk == 0 and d_in % col_chunk == 0 and m % tm == 0
    kern = functools.partial(_mix_kernel, d_att=d_att, col_chunk=col_chunk,
                             scale=DIFF_HEAD_DIM ** -0.5 * LOG2_E)
    return pl.pallas_call(
        kern, grid=(m // tm,),
        in_specs=[pl.BlockSpec((tm, d), lambda i: (i, 0)), _resident((1, d)), _resident(w_in.shape)],
        out_specs=[pl.BlockSpec((tm, n_att), lambda i: (i, 0)),
                   pl.BlockSpec((tm, d_in - n_att), lambda i: (i, 0))],
        out_shape=[jax.ShapeDtypeStruct((m, n_att), BF16),
                   jax.ShapeDtypeStruct((m, d_in - n_att), BF16)],
        compiler_params=_params("parallel"), name="mix",
    )(x, norm_g.reshape(1, d), w_in)


N_BIAS_TILES = 5


def _bias_kernel(tab_ref, o_ref, *, t):
    h = pl.program_id(0)
    d = pl.program_id(1) - N_BIAS_TILES // 2
    half = REL_BUCKETS // 2
    max_exact = half // 2
    rel = lax.broadcasted_iota(jnp.int32, (8, 2 * t), 1) + (d - 1) * t
    ret = jnp.where(rel > 0, half, 0)
    n = jnp.abs(rel)
    nf = jnp.maximum(n, 1).astype(F32)
    large = max_exact + (jnp.log(nf / max_exact) / math.log(REL_MAX_DIST / max_exact)
                         * (half - max_exact)).astype(jnp.int32)
    large = jnp.minimum(large, half - 1)
    bucket = ret + jnp.where(n < max_exact, n, large)
    row = jnp.zeros(rel.shape, F32)
    for b in range(REL_BUCKETS):
        row = jnp.where(bucket == b, tab_ref[b, h], row)
    rows = jnp.broadcast_to(row[0:1, :] * LOG2_E, (t, 2 * t))
    o_ref[...] = pltpu.roll(rows, 0, 1, stride=1, stride_axis=0)[:, t:2 * t]


def _bias_tiles(rel_bias, t):
    nh = rel_bias.shape[1]
    return pl.pallas_call(
        functools.partial(_bias_kernel, t=t), grid=(nh, N_BIAS_TILES),
        in_specs=[pl.BlockSpec(memory_space=pltpu.SMEM)],
        out_specs=pl.BlockSpec((None, None, t, t), lambda h, d: (h, d, 0, 0)),
        out_shape=jax.ShapeDtypeStruct((nh, N_BIAS_TILES, t, t), F32),
        compiler_params=_params("parallel", "parallel"), name="rel_bias_tiles",
    )(rel_bias)


def _attn_kernel(q_ref, k_ref, v_ref, bt_ref, lq1_ref, lk1_ref, lq2_ref, lk2_ref, sg_ref,
                 o_ref, q2_ref, m_ref, al_ref, p_ref, acc_ref, vx_ref, *, t, nk, unroll, lambda_init):
    i = pl.program_id(2)
    dv = v_ref.shape[1]

    @pl.when(i == 0)
    def _():
        vx_ref[:, 0:dv] = v_ref[...]
        ones_lane = lax.broadcasted_iota(jnp.int32, (v_ref.shape[0], dv), 1) == 0
        vx_ref[:, dv:2 * dv] = jnp.where(ones_lane, 1.0, 0.0).astype(vx_ref.dtype)

    q = q_ref[...]
    lane = lax.broadcasted_iota(jnp.int32, q.shape, 1)
    zero = jnp.zeros_like(q)
    q2_ref[0:t, :] = jnp.where(lane < DIFF_HEAD_DIM, q, zero)
    q2_ref[t:2 * t, :] = jnp.where(lane >= DIFF_HEAD_DIM, q, zero)
    far = N_BIAS_TILES // 2

    def scores(j, rows, m_prev):
        k = k_ref[pl.ds(pl.multiple_of(j * t, t), t), :]
        s = lax.dot_general(q2_ref[rows, :], k, (((1,), (1,)), ((), ())), preferred_element_type=F32)
        bias = bt_ref[jnp.clip(j - i, -far, far) + far]
        cols = [s[:, c0:c0 + LANES] + bias[:, c0:c0 + LANES] for c0 in range(0, t, LANES)]
        rm = functools.reduce(jnp.maximum, cols)
        m_new = jnp.maximum(m_prev, jnp.max(rm, axis=1, keepdims=True))
        alpha = jnp.exp2(m_prev - m_new)
        p = jnp.concatenate([jnp.exp2(col - m_new).astype(BF16) for col in cols], axis=1)
        return m_new, alpha, p

    def values(j, alpha, p, acc_prev):
        pv = _dot(p, vx_ref[pl.ds(pl.multiple_of(j * t, t), t), :])
        return jnp.concatenate([alpha * acc_prev[:, 0:dv] + pv[:, 0:dv],
                                alpha * acc_prev[:, dv:2 * dv] + pv[:, dv:2 * dv]], axis=1)

    halves = (slice(0, t), slice(t, 2 * t))
    for rows in halves:
        m0, a0, p0 = scores(0, rows, jnp.full((t, LANES), -jnp.inf, F32))
        m_ref[rows, :], al_ref[rows, :], p_ref[rows, :] = m0, a0, p0
    acc_ref[...] = jnp.zeros(acc_ref.shape, F32)

    def trip(jj, _):
        for rows in halves:
            m, alpha, p, acc = m_ref[rows, :], al_ref[rows, :], p_ref[rows, :], acc_ref[rows, :]
            for u in range(unroll):
                j = jj * unroll + u
                acc = values(j, alpha, p, acc)
                m, alpha, p = scores(j + 1, rows, m)
            m_ref[rows, :], al_ref[rows, :], p_ref[rows, :], acc_ref[rows, :] = m, alpha, p, acc

    lax.fori_loop(0, (nk - 1) // unroll, trip, None)

    lam = (jnp.exp(jnp.sum(lq1_ref[...] * lk1_ref[...])) - jnp.exp(jnp.sum(lq2_ref[...] * lk2_ref[...]))
           + lambda_init)
    acc = jnp.concatenate([values(nk - 1, al_ref[rows, :], p_ref[rows, :], acc_ref[rows, :])
                           for rows in halves], axis=0)
    o = acc[:, 0:dv] / acc[:, dv:dv + 1]
    o = o[0:t] - lam * o[t:2 * t]
    o = _rms(o, sg_ref[...]) * (1.0 - lambda_init)
    o_ref[...] = o.astype(o_ref.dtype)


def _attention(qkv, bias_tiles, lq1, lk1, lq2, lk2, subln, *, n_heads, lambda_init, t, unroll):
    b, s, _ = qkv.shape
    nk = s // t
    assert nk * t == s and t % LANES == 0 and t + 1 >= 91 and (nk - 1) % unroll == 0
    kern = functools.partial(_attn_kernel, t=t, nk=nk, unroll=unroll, lambda_init=lambda_init)
    vec = lambda n: pl.BlockSpec((1, n), lambda b_, h, i: (0, 0))
    return pl.pallas_call(
        kern, grid=(b, n_heads, nk),
        in_specs=[
            pl.BlockSpec((None, t, V_HEAD_DIM), lambda b_, h, i: (b_, i, h)),
            pl.BlockSpec((None, s, V_HEAD_DIM), lambda b_, h, i: (b_, 0, n_heads + h)),
            pl.BlockSpec((None, s, V_HEAD_DIM), lambda b_, h, i: (b_, 0, 2 * n_heads + h)),
            pl.BlockSpec((None, N_BIAS_TILES, t, t), lambda b_, h, i: (h, 0, 0, 0)),
            vec(DIFF_HEAD_DIM), vec(DIFF_HEAD_DIM), vec(DIFF_HEAD_DIM), vec(DIFF_HEAD_DIM),
            vec(V_HEAD_DIM),
        ],
        out_specs=pl.BlockSpec((None, t, V_HEAD_DIM), lambda b_, h, i: (b_, i, h)),
        out_shape=jax.ShapeDtypeStruct((b, s, n_heads * V_HEAD_DIM), BF16),
        scratch_shapes=[pltpu.VMEM((2 * t, V_HEAD_DIM), BF16),
                        pltpu.VMEM((2 * t, LANES), F32),
                        pltpu.VMEM((2 * t, LANES), F32),
                        pltpu.VMEM((2 * t, t), BF16),
                        pltpu.VMEM((2 * t, 2 * V_HEAD_DIM), F32),
                        pltpu.VMEM((s, 2 * V_HEAD_DIM), BF16)],
        compiler_params=_params("parallel", "parallel", "arbitrary"), name="diff_attention",
    )(qkv, qkv, qkv, bias_tiles, lq1.reshape(1, -1), lk1.reshape(1, -1),
      lq2.reshape(1, -1), lk2.reshape(1, -1), subln.reshape(1, -1))


def _dft_tables(n_hi):
    n = n_hi * N_LO
    half = n_hi // 2
    hi = np.arange(n_hi)
    lo = np.arange(N_LO)
    f_hi = np.exp(-2j * np.pi * ((np.outer(hi, hi) % n_hi) / n_hi))
    fr, fi = f_hi.real, f_hi.imag
    s1_pair = np.block([[fr[:, :half], -fi[:, :half]], [fi[:, :half], fr[:, :half]]])
    s1_real = np.concatenate([fr[:, :half], fi[:, :half]], axis=0)
    ph = (hi[:, None, None] * lo[None, None, :] + n_hi * lo[None, :, None] * lo[None, None, :]) % n
    g = np.exp(-2j * np.pi * ph / n)
    g2 = np.concatenate([np.concatenate([g.real, -g.imag], axis=2),
                         np.concatenate([g.imag, g.real], axis=2)], axis=1)
    h2 = np.swapaxes(g2, 1, 2)
    s4 = np.block([[fr[:half], fi[:half]], [-fi[:half], fr[:half]]]) / n
    cast = lambda a: jnp.asarray(a.astype(np.float32)).astype(BF16)
    return cast(s1_pair), cast(s1_real), cast(g2), cast(h2), cast(s4)


HALO_ROWS = 16


def _hy_pre_kernel(z_ref, zp_ref, zn_ref, w_ref, b_ref, vv_ref, x0_ref, *, c, n_blk):
    i = pl.program_id(1)
    z = z_ref[...].astype(F32)
    rows = z.shape[0]
    r = lax.broadcasted_iota(jnp.int32, z.shape, 0)
    prev_row = jnp.where(i > 0, zp_ref[HALO_ROWS - 1:HALO_ROWS, :].astype(F32), 0.0)
    next_row = jnp.where(i < n_blk - 1, zn_ref[0:1, :].astype(F32), 0.0)
    z_m1 = jnp.where(r == 0, prev_row, pltpu.roll(z, 1, 0))
    z_p1 = jnp.where(r == rows - 1, next_row, pltpu.roll(z, rows - 1, 0))
    u = b_ref[...] + z_m1 * w_ref[0:1, :] + z * w_ref[1:2, :] + z_p1 * w_ref[2:3, :]
    vv_ref[...] = (u[:, 2 * c:3 * c] * u[:, c:2 * c]).astype(vv_ref.dtype)
    x0_ref[...] = u[:, 0:c].astype(x0_ref.dtype)


def _hy_pre(z, conv_w, conv_b, *, tt=512):
    b, s, c3 = z.shape
    c = c3 // 3
    n_blk = s // tt
    assert n_blk * tt == s and tt % HALO_ROWS == 0
    kern = functools.partial(_hy_pre_kernel, c=c, n_blk=n_blk)
    sub = tt // HALO_ROWS
    last = s // HALO_ROWS - 1
    row = pl.BlockSpec((None, tt, c), lambda b_, i: (b_, i, 0))
    return pl.pallas_call(
        kern, grid=(b, n_blk),
        in_specs=[
            pl.BlockSpec((None, tt, c3), lambda b_, i: (b_, i, 0)),
            pl.BlockSpec((None, HALO_ROWS, c3), lambda b_, i: (b_, jnp.maximum(i * sub - 1, 0), 0)),
            pl.BlockSpec((None, HALO_ROWS, c3), lambda b_, i: (b_, jnp.minimum((i + 1) * sub, last), 0)),
            pl.BlockSpec((3, c3), lambda b_, i: (0, 0)),
            pl.BlockSpec((1, c3), lambda b_, i: (0, 0)),
        ],
        out_specs=[row, row],
        out_shape=[jax.ShapeDtypeStruct((b, s, c), BF16), jax.ShapeDtypeStruct((b, s, c), BF16)],
        compiler_params=_params("parallel", "parallel"), name="hyena_pre",
    )(z, z, z, conv_w, conv_b.reshape(1, c3))


def _hdot(a, b):
    return jnp.dot(a, b, preferred_element_type=F32, precision=lax.Precision.HIGHEST)


def _filter_kernel(z_ref, w1_ref, b1_ref, w2_ref, b2_ref, w3_ref, b3_ref, w4_ref, fr_ref, dl_ref,
                   hf_ref, hb_ref, *, c):
    i = pl.program_id(0)
    z = z_ref[...]
    fr = fr_ref[...]
    a = jnp.sin(fr * (_hdot(z, w1_ref[...]) + b1_ref[...]))
    a = jnp.sin(fr * (_hdot(a, w2_ref[...]) + b2_ref[...]))
    a = jnp.sin(fr * (_hdot(a, w3_ref[...]) + b3_ref[...]))
    hh = _hdot(a, w4_ref[...])
    decay = jnp.exp(-z[:, 0:1] * jnp.abs(dl_ref[...]))
    hf_ref[...] = (hh[:, 0:c] * decay).astype(hf_ref.dtype)
    r = lax.broadcasted_iota(jnp.int32, (z.shape[0], c), 0)
    lag0 = jnp.logical_and(i == 0, r == 0)
    hb_ref[...] = jnp.where(lag0, 0.0, hh[:, c:2 * c] * decay).astype(hb_ref.dtype)


def _filter_taps(z, w1, b1, w2, b2, w3, b3, w4, freq, deltas):
    s = z.shape[0]
    c = deltas.shape[0]
    fo = LANES
    pad2 = lambda a, r, cc: jnp.pad(a, ((0, r - a.shape[0]), (0, cc - a.shape[1])))
    rowp = lambda v: jnp.pad(v, (0, fo - v.shape[0])).reshape(1, fo)
    args = [pad2(z, s, LANES), pad2(w1, LANES, fo), rowp(b1), pad2(w2, fo, fo), rowp(b2),
            pad2(w3, fo, fo), rowp(b3), pad2(w4, fo, 2 * c), rowp(freq), deltas.reshape(1, c)]
    specs = [pl.BlockSpec((N_LO, LANES), lambda i: (i, 0))] + [
        pl.BlockSpec(a.shape, lambda i: (0, 0)) for a in args[1:]]
    out = pl.BlockSpec((N_LO, c), lambda i: (i, 0))
    shp = jax.ShapeDtypeStruct((s, c), BF16)
    return pl.pallas_call(
        functools.partial(_filter_kernel, c=c), grid=(s // N_LO,),
        in_specs=specs, out_specs=[out, out], out_shape=[shp, shp],
        compiler_params=_params("parallel"), name="hyena_filter_taps",
    )(*args)


def _dft1_kernel(*refs, n_in):
    x_refs, f_ref, ar_ref, ai_ref = refs[:n_in], refs[n_in], refs[n_in + 1], refs[n_in + 2]
    n_out = ar_ref.shape[0]
    for j in range(ar_ref.shape[1]):
        rows = [r[:, j, :] for r in x_refs]
        x = jnp.concatenate(rows, axis=0) if n_in > 1 else rows[0]
        a = _dot(f_ref[...], x)
        ar_ref[:, j, :] = a[0:n_out].astype(ar_ref.dtype)
        ai_ref[:, j, :] = a[n_out:2 * n_out].astype(ai_ref.dtype)


def _dft1_pair(xv, f_mat, *, n_hi, t_lo=16):
    b, half, _, c = xv.shape
    spec = lambda off: pl.BlockSpec((None, half, t_lo, c), lambda p, j: (2 * p + off, 0, j, 0))
    out = pl.BlockSpec((None, n_hi, t_lo, c), lambda p, j: (p, 0, j, 0))
    shp = jax.ShapeDtypeStruct((b // 2, n_hi, N_LO, c), BF16)
    return pl.pallas_call(
        functools.partial(_dft1_kernel, n_in=2), grid=(b // 2, N_LO // t_lo),
        in_specs=[spec(0), spec(1), pl.BlockSpec(f_mat.shape, lambda p, j: (0, 0))],
        out_specs=[out, out], out_shape=[shp, shp],
        compiler_params=_params("parallel", "parallel"), name="hyena_dft1",
    )(xv, xv, f_mat)


def _dft1_filter_kernel(hf_ref, hb_ref, f_ref, *out_refs):
    n_out = out_refs[0].shape[0]
    c = hf_ref.shape[2]
    for j in range(hf_ref.shape[1]):
        x = jnp.concatenate([hf_ref[:, j, :], hb_ref[:, j, :]], axis=1)
        a = _dot(f_ref[...], x)
        for o_ref, r0, c0 in zip(out_refs, (0, n_out, 0, n_out), (0, 0, c, c)):
            o_ref[:, j, :] = a[r0:r0 + n_out, c0:c0 + c].astype(o_ref.dtype)


def _dft1_filter(hf, hb, f_mat, *, n_hi, t_lo=16):
    half, _, c = hf.shape
    blk = pl.BlockSpec((half, t_lo, c), lambda j: (0, j, 0))
    out = pl.BlockSpec((n_hi, t_lo, c), lambda j: (0, j, 0))
    shp = jax.ShapeDtypeStruct((n_hi, N_LO, c), BF16)
    return pl.pallas_call(
        _dft1_filter_kernel, grid=(N_LO // t_lo,),
        in_specs=[blk, blk, pl.BlockSpec(f_mat.shape, lambda j: (0, 0))],
        out_specs=[out] * 4, out_shape=[shp] * 4,
        compiler_params=_params("parallel"), name="hyena_filter_dft1",
    )(hf, hb, f_mat)


def _spectral_kernel(ar_ref, ai_ref, fr_ref, fi_ref, br_ref, bi_ref, g_ref, h_ref, cr_ref, ci_ref):
    c = ar_ref.shape[1]
    a = jnp.concatenate([jnp.concatenate([ar_ref[...], fr_ref[...], br_ref[...]], axis=1),
                         jnp.concatenate([ai_ref[...], fi_ref[...], bi_ref[...]], axis=1)], axis=0)
    x = _dot(g_ref[...], a)
    xr, xi = x[0:N_LO, 0:c], x[N_LO:2 * N_LO, 0:c]
    kr = x[0:N_LO, c:2 * c] + x[0:N_LO, 2 * c:3 * c]
    ki = x[N_LO:2 * N_LO, c:2 * c] - x[N_LO:2 * N_LO, 2 * c:3 * c]
    y = jnp.concatenate([xr * kr - xi * ki, xr * ki + xi * kr], axis=0).astype(BF16)
    cc = _dot(h_ref[...], y)
    cr_ref[...] = cc[0:N_LO].astype(cr_ref.dtype)
    ci_ref[...] = cc[N_LO:2 * N_LO].astype(ci_ref.dtype)


def _spectral(ar, ai, filt, g2, h2, *, n_hi):
    p, _, _, c = ar.shape
    blk = pl.BlockSpec((None, None, N_LO, c), lambda q, k: (q, k, 0, 0))
    fblk = pl.BlockSpec((None, N_LO, c), lambda q, k: (k, 0, 0))
    mat = pl.BlockSpec((None, 2 * N_LO, 2 * N_LO), lambda q, k: (k, 0, 0))
    shp = jax.ShapeDtypeStruct((p, n_hi, N_LO, c), BF16)
    return pl.pallas_call(
        _spectral_kernel, grid=(p, n_hi),
        in_specs=[blk, blk, fblk, fblk, fblk, fblk, mat, mat],
        out_specs=[blk, blk], out_shape=[shp, shp],
        compiler_params=_params("parallel", "parallel"), name="hyena_spectral",
    )(ar, ai, *filt, g2, h2)


def _idft_kernel(cr_ref, ci_ref, m_ref, o_ref):
    half = o_ref.shape[1]
    for j in range(o_ref.shape[2]):
        cc = jnp.concatenate([cr_ref[:, j, :], ci_ref[:, j, :]], axis=0)
        y = _dot(m_ref[...], cc)
        o_ref[0, :, j, :] = y[0:half]
        o_ref[1, :, j, :] = y[half:2 * half]


def _idft(cr, ci, s4, *, n_hi, t_lo=16):
    p, _, _, c = cr.shape
    half = n_hi // 2
    spec = pl.BlockSpec((None, n_hi, t_lo, c), lambda q, j: (q, 0, j, 0))
    return pl.pallas_call(
        _idft_kernel, grid=(p, N_LO // t_lo),
        in_specs=[spec, spec, pl.BlockSpec(s4.shape, lambda q, j: (0, 0))],
        out_specs=pl.BlockSpec((2, half, t_lo, c), lambda q, j: (q, 0, j, 0)),
        out_shape=jax.ShapeDtypeStruct((2 * p, half, N_LO, c), F32),
        compiler_params=_params("parallel", "parallel"), name="hyena_idft",
    )(cr, ci, s4)


def _hy_post_kernel(y_ref, vv_ref, x0_ref, fb_ref, og_ref, mg_ref, o_ref):
    vv = vv_ref[...].astype(F32)
    y = (y_ref[...] + vv * fb_ref[...]) * x0_ref[...].astype(F32)
    sq = y * y
    hi = sq.astype(BF16)
    lo = (sq - hi.astype(F32)).astype(BF16)
    ms = _dot(hi, mg_ref[...]) + _dot(lo, mg_ref[...])
    o_ref[...] = (y * lax.rsqrt(ms + RMS_EPS) * og_ref[...]).astype(o_ref.dtype)


def _hy_post(y, vv, x0, filt_bias, out_g, *, tt=512):
    b, s, c = vv.shape
    gdim = c // N_HYENA_GROUPS
    grp = np.arange(c) // gdim
    assert gdim & (gdim - 1) == 0
    mg = jnp.asarray((grp[:, None] == grp[None, :]).astype(np.float32) / gdim).astype(BF16)
    row = pl.BlockSpec((None, tt, c), lambda b_, i: (b_, i, 0))
    vec = pl.BlockSpec((1, c), lambda b_, i: (0, 0))
    return pl.pallas_call(
        _hy_post_kernel, grid=(b, s // tt),
        in_specs=[row, row, row, vec, vec, pl.BlockSpec((c, c), lambda b_, i: (0, 0))],
        out_specs=row, out_shape=jax.ShapeDtypeStruct((b, s, c), BF16),
        compiler_params=_params("parallel", "parallel"), name="hyena_post",
    )(y, vv, x0, filt_bias.reshape(1, c), out_g.reshape(1, c), mg)


def _filter_positions(s):
    t = jnp.linspace(0.0, 1.0, s, dtype=F32)[:, None]
    bands = (HYENA_EMB_DIM - 1) // 2
    w = 2.0 * math.pi * jnp.arange(s, dtype=F32)[:, None] / s
    f = jnp.linspace(1e-4, bands - 1, bands, dtype=F32)[None, :]
    fw = f * w
    return jnp.concatenate([t, jnp.cos(fw), -jnp.sin(fw)], axis=-1)


def _hyena(z_hy, conv_w, conv_b, w1, b1, w2, b2, w3, b3, w4, freq, filt_bias, out_g):
    b, s, c3 = z_hy.shape
    c = c3 // 3
    n_hi = 2 * s // N_LO
    half = n_hi // 2
    assert b % 2 == 0 and n_hi * N_LO == 2 * s
    s1_pair, s1_real, g2, h2, s4 = _dft_tables(n_hi)

    max_decay = math.log(HYENA_DECAY_TARGET) / HYENA_FAST_DECAY
    min_decay = math.log(HYENA_DECAY_TARGET) / HYENA_SLOW_DECAY
    deltas = jnp.linspace(min_decay, max_decay, c, dtype=F32)
    hf, hb = _filter_taps(_filter_positions(s), w1, b1, w2, b2, w3, b3, w4, freq, deltas)
    filt = _dft1_filter(hf.reshape(half, N_LO, c), hb.reshape(half, N_LO, c), s1_real, n_hi=n_hi)

    vv, x0 = _hy_pre(z_hy, conv_w, conv_b)
    ar, ai = _dft1_pair(vv.reshape(b, half, N_LO, c), s1_pair, n_hi=n_hi)
    cr, ci = _spectral(ar, ai, filt, g2, h2, n_hi=n_hi)
    y = _idft(cr, ci, s4, n_hi=n_hi).reshape(b, s, c)
    return _hy_post(y, vv, x0, filt_bias, out_g)


def kernel(x, rel_bias, ffn1_norm, ffn1_w_gate, ffn1_w_up, ffn1_w_down, mix_norm, w_in,
           lambda_q1, lambda_k1, lambda_q2, lambda_k2, diff_subln,
           hy_conv_w, hy_conv_b, hy_f_w1, hy_f_b1, hy_f_w2, hy_f_b2, hy_f_w3, hy_f_b3,
           hy_f_w4, hy_f_freq, hy_bias, hy_out_norm, w_out,
           ffn2_norm, ffn2_w_gate, ffn2_w_up, ffn2_w_down, final_norm):
    b, s, d = x.shape
    depth = w_in.shape[0]
    d_att = diff_subln.shape[1] * rel_bias.shape[1]
    n_heads = rel_bias.shape[1]
    attn_tile = min(512, s)
    n_rest = s // attn_tile - 1
    attn_unroll = max([u for u in (5, 4, 3, 2, 1) if n_rest % u == 0] if n_rest else [1])
    bf = lambda a: a.astype(BF16)

    xf = x.reshape(b * s, d)
    for l in range(depth):
        last = l == depth - 1
        xf = _ffn(xf, ffn1_norm[l], bf(ffn1_w_gate[l]), bf(ffn1_w_up[l]), bf(ffn1_w_down[l]))
        qkv, z_hy = _mix(xf, mix_norm[l], bf(w_in[l]), d_att=d_att)
        lambda_init = 0.8 - 0.6 * math.exp(-0.3 * l)
        att = _attention(qkv.reshape(b, s, -1), _bias_tiles(rel_bias, attn_tile),
                         lambda_q1[l], lambda_k1[l], lambda_q2[l], lambda_k2[l], diff_subln[l],
                         n_heads=n_heads, lambda_init=lambda_init, t=attn_tile, unroll=attn_unroll)
        hy = _hyena(z_hy.reshape(b, s, -1), hy_conv_w[l], hy_conv_b[l], hy_f_w1[l], hy_f_b1[l],
                    hy_f_w2[l], hy_f_b2[l], hy_f_w3[l], hy_f_b3[l], hy_f_w4[l], hy_f_freq[l],
                    hy_bias[l], hy_out_norm[l])
        wo = bf(w_out[l])
        xf = _ffn(xf, ffn2_norm[l], bf(ffn2_w_gate[l]), bf(ffn2_w_up[l]), bf(ffn2_w_down[l]),
                  pre=(att.reshape(b * s, -1), hy.reshape(b * s, -1), wo[:d_att], wo[d_att:]),
                  final_g=final_norm if last else None)
    if depth == 0:
        raise ValueError("depth must be positive")
    return xf.reshape(b, s, d)
```

```python
import functools
import math

import numpy as np
import jax
import jax.numpy as jnp
from jax import lax
from jax.experimental import pallas as pl
from jax.experimental.pallas import tpu as pltpu

F32 = jnp.float32
BF16 = jnp.bfloat16

RMS_EPS = 1e-6
LANES = 128
DIFF_HEAD_DIM = 64
V_HEAD_DIM = 2 * DIFF_HEAD_DIM
REL_BUCKETS = 32
REL_MAX_DIST = 128
N_HYENA_GROUPS = 8
HYENA_EMB_DIM = 33
HYENA_DECAY_TARGET = 1e-2
HYENA_FAST_DECAY = 0.3
HYENA_SLOW_DECAY = 1.5
N_LO = 128
LOG2_E = math.log2(math.e)
VMEM_LIMIT = 56 * 1024 * 1024


def _params(*sem):
    return pltpu.CompilerParams(dimension_semantics=sem, vmem_limit_bytes=VMEM_LIMIT)


def _resident(shape):
    return pl.BlockSpec(shape, lambda *_: (0,) * len(shape), pipeline_mode=pl.Buffered(1))


def _rms(x, g):
    return x * lax.rsqrt(jnp.mean(x * x, axis=-1, keepdims=True) + RMS_EPS) * g


def _dot(a, b):
    return jnp.dot(a, b, preferred_element_type=F32)


def _ffn_kernel(*refs, ff_chunk, n_chunk, pre, post):
    if pre:
        x_ref, att_ref, hy_ref, woa_ref, woh_ref = refs[:5]
        refs = refs[5:]
    else:
        x_ref = refs[0]
        refs = refs[1:]
    g_ref, wg_ref, wu_ref, wd_ref = refs[:4]
    refs = refs[4:]
    if post:
        fg_ref, o_ref = refs
    else:
        (o_ref,) = refs

    x = x_ref[...]
    if pre:
        x = x + _dot(att_ref[...], woa_ref[...]) + _dot(hy_ref[...], woh_ref[...])
    xn = _rms(x, g_ref[...]).astype(BF16)
    acc = jnp.zeros(x.shape, F32)
    for c in range(n_chunk):
        sl = slice(c * ff_chunk, (c + 1) * ff_chunk)
        gate = _dot(xn, wg_ref[:, sl])
        up = _dot(xn, wu_ref[:, sl])
        h = (jax.nn.silu(gate) * up).astype(BF16)
        acc = acc + _dot(h, wd_ref[sl, :])
    y = x + 0.5 * acc
    if post:
        y = _rms(y, fg_ref[...])
    o_ref[...] = y


def _ffn(x, norm_g, wg, wu, wd, *, pre=None, final_g=None, tm=512, ff_chunk=256):
    m, d = x.shape
    dff = wg.shape[1]
    n_chunk = dff // ff_chunk
    assert n_chunk * ff_chunk == dff and m % tm == 0
    row = lambda w: pl.BlockSpec((tm, w), lambda i: (i, 0))
    args, specs = [x], [row(d)]
    if pre is not None:
        att, hy, woa, woh = pre
        args += [att, hy, woa, woh]
        specs += [row(att.shape[1]), row(hy.shape[1]), _resident(woa.shape), _resident(woh.shape)]
    args += [norm_g.reshape(1, d), wg, wu, wd]
    specs += [_resident((1, d)), _resident(wg.shape), _resident(wu.shape), _resident(wd.shape)]
    if final_g is not None:
        args.append(final_g.reshape(1, d))
        specs.append(_resident((1, d)))
    kern = functools.partial(_ffn_kernel, ff_chunk=ff_chunk, n_chunk=n_chunk,
                             pre=pre is not None, post=final_g is not None)
    return pl.pallas_call(
        kern, grid=(m // tm,), in_specs=specs, out_specs=row(d),
        out_shape=jax.ShapeDtypeStruct((m, d), F32),
        compiler_params=_params("parallel"), name="ffn_pre" if pre is not None else "ffn",
    )(*args)


def _mix_kernel(x_ref, g_ref, w_ref, qkv_ref, z_ref, *, d_att, col_chunk, scale):
    xn = _rms(x_ref[...], g_ref[...]).astype(BF16)
    n_att = 3 * d_att
    for c0 in range(0, w_ref.shape[1], col_chunk):
        p = _dot(xn, w_ref[:, c0:c0 + col_chunk])
        if c0 < n_att:
            if c0 < d_att:
                p = p * scale
            qkv_ref[:, c0:c0 + col_chunk] = p.astype(BF16)
        else:
            z_ref[:, c0 - n_att:c0 - n_att + col_chunk] = p.astype(z_ref.dtype)


def _mix(x, norm_g, w_in, *, d_att, tm=512, col_chunk=512):
    m, d = x.shape
    d_in = w_in.shape[1]
    n_att = 3 * d_att
    assert d_att % col_chunk == 0 and d_in % col_chunk == 0 and m % tm == 0
    kern = functools.partial(_mix_kernel, d_att=d_att, col_chunk=col_chunk,
                             scale=DIFF_HEAD_DIM ** -0.5 * LOG2_E)
    return pl.pallas_call(
        kern, grid=(m // tm,),
        in_specs=[pl.BlockSpec((tm, d), lambda i: (i, 0)), _resident((1, d)), _resident(w_in.shape)],
        out_specs=[pl.BlockSpec((tm, n_att), lambda i: (i, 0)),
                   pl.BlockSpec((tm, d_in - n_att), lambda i: (i, 0))],
        out_shape=[jax.ShapeDtypeStruct((m, n_att), BF16),
                   jax.ShapeDtypeStruct((m, d_in - n_att), BF16)],
        compiler_params=_params("parallel"), name="mix",
    )(x, norm_g.reshape(1, d), w_in)


N_BIAS_TILES = 5


def _bias_kernel(tab_ref, o_ref, *, t):
    h = pl.program_id(0)
    d = pl.program_id(1) - N_BIAS_TILES // 2
    half = REL_BUCKETS // 2
    max_exact = half // 2
    rel = lax.broadcasted_iota(jnp.int32, (8, 2 * t), 1) + (d - 1) * t
    ret = jnp.where(rel > 0, half, 0)
    n = jnp.abs(rel)
    nf = jnp.maximum(n, 1).astype(F32)
    large = max_exact + (jnp.log(nf / max_exact) / math.log(REL_MAX_DIST / max_exact)
                         * (half - max_exact)).astype(jnp.int32)
    large = jnp.minimum(large, half - 1)
    bucket = ret + jnp.where(n < max_exact, n, large)
    row = jnp.zeros(rel.shape, F32)
    for b in range(REL_BUCKETS):
        row = jnp.where(bucket == b, tab_ref[b, h], row)
    rows = jnp.broadcast_to(row[0:1, :] * LOG2_E, (t, 2 * t))
    o_ref[...] = pltpu.roll(rows, 0, 1, stride=1, stride_axis=0)[:, t:2 * t]


def _bias_tiles(rel_bias, t):
    nh = rel_bias.shape[1]
    return pl.pallas_call(
        functools.partial(_bias_kernel, t=t), grid=(nh, N_BIAS_TILES),
        in_specs=[pl.BlockSpec(memory_space=pltpu.SMEM)],
        out_specs=pl.BlockSpec((None, None, t, t), lambda h, d: (h, d, 0, 0)),
        out_shape=jax.ShapeDtypeStruct((nh, N_BIAS_TILES, t, t), F32),
        compiler_params=_params("parallel", "parallel"), name="rel_bias_tiles",
    )(rel_bias)


def _attn_kernel(q_ref, k_ref, v_ref, bt_ref, lq1_ref, lk1_ref, lq2_ref, lk2_ref, sg_ref,
                 o_ref, q2_ref, vx_ref, *, t, nk, n_chain, lambda_init):
    i = pl.program_id(2)
    dv = v_ref.shape[1]

    @pl.when(i == 0)
    def _():
        vx_ref[:, 0:dv] = v_ref[...]
        ones_lane = lax.broadcasted_iota(jnp.int32, (v_ref.shape[0], dv), 1) == 0
        vx_ref[:, dv:2 * dv] = jnp.where(ones_lane, 1.0, 0.0).astype(vx_ref.dtype)

    q = q_ref[...]
    lane = lax.broadcasted_iota(jnp.int32, q.shape, 1)
    zero = jnp.zeros_like(q)
    q2_ref[0:t, :] = jnp.where(lane < DIFF_HEAD_DIM, q, zero)
    q2_ref[t:2 * t, :] = jnp.where(lane >= DIFF_HEAD_DIM, q, zero)
    far = N_BIAS_TILES // 2

    def scores(d, rows, n_keys=t):
        wrapped = i + d >= nk
        j = jnp.where(wrapped, i + d - nk, i + d)
        k = k_ref[pl.ds(pl.multiple_of(j * t, t), n_keys), :]
        s = lax.dot_general(q2_ref[rows, :], k, (((1,), (1,)), ((), ())), preferred_element_type=F32)
        cols = [s[:, c0:c0 + LANES] for c0 in range(0, n_keys, LANES)]
        if d <= 1 or d >= nk - 1:
            qr = slice(rows.start % t, rows.start % t + rows.stop - rows.start)
            bias = bt_ref[jnp.clip(j - i, -far, far) + far, qr, :]
            return j, [c + bias[:, n * LANES:(n + 1) * LANES] for n, c in enumerate(cols)], None
        side = jnp.where(wrapped, bt_ref[0, 0:1, 0:LANES], bt_ref[2 * far, 0:1, 0:LANES])
        return j, cols, side

    def tile(d, rows, m_prev, acc_prev):
        j, cols, side = scores(d, rows)
        shift = m_prev if side is None else m_prev - side
        rel = [(col - shift).astype(BF16) for col in cols]
        rm = jnp.max(functools.reduce(jnp.maximum, rel), axis=1, keepdims=True)
        delta = jnp.maximum(rm, 0.0)
        p = jnp.concatenate([jnp.exp2(x - delta) for x in rel], axis=1)
        d32 = jnp.broadcast_to(delta.astype(F32), m_prev.shape)
        alpha = jnp.exp2(-d32)
        pv = _dot(p, vx_ref[pl.ds(pl.multiple_of(j * t, t), t), :])
        acc_new = jnp.concatenate([alpha * acc_prev[:, 0:dv] + pv[:, 0:dv],
                                   alpha * acc_prev[:, dv:2 * dv] + pv[:, dv:2 * dv]], axis=1)
        return m_prev + d32, acc_new

    rc = 2 * t // n_chain
    chains = tuple(slice(r0, r0 + rc) for r0 in range(0, 2 * t, rc))
    state = []
    for rows in chains:
        m0 = jnp.max(scores(0, rows, LANES)[1][0], axis=1, keepdims=True)
        state.append((jnp.broadcast_to(m0, (rc, LANES)), jnp.zeros((rc, 2 * dv), F32)))
    for d in range(nk):
        state = [tile(d, rows, *st) for rows, st in zip(chains, state)]

    lam = (jnp.exp(jnp.sum(lq1_ref[...] * lk1_ref[...])) - jnp.exp(jnp.sum(lq2_ref[...] * lk2_ref[...]))
           + lambda_init)
    acc = jnp.concatenate([a for _, a in state], axis=0)
    o = acc[:, 0:dv] / acc[:, dv:dv + 1]
    o = o[0:t] - lam * o[t:2 * t]
    o = _rms(o, sg_ref[...]) * (1.0 - lambda_init)
    o_ref[...] = o.astype(o_ref.dtype)


def _attention(qkv, bias_tiles, lq1, lk1, lq2, lk2, subln, *, n_heads, lambda_init, t, n_chain):
    b, s, _ = qkv.shape
    nk = s // t
    assert nk * t == s and t % LANES == 0 and t + 1 >= 91 and (2 * t) % n_chain == 0
    kern = functools.partial(_attn_kernel, t=t, nk=nk, n_chain=n_chain, lambda_init=lambda_init)
    vec = lambda n: pl.BlockSpec((1, n), lambda b_, h, i: (0, 0))
    return pl.pallas_call(
        kern, grid=(b, n_heads, nk),
        in_specs=[
            pl.BlockSpec((None, t, V_HEAD_DIM), lambda b_, h, i: (b_, i, h)),
            pl.BlockSpec((None, s, V_HEAD_DIM), lambda b_, h, i: (b_, 0, n_heads + h)),
            pl.BlockSpec((None, s, V_HEAD_DIM), lambda b_, h, i: (b_, 0, 2 * n_heads + h)),
            pl.BlockSpec((None, N_BIAS_TILES, t, t), lambda b_, h, i: (h, 0, 0, 0)),
            vec(DIFF_HEAD_DIM), vec(DIFF_HEAD_DIM), vec(DIFF_HEAD_DIM), vec(DIFF_HEAD_DIM),
            vec(V_HEAD_DIM),
        ],
        out_specs=pl.BlockSpec((None, t, V_HEAD_DIM), lambda b_, h, i: (b_, i, h)),
        out_shape=jax.ShapeDtypeStruct((b, s, n_heads * V_HEAD_DIM), BF16),
        scratch_shapes=[pltpu.VMEM((2 * t, V_HEAD_DIM), BF16),
                        pltpu.VMEM((s, 2 * V_HEAD_DIM), BF16)],
        compiler_params=_params("parallel", "parallel", "arbitrary"), name="diff_attention",
    )(qkv, qkv, qkv, bias_tiles, lq1.reshape(1, -1), lk1.reshape(1, -1),
      lq2.reshape(1, -1), lk2.reshape(1, -1), subln.reshape(1, -1))


def _dft_tables(n_hi):
    n = n_hi * N_LO
    half = n_hi // 2
    hi = np.arange(n_hi)
    lo = np.arange(N_LO)
    f_hi = np.exp(-2j * np.pi * ((np.outer(hi, hi) % n_hi) / n_hi))
    fr, fi = f_hi.real, f_hi.imag
    s1_pair = np.block([[fr[:, :half], -fi[:, :half]], [fi[:, :half], fr[:, :half]]])
    s1_real = np.concatenate([fr[:, :half], fi[:, :half]], axis=0)
    ph = (hi[:, None, None] * lo[None, None, :] + n_hi * lo[None, :, None] * lo[None, None, :]) % n
    g = np.exp(-2j * np.pi * ph / n)
    g2 = np.concatenate([np.concatenate([g.real, -g.imag], axis=2),
                         np.concatenate([g.imag, g.real], axis=2)], axis=1)
    h2 = np.swapaxes(g2, 1, 2)
    s4 = np.block([[fr[:half], fi[:half]], [-fi[:half], fr[:half]]]) / n
    cast = lambda a: jnp.asarray(a.astype(np.float32)).astype(BF16)
    return cast(s1_pair), cast(s1_real), cast(g2), cast(h2), cast(s4)


HALO_ROWS = 16


def _hy_pre_kernel(z_ref, zp_ref, zn_ref, w_ref, b_ref, vv_ref, x0_ref, *, c, n_blk):
    i = pl.program_id(1)
    z = z_ref[...].astype(F32)
    rows = z.shape[0]
    r = lax.broadcasted_iota(jnp.int32, z.shape, 0)
    prev_row = jnp.where(i > 0, zp_ref[HALO_ROWS - 1:HALO_ROWS, :].astype(F32), 0.0)
    next_row = jnp.where(i < n_blk - 1, zn_ref[0:1, :].astype(F32), 0.0)
    z_m1 = jnp.where(r == 0, prev_row, pltpu.roll(z, 1, 0))
    z_p1 = jnp.where(r == rows - 1, next_row, pltpu.roll(z, rows - 1, 0))
    u = b_ref[...] + z_m1 * w_ref[0:1, :] + z * w_ref[1:2, :] + z_p1 * w_ref[2:3, :]
    vv_ref[...] = (u[:, 2 * c:3 * c] * u[:, c:2 * c]).astype(vv_ref.dtype)
    x0_ref[...] = u[:, 0:c].astype(x0_ref.dtype)


def _hy_pre(z, conv_w, conv_b, *, tt=512):
    b, s, c3 = z.shape
    c = c3 // 3
    n_blk = s // tt
    assert n_blk * tt == s and tt % HALO_ROWS == 0
    kern = functools.partial(_hy_pre_kernel, c=c, n_blk=n_blk)
    sub = tt // HALO_ROWS
    last = s // HALO_ROWS - 1
    row = pl.BlockSpec((None, tt, c), lambda b_, i: (b_, i, 0))
    return pl.pallas_call(
        kern, grid=(b, n_blk),
        in_specs=[
            pl.BlockSpec((None, tt, c3), lambda b_, i: (b_, i, 0)),
            pl.BlockSpec((None, HALO_ROWS, c3), lambda b_, i: (b_, jnp.maximum(i * sub - 1, 0), 0)),
            pl.BlockSpec((None, HALO_ROWS, c3), lambda b_, i: (b_, jnp.minimum((i + 1) * sub, last), 0)),
            pl.BlockSpec((3, c3), lambda b_, i: (0, 0)),
            pl.BlockSpec((1, c3), lambda b_, i: (0, 0)),
        ],
        out_specs=[row, row],
        out_shape=[jax.ShapeDtypeStruct((b, s, c), BF16), jax.ShapeDtypeStruct((b, s, c), BF16)],
        compiler_params=_params("parallel", "parallel"), name="hyena_pre",
    )(z, z, z, conv_w, conv_b.reshape(1, c3))


def _hdot(a, b):
    return jnp.dot(a, b, preferred_element_type=F32, precision=lax.Precision.HIGHEST)


def _filter_kernel(z_ref, w1_ref, b1_ref, w2_ref, b2_ref, w3_ref, b3_ref, w4_ref, fr_ref, dl_ref,
                   hf_ref, hb_ref, *, c):
    i = pl.program_id(0)
    z = z_ref[...]
    fr = fr_ref[...]
    a = jnp.sin(fr * (_hdot(z, w1_ref[...]) + b1_ref[...]))
    a = jnp.sin(fr * (_hdot(a, w2_ref[...]) + b2_ref[...]))
    a = jnp.sin(fr * (_hdot(a, w3_ref[...]) + b3_ref[...]))
    hh = _hdot(a, w4_ref[...])
    decay = jnp.exp(-z[:, 0:1] * jnp.abs(dl_ref[...]))
    hf_ref[...] = (hh[:, 0:c] * decay).astype(hf_ref.dtype)
    r = lax.broadcasted_iota(jnp.int32, (z.shape[0], c), 0)
    lag0 = jnp.logical_and(i == 0, r == 0)
    hb_ref[...] = jnp.where(lag0, 0.0, hh[:, c:2 * c] * decay).astype(hb_ref.dtype)


def _filter_taps(z, w1, b1, w2, b2, w3, b3, w4, freq, deltas):
    s = z.shape[0]
    c = deltas.shape[0]
    fo = LANES
    pad2 = lambda a, r, cc: jnp.pad(a, ((0, r - a.shape[0]), (0, cc - a.shape[1])))
    rowp = lambda v: jnp.pad(v, (0, fo - v.shape[0])).reshape(1, fo)
    args = [pad2(z, s, LANES), pad2(w1, LANES, fo), rowp(b1), pad2(w2, fo, fo), rowp(b2),
            pad2(w3, fo, fo), rowp(b3), pad2(w4, fo, 2 * c), rowp(freq), deltas.reshape(1, c)]
    specs = [pl.BlockSpec((N_LO, LANES), lambda i: (i, 0))] + [
        pl.BlockSpec(a.shape, lambda i: (0, 0)) for a in args[1:]]
    out = pl.BlockSpec((N_LO, c), lambda i: (i, 0))
    shp = jax.ShapeDtypeStruct((s, c), BF16)
    return pl.pallas_call(
        functools.partial(_filter_kernel, c=c), grid=(s // N_LO,),
        in_specs=specs, out_specs=[out, out], out_shape=[shp, shp],
        compiler_params=_params("parallel"), name="hyena_filter_taps",
    )(*args)


def _dft1_kernel(*refs, n_in):
    x_refs, f_ref, ar_ref, ai_ref = refs[:n_in], refs[n_in], refs[n_in + 1], refs[n_in + 2]
    n_out = ar_ref.shape[0]
    for j in range(ar_ref.shape[1]):
        rows = [r[:, j, :] for r in x_refs]
        x = jnp.concatenate(rows, axis=0) if n_in > 1 else rows[0]
        a = _dot(f_ref[...], x)
        ar_ref[:, j, :] = a[0:n_out].astype(ar_ref.dtype)
        ai_ref[:, j, :] = a[n_out:2 * n_out].astype(ai_ref.dtype)


def _dft1_pair(xv, f_mat, *, n_hi, t_lo=16):
    b, half, _, c = xv.shape
    spec = lambda off: pl.BlockSpec((None, half, t_lo, c), lambda p, j: (2 * p + off, 0, j, 0))
    out = pl.BlockSpec((None, n_hi, t_lo, c), lambda p, j: (p, 0, j, 0))
    shp = jax.ShapeDtypeStruct((b // 2, n_hi, N_LO, c), BF16)
    return pl.pallas_call(
        functools.partial(_dft1_kernel, n_in=2), grid=(b // 2, N_LO // t_lo),
        in_specs=[spec(0), spec(1), pl.BlockSpec(f_mat.shape, lambda p, j: (0, 0))],
        out_specs=[out, out], out_shape=[shp, shp],
        compiler_params=_params("parallel", "parallel"), name="hyena_dft1",
    )(xv, xv, f_mat)


def _dft1_filter_kernel(hf_ref, hb_ref, f_ref, *out_refs):
    n_out = out_refs[0].shape[0]
    c = hf_ref.shape[2]
    for j in range(hf_ref.shape[1]):
        x = jnp.concatenate([hf_ref[:, j, :], hb_ref[:, j, :]], axis=1)
        a = _dot(f_ref[...], x)
        for o_ref, r0, c0 in zip(out_refs, (0, n_out, 0, n_out), (0, 0, c, c)):
            o_ref[:, j, :] = a[r0:r0 + n_out, c0:c0 + c].astype(o_ref.dtype)


def _dft1_filter(hf, hb, f_mat, *, n_hi, t_lo=16):
    half, _, c = hf.shape
    blk = pl.BlockSpec((half, t_lo, c), lambda j: (0, j, 0))
    out = pl.BlockSpec((n_hi, t_lo, c), lambda j: (0, j, 0))
    shp = jax.ShapeDtypeStruct((n_hi, N_LO, c), BF16)
    return pl.pallas_call(
        _dft1_filter_kernel, grid=(N_LO // t_lo,),
        in_specs=[blk, blk, pl.BlockSpec(f_mat.shape, lambda j: (0, 0))],
        out_specs=[out] * 4, out_shape=[shp] * 4,
        compiler_params=_params("parallel"), name="hyena_filter_dft1",
    )(hf, hb, f_mat)


def _spectral_kernel(ar_ref, ai_ref, fr_ref, fi_ref, br_ref, bi_ref, g_ref, h_ref, cr_ref, ci_ref):
    c = ar_ref.shape[1]
    a = jnp.concatenate([jnp.concatenate([ar_ref[...], fr_ref[...], br_ref[...]], axis=1),
                         jnp.concatenate([ai_ref[...], fi_ref[...], bi_ref[...]], axis=1)], axis=0)
    x = _dot(g_ref[...], a)
    xr, xi = x[0:N_LO, 0:c], x[N_LO:2 * N_LO, 0:c]
    kr = x[0:N_LO, c:2 * c] + x[0:N_LO, 2 * c:3 * c]
    ki = x[N_LO:2 * N_LO, c:2 * c] - x[N_LO:2 * N_LO, 2 * c:3 * c]
    y = jnp.concatenate([xr * kr - xi * ki, xr * ki + xi * kr], axis=0).astype(BF16)
    cc = _dot(h_ref[...], y)
    cr_ref[...] = cc[0:N_LO].astype(cr_ref.dtype)
    ci_ref[...] = cc[N_LO:2 * N_LO].astype(ci_ref.dtype)


def _spectral(ar, ai, filt, g2, h2, *, n_hi):
    p, _, _, c = ar.shape
    blk = pl.BlockSpec((None, None, N_LO, c), lambda q, k: (q, k, 0, 0))
    fblk = pl.BlockSpec((None, N_LO, c), lambda q, k: (k, 0, 0))
    mat = pl.BlockSpec((None, 2 * N_LO, 2 * N_LO), lambda q, k: (k, 0, 0))
    shp = jax.ShapeDtypeStruct((p, n_hi, N_LO, c), BF16)
    return pl.pallas_call(
        _spectral_kernel, grid=(p, n_hi),
        in_specs=[blk, blk, fblk, fblk, fblk, fblk, mat, mat],
        out_specs=[blk, blk], out_shape=[shp, shp],
        compiler_params=_params("parallel", "parallel"), name="hyena_spectral",
    )(ar, ai, *filt, g2, h2)


def _idft_kernel(cr_ref, ci_ref, m_ref, o_ref):
    half = o_ref.shape[1]
    for j in range(o_ref.shape[2]):
        cc = jnp.concatenate([cr_ref[:, j, :], ci_ref[:, j, :]], axis=0)
        y = _dot(m_ref[...], cc)
        o_ref[0, :, j, :] = y[0:half]
        o_ref[1, :, j, :] = y[half:2 * half]


def _idft(cr, ci, s4, *, n_hi, t_lo=16):
    p, _, _, c = cr.shape
    half = n_hi // 2
    spec = pl.BlockSpec((None, n_hi, t_lo, c), lambda q, j: (q, 0, j, 0))
    return pl.pallas_call(
        _idft_kernel, grid=(p, N_LO // t_lo),
        in_specs=[spec, spec, pl.BlockSpec(s4.shape, lambda q, j: (0, 0))],
        out_specs=pl.BlockSpec((2, half, t_lo, c), lambda q, j: (q, 0, j, 0)),
        out_shape=jax.ShapeDtypeStruct((2 * p, half, N_LO, c), F32),
        compiler_params=_params("parallel", "parallel"), name="hyena_idft",
    )(cr, ci, s4)


def _hy_post_kernel(y_ref, vv_ref, x0_ref, fb_ref, og_ref, mg_ref, o_ref):
    vv = vv_ref[...].astype(F32)
    y = (y_ref[...] + vv * fb_ref[...]) * x0_ref[...].astype(F32)
    sq = y * y
    hi = sq.astype(BF16)
    lo = (sq - hi.astype(F32)).astype(BF16)
    ms = _dot(hi, mg_ref[...]) + _dot(lo, mg_ref[...])
    o_ref[...] = (y * lax.rsqrt(ms + RMS_EPS) * og_ref[...]).astype(o_ref.dtype)


def _hy_post(y, vv, x0, filt_bias, out_g, *, tt=512):
    b, s, c = vv.shape
    gdim = c // N_HYENA_GROUPS
    grp = np.arange(c) // gdim
    assert gdim & (gdim - 1) == 0
    mg = jnp.asarray((grp[:, None] == grp[None, :]).astype(np.float32) / gdim).astype(BF16)
    row = pl.BlockSpec((None, tt, c), lambda b_, i: (b_, i, 0))
    vec = pl.BlockSpec((1, c), lambda b_, i: (0, 0))
    return pl.pallas_call(
        _hy_post_kernel, grid=(b, s // tt),
        in_specs=[row, row, row, vec, vec, pl.BlockSpec((c, c), lambda b_, i: (0, 0))],
        out_specs=row, out_shape=jax.ShapeDtypeStruct((b, s, c), BF16),
        compiler_params=_params("parallel", "parallel"), name="hyena_post",
    )(y, vv, x0, filt_bias.reshape(1, c), out_g.reshape(1, c), mg)


def _filter_positions(s):
    t = jnp.linspace(0.0, 1.0, s, dtype=F32)[:, None]
    bands = (HYENA_EMB_DIM - 1) // 2
    w = 2.0 * math.pi * jnp.arange(s, dtype=F32)[:, None] / s
    f = jnp.linspace(1e-4, bands - 1, bands, dtype=F32)[None, :]
    fw = f * w
    return jnp.concatenate([t, jnp.cos(fw), -jnp.sin(fw)], axis=-1)


def _hyena(z_hy, conv_w, conv_b, w1, b1, w2, b2, w3, b3, w4, freq, filt_bias, out_g):
    b, s, c3 = z_hy.shape
    c = c3 // 3
    n_hi = 2 * s // N_LO
    half = n_hi // 2
    assert b % 2 == 0 and n_hi * N_LO == 2 * s
    s1_pair, s1_real, g2, h2, s4 = _dft_tables(n_hi)

    max_decay = math.log(HYENA_DECAY_TARGET) / HYENA_FAST_DECAY
    min_decay = math.log(HYENA_DECAY_TARGET) / HYENA_SLOW_DECAY
    deltas = jnp.linspace(min_decay, max_decay, c, dtype=F32)
    hf, hb = _filter_taps(_filter_positions(s), w1, b1, w2, b2, w3, b3, w4, freq, deltas)
    filt = _dft1_filter(hf.reshape(half, N_LO, c), hb.reshape(half, N_LO, c), s1_real, n_hi=n_hi)

    vv, x0 = _hy_pre(z_hy, conv_w, conv_b)
    ar, ai = _dft1_pair(vv.reshape(b, half, N_LO, c), s1_pair, n_hi=n_hi)
    cr, ci = _spectral(ar, ai, filt, g2, h2, n_hi=n_hi)
    y = _idft(cr, ci, s4, n_hi=n_hi).reshape(b, s, c)
    return _hy_post(y, vv, x0, filt_bias, out_g)


def kernel(x, rel_bias, ffn1_norm, ffn1_w_gate, ffn1_w_up, ffn1_w_down, mix_norm, w_in,
           lambda_q1, lambda_k1, lambda_q2, lambda_k2, diff_subln,
           hy_conv_w, hy_conv_b, hy_f_w1, hy_f_b1, hy_f_w2, hy_f_b2, hy_f_w3, hy_f_b3,
           hy_f_w4, hy_f_freq, hy_bias, hy_out_norm, w_out,
           ffn2_norm, ffn2_w_gate, ffn2_w_up, ffn2_w_down, final_norm):
    b, s, d = x.shape
    depth = w_in.shape[0]
    d_att = diff_subln.shape[1] * rel_bias.shape[1]
    n_heads = rel_bias.shape[1]
    attn_tile = min(512, s)
    bf = lambda a: a.astype(BF16)

    xf = x.reshape(b * s, d)
    for l in range(depth):
        last = l == depth - 1
        xf = _ffn(xf, ffn1_norm[l], bf(ffn1_w_gate[l]), bf(ffn1_w_up[l]), bf(ffn1_w_down[l]))
        qkv, z_hy = _mix(xf, mix_norm[l], bf(w_in[l]), d_att=d_att)
        lambda_init = 0.8 - 0.6 * math.exp(-0.3 * l)
        att = _attention(qkv.reshape(b, s, -1), _bias_tiles(rel_bias, attn_tile),
                         lambda_q1[l], lambda_k1[l], lambda_q2[l], lambda_k2[l], diff_subln[l],
                         n_heads=n_heads, lambda_init=lambda_init, t=attn_tile, n_chain=4)
        hy = _hyena(z_hy.reshape(b, s, -1), hy_conv_w[l], hy_conv_b[l], hy_f_w1[l], hy_f_b1[l],
                    hy_f_w2[l], hy_f_b2[l], hy_f_w3[l], hy_f_b3[l], hy_f_w4[l], hy_f_freq[l],
                    hy_bias[l], hy_out_norm[l])
        wo = bf(w_out[l])
        xf = _ffn(xf, ffn2_norm[l], bf(ffn2_w_gate[l]), bf(ffn2_w_up[l]), bf(ffn2_w_down[l]),
                  pre=(att.reshape(b * s, -1), hy.reshape(b * s, -1), wo[:d_att], wo[d_att:]),
                  final_g=final_norm if last else None)
    if depth == 0:
        raise ValueError("depth must be positive")
    return xf.reshape(b, s, d)
```

```python
import functools
import math

import numpy as np
import jax
import jax.numpy as jnp
from jax import lax
from jax.experimental import pallas as pl
from jax.experimental.pallas import tpu as pltpu

F32 = jnp.float32
BF16 = jnp.bfloat16

RMS_EPS = 1e-6
LANES = 128
DIFF_HEAD_DIM = 64
V_HEAD_DIM = 2 * DIFF_HEAD_DIM
REL_BUCKETS = 32
REL_MAX_DIST = 128
N_HYENA_GROUPS = 8
HYENA_EMB_DIM = 33
HYENA_DECAY_TARGET = 1e-2
HYENA_FAST_DECAY = 0.3
HYENA_SLOW_DECAY = 1.5
N_LO = 128
LOG2_E = math.log2(math.e)
VMEM_LIMIT = 56 * 1024 * 1024


def _params(*sem):
    return pltpu.CompilerParams(dimension_semantics=sem, vmem_limit_bytes=VMEM_LIMIT)


def _resident(shape):
    return pl.BlockSpec(shape, lambda *_: (0,) * len(shape), pipeline_mode=pl.Buffered(1))


def _rms(x, g):
    return x * lax.rsqrt(jnp.mean(x * x, axis=-1, keepdims=True) + RMS_EPS) * g


def _dot(a, b):
    return jnp.dot(a, b, preferred_element_type=F32)


def _ffn_kernel(*refs, ff_chunk, n_chunk, pre, post):
    if pre:
        x_ref, att_ref, hy_ref, woa_ref, woh_ref = refs[:5]
        refs = refs[5:]
    else:
        x_ref = refs[0]
        refs = refs[1:]
    g_ref, wg_ref, wu_ref, wd_ref = refs[:4]
    refs = refs[4:]
    if post:
        fg_ref, o_ref = refs
    else:
        (o_ref,) = refs

    x = x_ref[...]
    if pre:
        x = x + _dot(att_ref[...], woa_ref[...]) + _dot(hy_ref[...], woh_ref[...])
    xn = _rms(x, g_ref[...]).astype(BF16)
    acc = jnp.zeros(x.shape, F32)
    for c in range(n_chunk):
        sl = slice(c * ff_chunk, (c + 1) * ff_chunk)
        gate = _dot(xn, wg_ref[:, sl])
        up = _dot(xn, wu_ref[:, sl])
        h = (jax.nn.silu(gate) * up).astype(BF16)
        acc = acc + _dot(h, wd_ref[sl, :])
    y = x + 0.5 * acc
    if post:
        y = _rms(y, fg_ref[...])
    o_ref[...] = y


def _ffn(x, norm_g, wg, wu, wd, *, pre=None, final_g=None, tm=512, ff_chunk=256):
    m, d = x.shape
    dff = wg.shape[1]
    n_chunk = dff // ff_chunk
    assert n_chunk * ff_chunk == dff and m % tm == 0
    row = lambda w: pl.BlockSpec((tm, w), lambda i: (i, 0))
    args, specs = [x], [row(d)]
    if pre is not None:
        att, hy, woa, woh = pre
        args += [att, hy, woa, woh]
        specs += [row(att.shape[1]), row(hy.shape[1]), _resident(woa.shape), _resident(woh.shape)]
    args += [norm_g.reshape(1, d), wg, wu, wd]
    specs += [_resident((1, d)), _resident(wg.shape), _resident(wu.shape), _resident(wd.shape)]
    if final_g is not None:
        args.append(final_g.reshape(1, d))
        specs.append(_resident((1, d)))
    kern = functools.partial(_ffn_kernel, ff_chunk=ff_chunk, n_chunk=n_chunk,
                             pre=pre is not None, post=final_g is not None)
    return pl.pallas_call(
        kern, grid=(m // tm,), in_specs=specs, out_specs=row(d),
        out_shape=jax.ShapeDtypeStruct((m, d), F32),
        compiler_params=_params("parallel"), name="ffn_pre" if pre is not None else "ffn",
    )(*args)


def _mix_kernel(x_ref, g_ref, w_ref, qkv_ref, z_ref, *, d_att, col_chunk, scale):
    xn = _rms(x_ref[...], g_ref[...]).astype(BF16)
    n_att = 3 * d_att
    for c0 in range(0, w_ref.shape[1], col_chunk):
        p = _dot(xn, w_ref[:, c0:c0 + col_chunk])
        if c0 < n_att:
            if c0 < d_att:
                p = p * scale
            qkv_ref[:, c0:c0 + col_chunk] = p.astype(BF16)
        else:
            z_ref[:, c0 - n_att:c0 - n_att + col_chunk] = p.astype(z_ref.dtype)


def _mix(x, norm_g, w_in, *, d_att, tm=512, col_chunk=512):
    m, d = x.shape
    d_in = w_in.shape[1]
    n_att = 3 * d_att
    assert d_att % col_chunk == 0 and d_in % col_chunk == 0 and m % tm == 0
    kern = functools.partial(_mix_kernel, d_att=d_att, col_chunk=col_chunk,
                             scale=DIFF_HEAD_DIM ** -0.5 * LOG2_E)
    return pl.pallas_call(
        kern, grid=(m // tm,),
        in_specs=[pl.BlockSpec((tm, d), lambda i: (i, 0)), _resident((1, d)), _resident(w_in.shape)],
        out_specs=[pl.BlockSpec((tm, n_att), lambda i: (i, 0)),
                   pl.BlockSpec((tm, d_in - n_att), lambda i: (i, 0))],
        out_shape=[jax.ShapeDtypeStruct((m, n_att), BF16),
                   jax.ShapeDtypeStruct((m, d_in - n_att), BF16)],
        compiler_params=_params("parallel"), name="mix",
    )(x, norm_g.reshape(1, d), w_in)


N_BIAS_TILES = 5


def _bias_kernel(tab_ref, o_ref, *, t):
    h = pl.program_id(0)
    d = pl.program_id(1) - N_BIAS_TILES // 2
    half = REL_BUCKETS // 2
    max_exact = half // 2
    rel = lax.broadcasted_iota(jnp.int32, (8, 2 * t), 1) + (d - 1) * t
    ret = jnp.where(rel > 0, half, 0)
    n = jnp.abs(rel)
    nf = jnp.maximum(n, 1).astype(F32)
    large = max_exact + (jnp.log(nf / max_exact) / math.log(REL_MAX_DIST / max_exact)
                         * (half - max_exact)).astype(jnp.int32)
    large = jnp.minimum(large, half - 1)
    bucket = ret + jnp.where(n < max_exact, n, large)
    row = jnp.zeros(rel.shape, F32)
    for b in range(REL_BUCKETS):
        row = jnp.where(bucket == b, tab_ref[b, h], row)
    rows = jnp.broadcast_to(row[0:1, :] * LOG2_E, (t, 2 * t))
    o_ref[...] = pltpu.roll(rows, 0, 1, stride=1, stride_axis=0)[:, t:2 * t]


def _bias_tiles(rel_bias, t):
    nh = rel_bias.shape[1]
    return pl.pallas_call(
        functools.partial(_bias_kernel, t=t), grid=(nh, N_BIAS_TILES),
        in_specs=[pl.BlockSpec(memory_space=pltpu.SMEM)],
        out_specs=pl.BlockSpec((None, None, t, t), lambda h, d: (h, d, 0, 0)),
        out_shape=jax.ShapeDtypeStruct((nh, N_BIAS_TILES, t, t), F32),
        compiler_params=_params("parallel", "parallel"), name="rel_bias_tiles",
    )(rel_bias)


def _attn_kernel(q_ref, k_ref, v_ref, bt_ref, lq1_ref, lk1_ref, lq2_ref, lk2_ref, sg_ref,
                 o_ref, q2_ref, vx_ref, *, t, nk, n_chain, lambda_init):
    i = pl.program_id(2)
    dv = v_ref.shape[1]

    @pl.when(i == 0)
    def _():
        vx_ref[:, 0:dv] = v_ref[...]
        ones_lane = lax.broadcasted_iota(jnp.int32, (v_ref.shape[0], dv), 1) == 0
        vx_ref[:, dv:2 * dv] = jnp.where(ones_lane, 1.0, 0.0).astype(vx_ref.dtype)

    q = q_ref[...]
    lane = lax.broadcasted_iota(jnp.int32, q.shape, 1)
    zero = jnp.zeros_like(q)
    q2_ref[0:t, :] = jnp.where(lane < DIFF_HEAD_DIM, q, zero)
    q2_ref[t:2 * t, :] = jnp.where(lane >= DIFF_HEAD_DIM, q, zero)
    far = N_BIAS_TILES // 2

    def scores(d, rows, n_keys=t):
        wrapped = i + d >= nk
        j = jnp.where(wrapped, i + d - nk, i + d)
        k = k_ref[pl.ds(pl.multiple_of(j * t, t), n_keys), :]
        s = lax.dot_general(q2_ref[rows, :], k, (((1,), (1,)), ((), ())), preferred_element_type=F32)
        cols = [s[:, c0:c0 + LANES] for c0 in range(0, n_keys, LANES)]
        if d <= 1 or d >= nk - 1:
            qr = slice(rows.start % t, rows.start % t + rows.stop - rows.start)
            bias = bt_ref[jnp.clip(j - i, -far, far) + far, qr, :]
            return j, [c + bias[:, n * LANES:(n + 1) * LANES] for n, c in enumerate(cols)], None
        side = jnp.where(wrapped, bt_ref[0, 0:1, 0:LANES], bt_ref[2 * far, 0:1, 0:LANES])
        return j, cols, side

    def tile(d, rows, m_prev, acc_prev):
        j, cols, side = scores(d, rows)
        shift = m_prev if side is None else m_prev - side
        rel = [(col - shift).astype(BF16) for col in cols]
        rm = jnp.max(functools.reduce(jnp.maximum, rel), axis=1, keepdims=True)
        delta = jnp.maximum(rm, 0.0)
        p = jnp.concatenate([jnp.exp2(x - delta) for x in rel], axis=1)
        d32 = jnp.broadcast_to(delta.astype(F32), m_prev.shape)
        alpha = jnp.exp2(-d32)
        pv = _dot(p, vx_ref[pl.ds(pl.multiple_of(j * t, t), t), :])
        acc_new = jnp.concatenate([alpha * acc_prev[:, 0:dv] + pv[:, 0:dv],
                                   alpha * acc_prev[:, dv:2 * dv] + pv[:, dv:2 * dv]], axis=1)
        return m_prev + d32, acc_new

    rc = 2 * t // n_chain
    chains = tuple(slice(r0, r0 + rc) for r0 in range(0, 2 * t, rc))
    state = []
    for rows in chains:
        m0 = jnp.max(scores(0, rows, LANES)[1][0], axis=1, keepdims=True)
        state.append((jnp.broadcast_to(m0, (rc, LANES)), jnp.zeros((rc, 2 * dv), F32)))
    for d in range(nk):
        state = [tile(d, rows, *st) for rows, st in zip(chains, state)]

    lam = (jnp.exp(jnp.sum(lq1_ref[...] * lk1_ref[...])) - jnp.exp(jnp.sum(lq2_ref[...] * lk2_ref[...]))
           + lambda_init)
    acc = jnp.concatenate([a for _, a in state], axis=0)
    o = acc[:, 0:dv] / acc[:, dv:dv + 1]
    o = o[0:t] - lam * o[t:2 * t]
    o = _rms(o, sg_ref[...]) * (1.0 - lambda_init)
    o_ref[...] = o.astype(o_ref.dtype)


def _attention(qkv, bias_tiles, lq1, lk1, lq2, lk2, subln, *, n_heads, lambda_init, t, n_chain):
    b, s, _ = qkv.shape
    nk = s // t
    assert nk * t == s and t % LANES == 0 and t + 1 >= 91 and (2 * t) % n_chain == 0
    kern = functools.partial(_attn_kernel, t=t, nk=nk, n_chain=n_chain, lambda_init=lambda_init)
    vec = lambda n: pl.BlockSpec((1, n), lambda b_, h, i: (0, 0))
    return pl.pallas_call(
        kern, grid=(b, n_heads, nk),
        in_specs=[
            pl.BlockSpec((None, t, V_HEAD_DIM), lambda b_, h, i: (b_, i, h)),
            pl.BlockSpec((None, s, V_HEAD_DIM), lambda b_, h, i: (b_, 0, n_heads + h)),
            pl.BlockSpec((None, s, V_HEAD_DIM), lambda b_, h, i: (b_, 0, 2 * n_heads + h)),
            pl.BlockSpec((None, N_BIAS_TILES, t, t), lambda b_, h, i: (h, 0, 0, 0)),
            vec(DIFF_HEAD_DIM), vec(DIFF_HEAD_DIM), vec(DIFF_HEAD_DIM), vec(DIFF_HEAD_DIM),
            vec(V_HEAD_DIM),
        ],
        out_specs=pl.BlockSpec((None, t, V_HEAD_DIM), lambda b_, h, i: (b_, i, h)),
        out_shape=jax.ShapeDtypeStruct((b, s, n_heads * V_HEAD_DIM), BF16),
        scratch_shapes=[pltpu.VMEM((2 * t, V_HEAD_DIM), BF16),
                        pltpu.VMEM((s, 2 * V_HEAD_DIM), BF16)],
        compiler_params=_params("parallel", "parallel", "arbitrary"), name="diff_attention",
    )(qkv, qkv, qkv, bias_tiles, lq1.reshape(1, -1), lk1.reshape(1, -1),
      lq2.reshape(1, -1), lk2.reshape(1, -1), subln.reshape(1, -1))


def _dft_tables(n_hi):
    n = n_hi * N_LO
    half = n_hi // 2
    hi = np.arange(n_hi)
    lo = np.arange(N_LO)
    f_hi = np.exp(-2j * np.pi * ((np.outer(hi, hi) % n_hi) / n_hi))
    fr, fi = f_hi.real, f_hi.imag
    s1_pair = np.block([[fr[:, :half], -fi[:, :half]], [fi[:, :half], fr[:, :half]]])
    s1_real = np.concatenate([fr[:, :half], fi[:, :half]], axis=0)
    ph = (hi[:, None, None] * lo[None, None, :] + n_hi * lo[None, :, None] * lo[None, None, :]) % n
    g = np.exp(-2j * np.pi * ph / n)
    g2 = np.concatenate([np.concatenate([g.real, -g.imag], axis=2),
                         np.concatenate([g.imag, g.real], axis=2)], axis=1)
    gh = np.stack([g2, np.swapaxes(g2, 1, 2)], axis=1)
    s4 = np.block([[fr[:half], fi[:half]], [-fi[:half], fr[:half]]]) / n
    cast = lambda a: jnp.asarray(a.astype(np.float32)).astype(BF16)
    return cast(s1_pair), cast(s1_real), cast(gh), cast(s4)


HALO_ROWS = 16


def _hy_pre_kernel(z_ref, zp_ref, zn_ref, w_ref, b_ref, vv_ref, x0_ref, *, c, n_blk):
    i = pl.program_id(1)
    z = z_ref[...].astype(F32)
    rows = z.shape[0]
    edge = HALO_ROWS
    z_m1 = pltpu.roll(z, 1, 0)
    z_p1 = pltpu.roll(z, rows - 1, 0)

    def emit(rs, zm, zc, zp):
        u = b_ref[...] + zm * w_ref[0:1, :] + zc * w_ref[1:2, :] + zp * w_ref[2:3, :]
        vv_ref[rs, :] = (u[:, 2 * c:3 * c] * u[:, c:2 * c]).astype(vv_ref.dtype)
        x0_ref[rs, :] = u[:, 0:c].astype(x0_ref.dtype)

    emit(slice(0, rows), z_m1, z, z_p1)
    r = lax.broadcasted_iota(jnp.int32, (edge, z.shape[1]), 0)
    prev_row = jnp.where(i > 0, zp_ref[HALO_ROWS - 1:HALO_ROWS, :].astype(F32), 0.0)
    next_row = jnp.where(i < n_blk - 1, zn_ref[0:1, :].astype(F32), 0.0)
    head, tail = slice(0, edge), slice(rows - edge, rows)
    emit(head, jnp.where(r == 0, prev_row, z_m1[head]), z[head], z_p1[head])
    emit(tail, z_m1[tail], z[tail], jnp.where(r == edge - 1, next_row, z_p1[tail]))


def _hy_pre(z, conv_w, conv_b, *, tt=512):
    b, s, c3 = z.shape
    c = c3 // 3
    n_blk = s // tt
    assert n_blk * tt == s and tt % HALO_ROWS == 0
    kern = functools.partial(_hy_pre_kernel, c=c, n_blk=n_blk)
    sub = tt // HALO_ROWS
    last = s // HALO_ROWS - 1
    row = pl.BlockSpec((None, tt, c), lambda b_, i: (b_, i, 0))
    return pl.pallas_call(
        kern, grid=(b, n_blk),
        in_specs=[
            pl.BlockSpec((None, tt, c3), lambda b_, i: (b_, i, 0)),
            pl.BlockSpec((None, HALO_ROWS, c3), lambda b_, i: (b_, jnp.maximum(i * sub - 1, 0), 0)),
            pl.BlockSpec((None, HALO_ROWS, c3), lambda b_, i: (b_, jnp.minimum((i + 1) * sub, last), 0)),
            pl.BlockSpec((3, c3), lambda b_, i: (0, 0)),
            pl.BlockSpec((1, c3), lambda b_, i: (0, 0)),
        ],
        out_specs=[row, row],
        out_shape=[jax.ShapeDtypeStruct((b, s, c), BF16), jax.ShapeDtypeStruct((b, s, c), BF16)],
        compiler_params=_params("parallel", "parallel"), name="hyena_pre",
    )(z, z, z, conv_w, conv_b.reshape(1, c3))


def _split(a):
    hi = a.astype(BF16)
    return hi, (a - hi.astype(F32)).astype(BF16)


def _hdot(a, b, dims=(((1,), (0,)), ((), ()))):
    (ah, al), (bh, bl) = _split(a), _split(b)
    dg = functools.partial(lax.dot_general, dimension_numbers=dims, preferred_element_type=F32)
    return dg(ah, bh) + dg(ah, bl) + dg(al, bh)


def _filter_kernel(zt_ref, t_ref, w1_ref, b1_ref, w2_ref, b2_ref, w3_ref, b3_ref, w4_ref, fr_ref, dl_ref,
                   hf_ref, hb_ref, *, c):
    i = pl.program_id(0)
    fr = fr_ref[...]
    a = jnp.sin(fr * (_hdot(w1_ref[...], zt_ref[...]) + b1_ref[...]))
    a = jnp.sin(fr * (_hdot(w2_ref[...], a) + b2_ref[...]))
    a = jnp.sin(fr * (_hdot(w3_ref[...], a) + b3_ref[...]))
    hh = _hdot(a, w4_ref[...], (((0,), (0,)), ((), ())))
    decay = jnp.exp(-t_ref[...] * jnp.abs(dl_ref[...]))
    hf_ref[...] = (hh[:, 0:c] * decay).astype(hf_ref.dtype)
    r = lax.broadcasted_iota(jnp.int32, (hh.shape[0], c), 0)
    lag0 = jnp.logical_and(i == 0, r == 0)
    hb_ref[...] = jnp.where(lag0, 0.0, hh[:, c:2 * c] * decay).astype(hb_ref.dtype)


def _filter_taps(z, w1, b1, w2, b2, w3, b3, w4, freq, deltas, *, tb=512):
    s, emb = z.shape
    c = deltas.shape[0]
    assert s % tb == 0
    col = lambda v: v.reshape(-1, 1)
    args = [jnp.pad(z, ((0, 0), (0, LANES - emb))).T, z[:, 0:1],
            jnp.pad(w1, ((0, LANES - emb), (0, 0))).T, col(b1), w2.T, col(b2), w3.T, col(b3), w4,
            col(freq), deltas.reshape(1, c)]
    specs = [pl.BlockSpec((LANES, tb), lambda i: (0, i)), pl.BlockSpec((tb, 1), lambda i: (i, 0))] + [
        pl.BlockSpec(a.shape, lambda i: (0, 0)) for a in args[2:]]
    out = pl.BlockSpec((tb, c), lambda i: (i, 0))
    shp = jax.ShapeDtypeStruct((s, c), BF16)
    return pl.pallas_call(
        functools.partial(_filter_kernel, c=c), grid=(s // tb,),
        in_specs=specs, out_specs=[out, out], out_shape=[shp, shp],
        compiler_params=_params("parallel"), name="hyena_filter_taps",
    )(*args)


def _lo_major(ref):
    return jnp.swapaxes(ref[...], 0, 1)


def _dft1_kernel(x0_ref, x1_ref, f_ref, a_ref):
    n_out = a_ref.shape[0]
    x0, x1 = _lo_major(x0_ref), _lo_major(x1_ref)
    a = jnp.stack([_dot(f_ref[...], jnp.concatenate([x0[j], x1[j]], axis=0)).astype(a_ref.dtype)
                   for j in range(x0.shape[0])], axis=0)
    a = jnp.swapaxes(a, 0, 1)
    a_ref[:, 0] = a[0:n_out]
    a_ref[:, 1] = a[n_out:2 * n_out]


def _dft1_pair(xv, f_mat, *, n_hi, t_lo=16):
    b, half, _, c = xv.shape
    spec = lambda off: pl.BlockSpec((None, half, t_lo, c), lambda p, j: (2 * p + off, 0, j, 0))
    return pl.pallas_call(
        _dft1_kernel, grid=(b // 2, N_LO // t_lo),
        in_specs=[spec(0), spec(1), pl.BlockSpec(f_mat.shape, lambda p, j: (0, 0))],
        out_specs=pl.BlockSpec((None, n_hi, 2, t_lo, c), lambda p, j: (p, 0, 0, j, 0)),
        out_shape=jax.ShapeDtypeStruct((b // 2, n_hi, 2, N_LO, c), BF16),
        compiler_params=_params("parallel", "parallel"), name="hyena_dft1",
    )(xv, xv, f_mat)


def _dft1_filter_kernel(hf_ref, hb_ref, f_ref, fa_ref):
    n_out = fa_ref.shape[0]
    c = hf_ref.shape[2]
    hf, hb = _lo_major(hf_ref), _lo_major(hb_ref)
    a = jnp.stack([_dot(f_ref[...], jnp.concatenate([hf[j], hb[j]], axis=1)).astype(fa_ref.dtype)
                   for j in range(hf.shape[0])], axis=0)
    a = jnp.swapaxes(a, 0, 1)
    for part, (r0, c0) in enumerate(((0, 0), (n_out, 0), (0, c), (n_out, c))):
        fa_ref[:, part] = a[r0:r0 + n_out, :, c0:c0 + c]


def _dft1_filter(hf, hb, f_mat, *, n_hi, t_lo=16):
    half, _, c = hf.shape
    blk = pl.BlockSpec((half, t_lo, c), lambda j: (0, j, 0))
    return pl.pallas_call(
        _dft1_filter_kernel, grid=(N_LO // t_lo,),
        in_specs=[blk, blk, pl.BlockSpec(f_mat.shape, lambda j: (0, 0))],
        out_specs=pl.BlockSpec((n_hi, 4, t_lo, c), lambda j: (0, 0, j, 0)),
        out_shape=jax.ShapeDtypeStruct((n_hi, 4, N_LO, c), BF16),
        compiler_params=_params("parallel"), name="hyena_filter_dft1",
    )(hf, hb, f_mat)


def _spectral_kernel(a_ref, fa_ref, gh_ref, cc_ref):
    c = a_ref.shape[-1]
    for u in range(a_ref.shape[0]):
        a = jnp.concatenate([a_ref[u].reshape(2 * N_LO, c), fa_ref[u, 0:2].reshape(2 * N_LO, c),
                             fa_ref[u, 2:4].reshape(2 * N_LO, c)], axis=1)
        x = _dot(gh_ref[u, 0], a)
        xr, xi = x[0:N_LO, 0:c], x[N_LO:2 * N_LO, 0:c]
        kr = x[0:N_LO, c:2 * c] + x[0:N_LO, 2 * c:3 * c]
        ki = x[N_LO:2 * N_LO, c:2 * c] - x[N_LO:2 * N_LO, 2 * c:3 * c]
        y = jnp.concatenate([xr * kr - xi * ki, xr * ki + xi * kr], axis=0).astype(BF16)
        cc_ref[u] = _dot(gh_ref[u, 1], y).astype(cc_ref.dtype).reshape(2, N_LO, c)


def _spectral(a, fa, gh, *, n_hi, kb=2):
    p, _, _, _, c = a.shape
    assert n_hi % kb == 0
    blk = pl.BlockSpec((None, kb, 2, N_LO, c), lambda q, k: (q, k, 0, 0, 0))
    return pl.pallas_call(
        _spectral_kernel, grid=(p, n_hi // kb),
        in_specs=[blk, pl.BlockSpec((kb, 4, N_LO, c), lambda q, k: (k, 0, 0, 0)),
                  pl.BlockSpec((kb, 2, 2 * N_LO, 2 * N_LO), lambda q, k: (k, 0, 0, 0))],
        out_specs=blk, out_shape=jax.ShapeDtypeStruct(a.shape, BF16),
        compiler_params=_params("parallel", "parallel"), name="hyena_spectral",
    )(a, fa, gh)


def _idft_kernel(cc_ref, m_ref, o_ref):
    half = o_ref.shape[1]
    cr, ci = jnp.swapaxes(cc_ref[:, 0], 0, 1), jnp.swapaxes(cc_ref[:, 1], 0, 1)
    y = jnp.stack([_dot(m_ref[...], jnp.concatenate([cr[j], ci[j]], axis=0)) for j in range(cr.shape[0])],
                  axis=0)
    y = jnp.swapaxes(y, 0, 1)
    o_ref[0] = y[0:half]
    o_ref[1] = y[half:2 * half]


def _idft(cc, s4, *, n_hi, t_lo=16):
    p, _, _, _, c = cc.shape
    half = n_hi // 2
    return pl.pallas_call(
        _idft_kernel, grid=(p, N_LO // t_lo),
        in_specs=[pl.BlockSpec((None, n_hi, 2, t_lo, c), lambda q, j: (q, 0, 0, j, 0)),
                  pl.BlockSpec(s4.shape, lambda q, j: (0, 0))],
        out_specs=pl.BlockSpec((2, half, t_lo, c), lambda q, j: (q, 0, j, 0)),
        out_shape=jax.ShapeDtypeStruct((2 * p, half, N_LO, c), F32),
        compiler_params=_params("parallel", "parallel"), name="hyena_idft",
    )(cc, s4)


def _hy_post_kernel(y_ref, vv_ref, x0_ref, fb_ref, og_ref, mg_ref, o_ref):
    vv = vv_ref[...].astype(F32)
    y = (y_ref[...] + vv * fb_ref[...]) * x0_ref[...].astype(F32)
    sq = y * y
    hi = sq.astype(BF16)
    lo = (sq - hi.astype(F32)).astype(BF16)
    ms = _dot(hi, mg_ref[...]) + _dot(lo, mg_ref[...])
    o_ref[...] = (y * lax.rsqrt(ms + RMS_EPS) * og_ref[...]).astype(o_ref.dtype)


def _hy_post(y, vv, x0, filt_bias, out_g, *, tt=512):
    b, s, c = vv.shape
    gdim = c // N_HYENA_GROUPS
    grp = np.arange(c) // gdim
    assert gdim & (gdim - 1) == 0
    mg = jnp.asarray((grp[:, None] == grp[None, :]).astype(np.float32) / gdim).astype(BF16)
    row = pl.BlockSpec((None, tt, c), lambda b_, i: (b_, i, 0))
    vec = pl.BlockSpec((1, c), lambda b_, i: (0, 0))
    return pl.pallas_call(
        _hy_post_kernel, grid=(b, s // tt),
        in_specs=[row, row, row, vec, vec, pl.BlockSpec((c, c), lambda b_, i: (0, 0))],
        out_specs=row, out_shape=jax.ShapeDtypeStruct((b, s, c), BF16),
        compiler_params=_params("parallel", "parallel"), name="hyena_post",
    )(y, vv, x0, filt_bias.reshape(1, c), out_g.reshape(1, c), mg)


def _filter_positions(s):
    t = jnp.linspace(0.0, 1.0, s, dtype=F32)[:, None]
    bands = (HYENA_EMB_DIM - 1) // 2
    w = 2.0 * math.pi * jnp.arange(s, dtype=F32)[:, None] / s
    f = jnp.linspace(1e-4, bands - 1, bands, dtype=F32)[None, :]
    fw = f * w
    return jnp.concatenate([t, jnp.cos(fw), -jnp.sin(fw)], axis=-1)


def _hyena(z_hy, conv_w, conv_b, w1, b1, w2, b2, w3, b3, w4, freq, filt_bias, out_g):
    b, s, c3 = z_hy.shape
    c = c3 // 3
    n_hi = 2 * s // N_LO
    half = n_hi // 2
    assert b % 2 == 0 and n_hi * N_LO == 2 * s
    s1_pair, s1_real, gh, s4 = _dft_tables(n_hi)

    max_decay = math.log(HYENA_DECAY_TARGET) / HYENA_FAST_DECAY
    min_decay = math.log(HYENA_DECAY_TARGET) / HYENA_SLOW_DECAY
    deltas = jnp.linspace(min_decay, max_decay, c, dtype=F32)
    hf, hb = _filter_taps(_filter_positions(s), w1, b1, w2, b2, w3, b3, w4, freq, deltas)
    fa = _dft1_filter(hf.reshape(half, N_LO, c), hb.reshape(half, N_LO, c), s1_real, n_hi=n_hi)

    vv, x0 = _hy_pre(z_hy, conv_w, conv_b)
    a = _dft1_pair(vv.reshape(b, half, N_LO, c), s1_pair, n_hi=n_hi)
    cc = _spectral(a, fa, gh, n_hi=n_hi)
    y = _idft(cc, s4, n_hi=n_hi).reshape(b, s, c)
    return _hy_post(y, vv, x0, filt_bias, out_g)


def kernel(x, rel_bias, ffn1_norm, ffn1_w_gate, ffn1_w_up, ffn1_w_down, mix_norm, w_in,
           lambda_q1, lambda_k1, lambda_q2, lambda_k2, diff_subln,
           hy_conv_w, hy_conv_b, hy_f_w1, hy_f_b1, hy_f_w2, hy_f_b2, hy_f_w3, hy_f_b3,
           hy_f_w4, hy_f_freq, hy_bias, hy_out_norm, w_out,
           ffn2_norm, ffn2_w_gate, ffn2_w_up, ffn2_w_down, final_norm):
    b, s, d = x.shape
    depth = w_in.shape[0]
    d_att = diff_subln.shape[1] * rel_bias.shape[1]
    n_heads = rel_bias.shape[1]
    attn_tile = min(512, s)
    bf = lambda a: a.astype(BF16)

    xf = x.reshape(b * s, d)
    for l in range(depth):
        last = l == depth - 1
        xf = _ffn(xf, ffn1_norm[l], bf(ffn1_w_gate[l]), bf(ffn1_w_up[l]), bf(ffn1_w_down[l]))
        qkv, z_hy = _mix(xf, mix_norm[l], bf(w_in[l]), d_att=d_att)
        lambda_init = 0.8 - 0.6 * math.exp(-0.3 * l)
        att = _attention(qkv.reshape(b, s, -1), _bias_tiles(rel_bias, attn_tile),
                         lambda_q1[l], lambda_k1[l], lambda_q2[l], lambda_k2[l], diff_subln[l],
                         n_heads=n_heads, lambda_init=lambda_init, t=attn_tile, n_chain=4)
        hy = _hyena(z_hy.reshape(b, s, -1), hy_conv_w[l], hy_conv_b[l], hy_f_w1[l], hy_f_b1[l],
                    hy_f_w2[l], hy_f_b2[l], hy_f_w3[l], hy_f_b3[l], hy_f_w4[l], hy_f_freq[l],
                    hy_bias[l], hy_out_norm[l])
        wo = bf(w_out[l])
        xf = _ffn(xf, ffn2_norm[l], bf(ffn2_w_gate[l]), bf(ffn2_w_up[l]), bf(ffn2_w_down[l]),
                  pre=(att.reshape(b * s, -1), hy.reshape(b * s, -1), wo[:d_att], wo[d_att:]),
                  final_g=final_norm if last else None)
    if depth == 0:
        raise ValueError("depth must be positive")
    return xf.reshape(b, s, d)
```

```python
import functools
import math

import numpy as np
import jax
import jax.numpy as jnp
from jax import lax
from jax.experimental import pallas as pl
from jax.experimental.pallas import tpu as pltpu

F32 = jnp.float32
BF16 = jnp.bfloat16

RMS_EPS = 1e-6
LANES = 128
DIFF_HEAD_DIM = 64
V_HEAD_DIM = 2 * DIFF_HEAD_DIM
REL_BUCKETS = 32
REL_MAX_DIST = 128
N_HYENA_GROUPS = 8
HYENA_EMB_DIM = 33
HYENA_DECAY_TARGET = 1e-2
HYENA_FAST_DECAY = 0.3
HYENA_SLOW_DECAY = 1.5
N_LO = 128
LOG2_E = math.log2(math.e)
VMEM_LIMIT = 56 * 1024 * 1024


def _params(*sem):
    return pltpu.CompilerParams(dimension_semantics=sem, vmem_limit_bytes=VMEM_LIMIT)


def _resident(shape):
    return pl.BlockSpec(shape, lambda *_: (0,) * len(shape), pipeline_mode=pl.Buffered(1))


def _rms(x, g):
    return x * lax.rsqrt(jnp.mean(x * x, axis=-1, keepdims=True) + RMS_EPS) * g


def _dot(a, b):
    return jnp.dot(a, b, preferred_element_type=F32)


def _split(a):
    hi = a.astype(BF16)
    return hi, (a - hi.astype(F32)).astype(BF16)


def _hyena_gate_norm(y_ref, vv_ref, x0_ref, fb_ref, og_ref, mg_ref):
    vv = vv_ref[...].astype(F32)
    y = (y_ref[...] + vv * fb_ref[...]) * x0_ref[...].astype(F32)
    hi, lo = _split(y * y)
    ms = _dot(hi, mg_ref[...]) + _dot(lo, mg_ref[...])
    return y * lax.rsqrt(ms + RMS_EPS) * og_ref[...]


def _ffn_kernel(*refs, ff_chunk, n_chunk, pre, post):
    if pre:
        x_ref, att_ref, woa_ref, woh_ref = refs[:4]
        hy_refs = refs[4:10]
        refs = refs[10:]
    else:
        x_ref = refs[0]
        refs = refs[1:]
    g_ref, wg_ref, wu_ref, wd_ref = refs[:4]
    refs = refs[4:]
    if post:
        fg_ref, o_ref = refs
    else:
        (o_ref,) = refs

    x = x_ref[...]
    if pre:
        hy = _hyena_gate_norm(*hy_refs).astype(BF16)
        x = x + _dot(att_ref[...], woa_ref[...]) + _dot(hy, woh_ref[...])
    xn = _rms(x, g_ref[...]).astype(BF16)
    acc = jnp.zeros(x.shape, F32)
    for c in range(n_chunk):
        sl = slice(c * ff_chunk, (c + 1) * ff_chunk)
        gate = _dot(xn, wg_ref[:, sl])
        up = _dot(xn, wu_ref[:, sl])
        h = (jax.nn.silu(gate) * up).astype(BF16)
        acc = acc + _dot(h, wd_ref[sl, :])
    y = x + 0.5 * acc
    if post:
        y = _rms(y, fg_ref[...])
    o_ref[...] = y


def _ffn(x, norm_g, wg, wu, wd, *, pre=None, final_g=None, tm=512, ff_chunk=256):
    m, d = x.shape
    dff = wg.shape[1]
    n_chunk = dff // ff_chunk
    assert n_chunk * ff_chunk == dff and m % tm == 0
    row = lambda w: pl.BlockSpec((tm, w), lambda i: (i, 0))
    args, specs = [x], [row(d)]
    if pre is not None:
        att, woa, woh, (y, vv, x0, filt_bias, out_g) = pre
        c = y.shape[1]
        gdim = c // N_HYENA_GROUPS
        assert gdim & (gdim - 1) == 0
        grp = np.arange(c) // gdim
        mg = jnp.asarray((grp[:, None] == grp[None, :]).astype(np.float32) / gdim).astype(BF16)
        args += [att, woa, woh, y, vv, x0, filt_bias.reshape(1, c), out_g.reshape(1, c), mg]
        specs += [row(att.shape[1]), _resident(woa.shape), _resident(woh.shape), row(c), row(c), row(c),
                  _resident((1, c)), _resident((1, c)), _resident((c, c))]
    args += [norm_g.reshape(1, d), wg, wu, wd]
    specs += [_resident((1, d)), _resident(wg.shape), _resident(wu.shape), _resident(wd.shape)]
    if final_g is not None:
        args.append(final_g.reshape(1, d))
        specs.append(_resident((1, d)))
    kern = functools.partial(_ffn_kernel, ff_chunk=ff_chunk, n_chunk=n_chunk,
                             pre=pre is not None, post=final_g is not None)
    return pl.pallas_call(
        kern, grid=(m // tm,), in_specs=specs, out_specs=row(d),
        out_shape=jax.ShapeDtypeStruct((m, d), F32),
        compiler_params=_params("parallel"), name="ffn_pre" if pre is not None else "ffn",
    )(*args)


def _mix_kernel(x_ref, g_ref, w_ref, qkv_ref, z_ref, *, d_att, col_chunk, scale):
    xn = _rms(x_ref[...], g_ref[...]).astype(BF16)
    n_att = 3 * d_att
    for c0 in range(0, w_ref.shape[1], col_chunk):
        p = _dot(xn, w_ref[:, c0:c0 + col_chunk])
        if c0 < n_att:
            if c0 < d_att:
                p = p * scale
            qkv_ref[:, c0:c0 + col_chunk] = p.astype(BF16)
        else:
            z_ref[:, c0 - n_att:c0 - n_att + col_chunk] = p.astype(z_ref.dtype)


def _mix(x, norm_g, w_in, *, d_att, tm=512, col_chunk=512):
    m, d = x.shape
    d_in = w_in.shape[1]
    n_att = 3 * d_att
    assert d_att % col_chunk == 0 and d_in % col_chunk == 0 and m % tm == 0
    kern = functools.partial(_mix_kernel, d_att=d_att, col_chunk=col_chunk,
                             scale=DIFF_HEAD_DIM ** -0.5 * LOG2_E)
    return pl.pallas_call(
        kern, grid=(m // tm,),
        in_specs=[pl.BlockSpec((tm, d), lambda i: (i, 0)), _resident((1, d)), _resident(w_in.shape)],
        out_specs=[pl.BlockSpec((tm, n_att), lambda i: (i, 0)),
                   pl.BlockSpec((tm, d_in - n_att), lambda i: (i, 0))],
        out_shape=[jax.ShapeDtypeStruct((m, n_att), BF16),
                   jax.ShapeDtypeStruct((m, d_in - n_att), BF16)],
        compiler_params=_params("parallel"), name="mix",
    )(x, norm_g.reshape(1, d), w_in)


N_BIAS_TILES = 5


def _bias_kernel(tab_ref, o_ref, *, t):
    h = pl.program_id(0)
    d = pl.program_id(1) - N_BIAS_TILES // 2
    half = REL_BUCKETS // 2
    max_exact = half // 2
    rel = lax.broadcasted_iota(jnp.int32, (8, 2 * t), 1) + (d - 1) * t
    ret = jnp.where(rel > 0, half, 0)
    n = jnp.abs(rel)
    nf = jnp.maximum(n, 1).astype(F32)
    large = max_exact + (jnp.log(nf / max_exact) / math.log(REL_MAX_DIST / max_exact)
                         * (half - max_exact)).astype(jnp.int32)
    large = jnp.minimum(large, half - 1)
    bucket = ret + jnp.where(n < max_exact, n, large)
    row = jnp.zeros(rel.shape, F32)
    for b in range(REL_BUCKETS):
        row = jnp.where(bucket == b, tab_ref[b, h], row)
    rows = jnp.broadcast_to(row[0:1, :] * LOG2_E, (t, 2 * t))
    o_ref[...] = pltpu.roll(rows, 0, 1, stride=1, stride_axis=0)[:, t:2 * t]


def _bias_tiles(rel_bias, t):
    nh = rel_bias.shape[1]
    return pl.pallas_call(
        functools.partial(_bias_kernel, t=t), grid=(nh, N_BIAS_TILES),
        in_specs=[pl.BlockSpec(memory_space=pltpu.SMEM)],
        out_specs=pl.BlockSpec((None, None, t, t), lambda h, d: (h, d, 0, 0)),
        out_shape=jax.ShapeDtypeStruct((nh, N_BIAS_TILES, t, t), F32),
        compiler_params=_params("parallel", "parallel"), name="rel_bias_tiles",
    )(rel_bias)


def _attn_kernel(q_ref, k_ref, v_ref, bt_ref, lq1_ref, lk1_ref, lq2_ref, lk2_ref, sg_ref,
                 o_ref, q2_ref, vx_ref, *, t, nk, n_q, n_chain, lambda_init):
    g = pl.program_id(2)
    dv = v_ref.shape[1]

    @pl.when(g == 0)
    def _():
        vx_ref[:, 0:dv] = v_ref[...]
        ones_lane = lax.broadcasted_iota(jnp.int32, (v_ref.shape[0], dv), 1) == 0
        vx_ref[:, dv:2 * dv] = jnp.where(ones_lane, 1.0, 0.0).astype(vx_ref.dtype)

    far = N_BIAS_TILES // 2
    rc = 2 * t // n_chain
    lam = (jnp.exp(jnp.sum(lq1_ref[...] * lk1_ref[...])) - jnp.exp(jnp.sum(lq2_ref[...] * lk2_ref[...]))
           + lambda_init)

    def scores(i, base, d, rows, n_keys=t):
        wrapped = i + d >= nk
        j = jnp.where(wrapped, i + d - nk, i + d)
        k = k_ref[pl.ds(pl.multiple_of(j * t, t), n_keys), :]
        q2 = q2_ref[base + rows.start:base + rows.stop, :]
        s = lax.dot_general(q2, k, (((1,), (1,)), ((), ())), preferred_element_type=F32)
        cols = [s[:, c0:c0 + LANES] for c0 in range(0, n_keys, LANES)]
        if d <= 1 or d >= nk - 1:
            qr = slice(rows.start % t, rows.start % t + rows.stop - rows.start)
            bias = bt_ref[jnp.clip(j - i, -far, far) + far, qr, :]
            return j, [c + bias[:, n * LANES:(n + 1) * LANES] for n, c in enumerate(cols)], None
        side = jnp.where(wrapped, bt_ref[0, 0:1, 0:LANES], bt_ref[2 * far, 0:1, 0:LANES])
        return j, cols, side

    def tile(i, base, d, rows, m_prev, acc_prev):
        j, cols, side = scores(i, base, d, rows)
        shift = m_prev if side is None else m_prev - side
        rel = [(col - shift).astype(BF16) for col in cols]
        rm = jnp.max(functools.reduce(jnp.maximum, rel), axis=1, keepdims=True)
        delta = jnp.maximum(rm, 0.0)
        p = jnp.concatenate([jnp.exp2(x - delta) for x in rel], axis=1)
        d32 = jnp.broadcast_to(delta.astype(F32), m_prev.shape)
        alpha = jnp.exp2(-d32)
        pv = _dot(p, vx_ref[pl.ds(pl.multiple_of(j * t, t), t), :])
        acc_new = jnp.concatenate([alpha * acc_prev[:, 0:dv] + pv[:, 0:dv],
                                   alpha * acc_prev[:, dv:2 * dv] + pv[:, dv:2 * dv]], axis=1)
        return m_prev + d32, acc_new

    streams = []
    for a in range(n_q):
        i, base = g * n_q + a, a * 2 * t
        q = q_ref[a * t:(a + 1) * t, :]
        lane = lax.broadcasted_iota(jnp.int32, q.shape, 1)
        zero = jnp.zeros_like(q)
        q2_ref[base:base + t, :] = jnp.where(lane < DIFF_HEAD_DIM, q, zero)
        q2_ref[base + t:base + 2 * t, :] = jnp.where(lane >= DIFF_HEAD_DIM, q, zero)
        for r0 in range(0, 2 * t, rc):
            streams.append((i, base, slice(r0, r0 + rc)))
    state = []
    for i, base, rows in streams:
        m0 = jnp.max(scores(i, base, 0, rows, LANES)[1][0], axis=1, keepdims=True)
        state.append((jnp.broadcast_to(m0, (rc, LANES)), jnp.zeros((rc, 2 * dv), F32)))
    for d in range(nk):
        state = [tile(i, base, d, rows, *st) for (i, base, rows), st in zip(streams, state)]

    for a in range(n_q):
        acc = jnp.concatenate([acc for _, acc in state[a * n_chain:(a + 1) * n_chain]], axis=0)
        o = acc[:, 0:dv] / acc[:, dv:dv + 1]
        o = o[0:t] - lam * o[t:2 * t]
        o = _rms(o, sg_ref[...]) * (1.0 - lambda_init)
        o_ref[a * t:(a + 1) * t, :] = o.astype(o_ref.dtype)


def _attention(qkv, bias_tiles, lq1, lk1, lq2, lk2, subln, *, n_heads, lambda_init, t, n_chain, n_q):
    b, s, _ = qkv.shape
    nk = s // t
    assert nk * t == s and t % LANES == 0 and t + 1 >= 91 and (2 * t) % n_chain == 0 and nk % n_q == 0
    kern = functools.partial(_attn_kernel, t=t, nk=nk, n_q=n_q, n_chain=n_chain, lambda_init=lambda_init)
    vec = lambda n: pl.BlockSpec((1, n), lambda b_, h, i: (0, 0))
    return pl.pallas_call(
        kern, grid=(b, n_heads, nk // n_q),
        in_specs=[
            pl.BlockSpec((None, n_q * t, V_HEAD_DIM), lambda b_, h, i: (b_, i, h)),
            pl.BlockSpec((None, s, V_HEAD_DIM), lambda b_, h, i: (b_, 0, n_heads + h)),
            pl.BlockSpec((None, s, V_HEAD_DIM), lambda b_, h, i: (b_, 0, 2 * n_heads + h)),
            pl.BlockSpec((None, N_BIAS_TILES, t, t), lambda b_, h, i: (h, 0, 0, 0)),
            vec(DIFF_HEAD_DIM), vec(DIFF_HEAD_DIM), vec(DIFF_HEAD_DIM), vec(DIFF_HEAD_DIM),
            vec(V_HEAD_DIM),
        ],
        out_specs=pl.BlockSpec((None, n_q * t, V_HEAD_DIM), lambda b_, h, i: (b_, i, h)),
        out_shape=jax.ShapeDtypeStruct((b, s, n_heads * V_HEAD_DIM), BF16),
        scratch_shapes=[pltpu.VMEM((n_q * 2 * t, V_HEAD_DIM), BF16),
                        pltpu.VMEM((s, 2 * V_HEAD_DIM), BF16)],
        compiler_params=_params("parallel", "parallel", "arbitrary"), name="diff_attention",
    )(qkv, qkv, qkv, bias_tiles, lq1.reshape(1, -1), lk1.reshape(1, -1),
      lq2.reshape(1, -1), lk2.reshape(1, -1), subln.reshape(1, -1))


def _dft_tables(n_hi):
    n = n_hi * N_LO
    half = n_hi // 2
    hi = np.arange(n_hi)
    lo = np.arange(N_LO)
    f_hi = np.exp(-2j * np.pi * ((np.outer(hi, hi) % n_hi) / n_hi))
    fr, fi = f_hi.real, f_hi.imag
    s1_pair = np.block([[fr[:, :half], -fi[:, :half]], [fi[:, :half], fr[:, :half]]])
    s1_real = np.concatenate([fr[:, :half], fi[:, :half]], axis=0)
    ph = (hi[:, None, None] * lo[None, None, :] + n_hi * lo[None, :, None] * lo[None, None, :]) % n
    g = np.exp(-2j * np.pi * ph / n)
    g2 = np.concatenate([np.concatenate([g.real, -g.imag], axis=2),
                         np.concatenate([g.imag, g.real], axis=2)], axis=1)
    gh = np.stack([g2, np.swapaxes(g2, 1, 2)], axis=1)
    s4 = np.block([[fr[:half], fi[:half]], [-fi[:half], fr[:half]]]) / n
    cast = lambda a: jnp.asarray(a.astype(np.float32)).astype(BF16)
    return cast(s1_pair), cast(s1_real), cast(gh), cast(s4)


HALO_ROWS = 16


def _hy_pre_kernel(z_ref, zp_ref, zn_ref, w_ref, b_ref, vv_ref, x0_ref, *, c, n_blk):
    i = pl.program_id(1)
    z = z_ref[...].astype(F32)
    rows = z.shape[0]
    edge = HALO_ROWS
    z_m1 = pltpu.roll(z, 1, 0)
    z_p1 = pltpu.roll(z, rows - 1, 0)

    def emit(rs, zm, zc, zp):
        u = b_ref[...] + zm * w_ref[0:1, :] + zc * w_ref[1:2, :] + zp * w_ref[2:3, :]
        vv_ref[rs, :] = (u[:, 2 * c:3 * c] * u[:, c:2 * c]).astype(vv_ref.dtype)
        x0_ref[rs, :] = u[:, 0:c].astype(x0_ref.dtype)

    emit(slice(0, rows), z_m1, z, z_p1)
    r = lax.broadcasted_iota(jnp.int32, (edge, z.shape[1]), 0)
    prev_row = jnp.where(i > 0, zp_ref[HALO_ROWS - 1:HALO_ROWS, :].astype(F32), 0.0)
    next_row = jnp.where(i < n_blk - 1, zn_ref[0:1, :].astype(F32), 0.0)
    head, tail = slice(0, edge), slice(rows - edge, rows)
    emit(head, jnp.where(r == 0, prev_row, z_m1[head]), z[head], z_p1[head])
    emit(tail, z_m1[tail], z[tail], jnp.where(r == edge - 1, next_row, z_p1[tail]))


def _hy_pre(z, conv_w, conv_b, *, tt=512):
    b, s, c3 = z.shape
    c = c3 // 3
    n_blk = s // tt
    assert n_blk * tt == s and tt % HALO_ROWS == 0
    kern = functools.partial(_hy_pre_kernel, c=c, n_blk=n_blk)
    sub = tt // HALO_ROWS
    last = s // HALO_ROWS - 1
    row = pl.BlockSpec((None, tt, c), lambda b_, i: (b_, i, 0))
    return pl.pallas_call(
        kern, grid=(b, n_blk),
        in_specs=[
            pl.BlockSpec((None, tt, c3), lambda b_, i: (b_, i, 0)),
            pl.BlockSpec((None, HALO_ROWS, c3), lambda b_, i: (b_, jnp.maximum(i * sub - 1, 0), 0)),
            pl.BlockSpec((None, HALO_ROWS, c3), lambda b_, i: (b_, jnp.minimum((i + 1) * sub, last), 0)),
            pl.BlockSpec((3, c3), lambda b_, i: (0, 0)),
            pl.BlockSpec((1, c3), lambda b_, i: (0, 0)),
        ],
        out_specs=[row, row],
        out_shape=[jax.ShapeDtypeStruct((b, s, c), BF16), jax.ShapeDtypeStruct((b, s, c), BF16)],
        compiler_params=_params("parallel", "parallel"), name="hyena_pre",
    )(z, z, z, conv_w, conv_b.reshape(1, c3))


def _hdot(a, b, dims=(((1,), (0,)), ((), ()))):
    (ah, al), (bh, bl) = _split(a), _split(b)
    dg = functools.partial(lax.dot_general, dimension_numbers=dims, preferred_element_type=F32)
    return dg(ah, bh) + dg(ah, bl) + dg(al, bh)


def _filter_kernel(zt_ref, t_ref, w1_ref, b1_ref, w2_ref, b2_ref, w3_ref, b3_ref, w4_ref, fr_ref, dl_ref,
                   hf_ref, hb_ref, *, c):
    i = pl.program_id(0)
    fr = fr_ref[...]
    a = jnp.sin(fr * (_hdot(w1_ref[...], zt_ref[...]) + b1_ref[...]))
    a = jnp.sin(fr * (_hdot(w2_ref[...], a) + b2_ref[...]))
    a = jnp.sin(fr * (_hdot(w3_ref[...], a) + b3_ref[...]))
    hh = _hdot(a, w4_ref[...], (((0,), (0,)), ((), ())))
    decay = jnp.exp(-t_ref[...] * jnp.abs(dl_ref[...]))
    hf_ref[...] = (hh[:, 0:c] * decay).astype(hf_ref.dtype)
    r = lax.broadcasted_iota(jnp.int32, (hh.shape[0], c), 0)
    lag0 = jnp.logical_and(i == 0, r == 0)
    hb_ref[...] = jnp.where(lag0, 0.0, hh[:, c:2 * c] * decay).astype(hb_ref.dtype)


def _filter_taps(z, w1, b1, w2, b2, w3, b3, w4, freq, deltas, *, tb=512):
    s, emb = z.shape
    c = deltas.shape[0]
    assert s % tb == 0
    col = lambda v: v.reshape(-1, 1)
    args = [jnp.pad(z, ((0, 0), (0, LANES - emb))).T, z[:, 0:1],
            jnp.pad(w1, ((0, LANES - emb), (0, 0))).T, col(b1), w2.T, col(b2), w3.T, col(b3), w4,
            col(freq), deltas.reshape(1, c)]
    specs = [pl.BlockSpec((LANES, tb), lambda i: (0, i)), pl.BlockSpec((tb, 1), lambda i: (i, 0))] + [
        pl.BlockSpec(a.shape, lambda i: (0, 0)) for a in args[2:]]
    out = pl.BlockSpec((tb, c), lambda i: (i, 0))
    shp = jax.ShapeDtypeStruct((s, c), BF16)
    return pl.pallas_call(
        functools.partial(_filter_kernel, c=c), grid=(s // tb,),
        in_specs=specs, out_specs=[out, out], out_shape=[shp, shp],
        compiler_params=_params("parallel"), name="hyena_filter_taps",
    )(*args)


def _lo_major(ref):
    return jnp.swapaxes(ref[...], 0, 1)


def _dft1_kernel(x0_ref, x1_ref, f_ref, a_ref):
    n_out = a_ref.shape[0]
    x0, x1 = _lo_major(x0_ref), _lo_major(x1_ref)
    a = jnp.stack([_dot(f_ref[...], jnp.concatenate([x0[j], x1[j]], axis=0)).astype(a_ref.dtype)
                   for j in range(x0.shape[0])], axis=0)
    a = jnp.swapaxes(a, 0, 1)
    a_ref[:, 0] = a[0:n_out]
    a_ref[:, 1] = a[n_out:2 * n_out]


def _dft1_pair(xv, f_mat, *, n_hi, t_lo=16):
    b, half, _, c = xv.shape
    spec = lambda off: pl.BlockSpec((None, half, t_lo, c), lambda p, j: (2 * p + off, 0, j, 0))
    return pl.pallas_call(
        _dft1_kernel, grid=(b // 2, N_LO // t_lo),
        in_specs=[spec(0), spec(1), pl.BlockSpec(f_mat.shape, lambda p, j: (0, 0))],
        out_specs=pl.BlockSpec((None, n_hi, 2, t_lo, c), lambda p, j: (p, 0, 0, j, 0)),
        out_shape=jax.ShapeDtypeStruct((b // 2, n_hi, 2, N_LO, c), BF16),
        compiler_params=_params("parallel", "parallel"), name="hyena_dft1",
    )(xv, xv, f_mat)


def _dft1_filter_kernel(hf_ref, hb_ref, f_ref, fa_ref):
    n_out = fa_ref.shape[0]
    c = hf_ref.shape[2]
    hf, hb = _lo_major(hf_ref), _lo_major(hb_ref)
    a = jnp.stack([_dot(f_ref[...], jnp.concatenate([hf[j], hb[j]], axis=1)).astype(fa_ref.dtype)
                   for j in range(hf.shape[0])], axis=0)
    a = jnp.swapaxes(a, 0, 1)
    for part, (r0, c0) in enumerate(((0, 0), (n_out, 0), (0, c), (n_out, c))):
        fa_ref[:, part] = a[r0:r0 + n_out, :, c0:c0 + c]


def _dft1_filter(hf, hb, f_mat, *, n_hi, t_lo=16):
    half, _, c = hf.shape
    blk = pl.BlockSpec((half, t_lo, c), lambda j: (0, j, 0))
    return pl.pallas_call(
        _dft1_filter_kernel, grid=(N_LO // t_lo,),
        in_specs=[blk, blk, pl.BlockSpec(f_mat.shape, lambda j: (0, 0))],
        out_specs=pl.BlockSpec((n_hi, 4, t_lo, c), lambda j: (0, 0, j, 0)),
        out_shape=jax.ShapeDtypeStruct((n_hi, 4, N_LO, c), BF16),
        compiler_params=_params("parallel"), name="hyena_filter_dft1",
    )(hf, hb, f_mat)


def _spectral_kernel(a_ref, fa_ref, gh_ref, cc_ref):
    c = a_ref.shape[-1]
    for u in range(a_ref.shape[0]):
        a = jnp.concatenate([a_ref[u].reshape(2 * N_LO, c), fa_ref[u, 0:2].reshape(2 * N_LO, c),
                             fa_ref[u, 2:4].reshape(2 * N_LO, c)], axis=1)
        x = _dot(gh_ref[u, 0], a)
        xr, xi = x[0:N_LO, 0:c], x[N_LO:2 * N_LO, 0:c]
        kr = x[0:N_LO, c:2 * c] + x[0:N_LO, 2 * c:3 * c]
        ki = x[N_LO:2 * N_LO, c:2 * c] - x[N_LO:2 * N_LO, 2 * c:3 * c]
        y = jnp.concatenate([xr * kr - xi * ki, xr * ki + xi * kr], axis=0).astype(BF16)
        cc_ref[u] = _dot(gh_ref[u, 1], y).astype(cc_ref.dtype).reshape(2, N_LO, c)


def _spectral(a, fa, gh, *, n_hi, kb=4):
    p, _, _, _, c = a.shape
    assert n_hi % kb == 0
    blk = pl.BlockSpec((None, kb, 2, N_LO, c), lambda q, k: (q, k, 0, 0, 0))
    return pl.pallas_call(
        _spectral_kernel, grid=(p, n_hi // kb),
        in_specs=[blk, pl.BlockSpec((kb, 4, N_LO, c), lambda q, k: (k, 0, 0, 0)),
                  pl.BlockSpec((kb, 2, 2 * N_LO, 2 * N_LO), lambda q, k: (k, 0, 0, 0))],
        out_specs=blk, out_shape=jax.ShapeDtypeStruct(a.shape, BF16),
        compiler_params=_params("parallel", "parallel"), name="hyena_spectral",
    )(a, fa, gh)


def _idft_kernel(cc_ref, m_ref, o_ref):
    half = o_ref.shape[1]
    cr, ci = jnp.swapaxes(cc_ref[:, 0], 0, 1), jnp.swapaxes(cc_ref[:, 1], 0, 1)
    y = jnp.stack([_dot(m_ref[...], jnp.concatenate([cr[j], ci[j]], axis=0)) for j in range(cr.shape[0])],
                  axis=0)
    y = jnp.swapaxes(y, 0, 1)
    o_ref[0] = y[0:half]
    o_ref[1] = y[half:2 * half]


def _idft(cc, s4, *, n_hi, t_lo=16):
    p, _, _, _, c = cc.shape
    half = n_hi // 2
    return pl.pallas_call(
        _idft_kernel, grid=(p, N_LO // t_lo),
        in_specs=[pl.BlockSpec((None, n_hi, 2, t_lo, c), lambda q, j: (q, 0, 0, j, 0)),
                  pl.BlockSpec(s4.shape, lambda q, j: (0, 0))],
        out_specs=pl.BlockSpec((2, half, t_lo, c), lambda q, j: (q, 0, j, 0)),
        out_shape=jax.ShapeDtypeStruct((2 * p, half, N_LO, c), F32),
        compiler_params=_params("parallel", "parallel"), name="hyena_idft",
    )(cc, s4)


def _filter_positions(s):
    t = jnp.linspace(0.0, 1.0, s, dtype=F32)[:, None]
    bands = (HYENA_EMB_DIM - 1) // 2
    w = 2.0 * math.pi * jnp.arange(s, dtype=F32)[:, None] / s
    f = jnp.linspace(1e-4, bands - 1, bands, dtype=F32)[None, :]
    fw = f * w
    return jnp.concatenate([t, jnp.cos(fw), -jnp.sin(fw)], axis=-1)


def _hyena(z_hy, conv_w, conv_b, w1, b1, w2, b2, w3, b3, w4, freq):
    b, s, c3 = z_hy.shape
    c = c3 // 3
    n_hi = 2 * s // N_LO
    half = n_hi // 2
    assert b % 2 == 0 and n_hi * N_LO == 2 * s
    s1_pair, s1_real, gh, s4 = _dft_tables(n_hi)

    max_decay = math.log(HYENA_DECAY_TARGET) / HYENA_FAST_DECAY
    min_decay = math.log(HYENA_DECAY_TARGET) / HYENA_SLOW_DECAY
    deltas = jnp.linspace(min_decay, max_decay, c, dtype=F32)
    hf, hb = _filter_taps(_filter_positions(s), w1, b1, w2, b2, w3, b3, w4, freq, deltas)
    fa = _dft1_filter(hf.reshape(half, N_LO, c), hb.reshape(half, N_LO, c), s1_real, n_hi=n_hi)

    vv, x0 = _hy_pre(z_hy, conv_w, conv_b)
    a = _dft1_pair(vv.reshape(b, half, N_LO, c), s1_pair, n_hi=n_hi)
    cc = _spectral(a, fa, gh, n_hi=n_hi)
    y = _idft(cc, s4, n_hi=n_hi).reshape(b, s, c)
    return y, vv, x0


def kernel(x, rel_bias, ffn1_norm, ffn1_w_gate, ffn1_w_up, ffn1_w_down, mix_norm, w_in,
           lambda_q1, lambda_k1, lambda_q2, lambda_k2, diff_subln,
           hy_conv_w, hy_conv_b, hy_f_w1, hy_f_b1, hy_f_w2, hy_f_b2, hy_f_w3, hy_f_b3,
           hy_f_w4, hy_f_freq, hy_bias, hy_out_norm, w_out,
           ffn2_norm, ffn2_w_gate, ffn2_w_up, ffn2_w_down, final_norm):
    b, s, d = x.shape
    depth = w_in.shape[0]
    d_att = diff_subln.shape[1] * rel_bias.shape[1]
    n_heads = rel_bias.shape[1]
    attn_tile = min(512, s)
    bf = lambda a: a.astype(BF16)

    xf = x.reshape(b * s, d)
    for l in range(depth):
        last = l == depth - 1
        xf = _ffn(xf, ffn1_norm[l], bf(ffn1_w_gate[l]), bf(ffn1_w_up[l]), bf(ffn1_w_down[l]))
        qkv, z_hy = _mix(xf, mix_norm[l], bf(w_in[l]), d_att=d_att)
        lambda_init = 0.8 - 0.6 * math.exp(-0.3 * l)
        att = _attention(qkv.reshape(b, s, -1), _bias_tiles(rel_bias, attn_tile),
                         lambda_q1[l], lambda_k1[l], lambda_q2[l], lambda_k2[l], diff_subln[l],
                         n_heads=n_heads, lambda_init=lambda_init, t=attn_tile, n_chain=4, n_q=1)
        hy_parts = _hyena(z_hy.reshape(b, s, -1), hy_conv_w[l], hy_conv_b[l], hy_f_w1[l], hy_f_b1[l],
                          hy_f_w2[l], hy_f_b2[l], hy_f_w3[l], hy_f_b3[l], hy_f_w4[l], hy_f_freq[l])
        hy_parts = tuple(a.reshape(b * s, -1) for a in hy_parts) + (hy_bias[l], hy_out_norm[l])
        wo = bf(w_out[l])
        xf = _ffn(xf, ffn2_norm[l], bf(ffn2_w_gate[l]), bf(ffn2_w_up[l]), bf(ffn2_w_down[l]),
                  pre=(att.reshape(b * s, -1), wo[:d_att], wo[d_att:], hy_parts),
                  final_g=final_norm if last else None)
    if depth == 0:
        raise ValueError("depth must be positive")
    return xf.reshape(b, s, d)
```

```python
import functools
import math

import numpy as np
import jax
import jax.numpy as jnp
from jax import lax
from jax.experimental import pallas as pl
from jax.experimental.pallas import tpu as pltpu

F32 = jnp.float32
BF16 = jnp.bfloat16

RMS_EPS = 1e-6
LANES = 128
DIFF_HEAD_DIM = 64
V_HEAD_DIM = 2 * DIFF_HEAD_DIM
REL_BUCKETS = 32
REL_MAX_DIST = 128
N_HYENA_GROUPS = 8
HYENA_EMB_DIM = 33
HYENA_DECAY_TARGET = 1e-2
HYENA_FAST_DECAY = 0.3
HYENA_SLOW_DECAY = 1.5
N_LO = 128
LOG2_E = math.log2(math.e)
VMEM_LIMIT = 56 * 1024 * 1024


def _params(*sem):
    return pltpu.CompilerParams(dimension_semantics=sem, vmem_limit_bytes=VMEM_LIMIT)


def _resident(shape):
    return pl.BlockSpec(shape, lambda *_: (0,) * len(shape), pipeline_mode=pl.Buffered(1))


def _rms(x, g):
    return x * lax.rsqrt(jnp.mean(x * x, axis=-1, keepdims=True) + RMS_EPS) * g


def _dot(a, b):
    return jnp.dot(a, b, preferred_element_type=F32)


def _split(a):
    hi = a.astype(BF16)
    return hi, (a - hi.astype(F32)).astype(BF16)


def _hyena_gate_norm(y_ref, vv_ref, x0_ref, fb_ref, og_ref, mg_ref):
    vv = vv_ref[...].astype(F32)
    y = (y_ref[...] + vv * fb_ref[...]) * x0_ref[...].astype(F32)
    hi, lo = _split(y * y)
    ms = _dot(hi, mg_ref[...]) + _dot(lo, mg_ref[...])
    return y * lax.rsqrt(ms + RMS_EPS) * og_ref[...]


def _ffn_kernel(*refs, ff_chunk, n_chunk, pre, post):
    if pre:
        x_ref, att_ref, woa_ref, woh_ref = refs[:4]
        hy_refs = refs[4:10]
        refs = refs[10:]
    else:
        x_ref = refs[0]
        refs = refs[1:]
    g_ref, wg_ref, wu_ref, wd_ref = refs[:4]
    refs = refs[4:]
    if post:
        fg_ref, o_ref = refs
    else:
        (o_ref,) = refs

    x = x_ref[...]
    if pre:
        hy = _hyena_gate_norm(*hy_refs).astype(BF16)
        x = x + _dot(att_ref[...], woa_ref[...]) + _dot(hy, woh_ref[...])
    xn = _rms(x, g_ref[...]).astype(BF16)
    acc = jnp.zeros(x.shape, F32)
    for c in range(n_chunk):
        sl = slice(c * ff_chunk, (c + 1) * ff_chunk)
        gate = _dot(xn, wg_ref[:, sl])
        up = _dot(xn, wu_ref[:, sl])
        h = (jax.nn.silu(gate) * up).astype(BF16)
        acc = acc + _dot(h, wd_ref[sl, :])
    y = x + 0.5 * acc
    if post:
        y = _rms(y, fg_ref[...])
    o_ref[...] = y


def _ffn(x, norm_g, wg, wu, wd, *, pre=None, final_g=None, tm=512, ff_chunk=256):
    m, d = x.shape
    dff = wg.shape[1]
    n_chunk = dff // ff_chunk
    assert n_chunk * ff_chunk == dff and m % tm == 0
    row = lambda w: pl.BlockSpec((tm, w), lambda i: (i, 0))
    args, specs = [x], [row(d)]
    if pre is not None:
        att, woa, woh, (y, vv, x0, filt_bias, out_g) = pre
        c = y.shape[1]
        gdim = c // N_HYENA_GROUPS
        assert gdim & (gdim - 1) == 0
        grp = np.arange(c) // gdim
        mg = jnp.asarray((grp[:, None] == grp[None, :]).astype(np.float32) / gdim).astype(BF16)
        args += [att, woa, woh, y, vv, x0, filt_bias.reshape(1, c), out_g.reshape(1, c), mg]
        specs += [row(att.shape[1]), _resident(woa.shape), _resident(woh.shape), row(c), row(c), row(c),
                  _resident((1, c)), _resident((1, c)), _resident((c, c))]
    args += [norm_g.reshape(1, d), wg, wu, wd]
    specs += [_resident((1, d)), _resident(wg.shape), _resident(wu.shape), _resident(wd.shape)]
    if final_g is not None:
        args.append(final_g.reshape(1, d))
        specs.append(_resident((1, d)))
    kern = functools.partial(_ffn_kernel, ff_chunk=ff_chunk, n_chunk=n_chunk,
                             pre=pre is not None, post=final_g is not None)
    return pl.pallas_call(
        kern, grid=(m // tm,), in_specs=specs, out_specs=row(d),
        out_shape=jax.ShapeDtypeStruct((m, d), F32),
        compiler_params=_params("parallel"), name="ffn_pre" if pre is not None else "ffn",
    )(*args)


def _mix_kernel(x_ref, g_ref, w_ref, qkv_ref, z_ref, *, d_att, col_chunk, scale):
    xn = _rms(x_ref[...], g_ref[...]).astype(BF16)
    n_att = 3 * d_att
    for c0 in range(0, w_ref.shape[1], col_chunk):
        p = _dot(xn, w_ref[:, c0:c0 + col_chunk])
        if c0 < n_att:
            if c0 < d_att:
                p = p * scale
            qkv_ref[:, c0:c0 + col_chunk] = p.astype(BF16)
        else:
            z_ref[:, c0 - n_att:c0 - n_att + col_chunk] = p.astype(z_ref.dtype)


def _mix(x, norm_g, w_in, *, d_att, tm=512, col_chunk=512):
    m, d = x.shape
    d_in = w_in.shape[1]
    n_att = 3 * d_att
    assert d_att % col_chunk == 0 and d_in % col_chunk == 0 and m % tm == 0
    kern = functools.partial(_mix_kernel, d_att=d_att, col_chunk=col_chunk,
                             scale=DIFF_HEAD_DIM ** -0.5 * LOG2_E)
    return pl.pallas_call(
        kern, grid=(m // tm,),
        in_specs=[pl.BlockSpec((tm, d), lambda i: (i, 0)), _resident((1, d)), _resident(w_in.shape)],
        out_specs=[pl.BlockSpec((tm, n_att), lambda i: (i, 0)),
                   pl.BlockSpec((tm, d_in - n_att), lambda i: (i, 0))],
        out_shape=[jax.ShapeDtypeStruct((m, n_att), BF16),
                   jax.ShapeDtypeStruct((m, d_in - n_att), BF16)],
        compiler_params=_params("parallel"), name="mix",
    )(x, norm_g.reshape(1, d), w_in)


N_BIAS_TILES = 5


def _bias_kernel(tab_ref, o_ref, *, t):
    h = pl.program_id(0)
    d = pl.program_id(1) - N_BIAS_TILES // 2
    half = REL_BUCKETS // 2
    max_exact = half // 2
    rel = lax.broadcasted_iota(jnp.int32, (8, 2 * t), 1) + (d - 1) * t
    ret = jnp.where(rel > 0, half, 0)
    n = jnp.abs(rel)
    nf = jnp.maximum(n, 1).astype(F32)
    large = max_exact + (jnp.log(nf / max_exact) / math.log(REL_MAX_DIST / max_exact)
                         * (half - max_exact)).astype(jnp.int32)
    large = jnp.minimum(large, half - 1)
    bucket = ret + jnp.where(n < max_exact, n, large)
    row = jnp.zeros(rel.shape, F32)
    for b in range(REL_BUCKETS):
        row = jnp.where(bucket == b, tab_ref[b, h], row)
    rows = jnp.broadcast_to(row[0:1, :] * LOG2_E, (t, 2 * t))
    o_ref[...] = pltpu.roll(rows, 0, 1, stride=1, stride_axis=0)[:, t:2 * t]


def _bias_tiles(rel_bias, t):
    nh = rel_bias.shape[1]
    return pl.pallas_call(
        functools.partial(_bias_kernel, t=t), grid=(nh, N_BIAS_TILES),
        in_specs=[pl.BlockSpec(memory_space=pltpu.SMEM)],
        out_specs=pl.BlockSpec((None, None, t, t), lambda h, d: (h, d, 0, 0)),
        out_shape=jax.ShapeDtypeStruct((nh, N_BIAS_TILES, t, t), F32),
        compiler_params=_params("parallel", "parallel"), name="rel_bias_tiles",
    )(rel_bias)


def _attn_kernel(q_ref, k_ref, v_ref, bt_ref, lq1_ref, lk1_ref, lq2_ref, lk2_ref, sg_ref,
                 o_ref, q2_ref, vx_ref, *, t, nk, n_q, n_chain, lambda_init):
    g = pl.program_id(2)
    dv = v_ref.shape[1]

    @pl.when(g == 0)
    def _():
        vx_ref[:, 0:dv] = v_ref[...]
        ones_lane = lax.broadcasted_iota(jnp.int32, (v_ref.shape[0], dv), 1) == 0
        vx_ref[:, dv:2 * dv] = jnp.where(ones_lane, 1.0, 0.0).astype(vx_ref.dtype)

    far = N_BIAS_TILES // 2
    rc = 2 * t // n_chain
    lam = (jnp.exp(jnp.sum(lq1_ref[...] * lk1_ref[...])) - jnp.exp(jnp.sum(lq2_ref[...] * lk2_ref[...]))
           + lambda_init)

    def scores(i, base, d, rows, n_keys=t):
        wrapped = i + d >= nk
        j = jnp.where(wrapped, i + d - nk, i + d)
        k = k_ref[pl.ds(pl.multiple_of(j * t, t), n_keys), :]
        q2 = q2_ref[base + rows.start:base + rows.stop, :]
        s = lax.dot_general(q2, k, (((1,), (1,)), ((), ())), preferred_element_type=F32)
        cols = [s[:, c0:c0 + LANES] for c0 in range(0, n_keys, LANES)]
        if d <= 1 or d >= nk - 1:
            qr = slice(rows.start % t, rows.start % t + rows.stop - rows.start)
            bias = bt_ref[jnp.clip(j - i, -far, far) + far, qr, :]
            return j, [c + bias[:, n * LANES:(n + 1) * LANES] for n, c in enumerate(cols)], None
        side = jnp.where(wrapped, bt_ref[0, 0:1, 0:LANES], bt_ref[2 * far, 0:1, 0:LANES])
        return j, cols, side

    def tile(i, base, d, rows, m_prev, acc_prev):
        j, cols, side = scores(i, base, d, rows)
        shift = m_prev if side is None else m_prev - side
        rel = [(col - shift).astype(BF16) for col in cols]
        rm = jnp.max(functools.reduce(jnp.maximum, rel), axis=1, keepdims=True)
        delta = jnp.maximum(rm, 0.0)
        p = jnp.concatenate([jnp.exp2(x - delta) for x in rel], axis=1)
        d32 = jnp.broadcast_to(delta.astype(F32), m_prev.shape)
        alpha = jnp.exp2(-d32)
        pv = _dot(p, vx_ref[pl.ds(pl.multiple_of(j * t, t), t), :])
        acc_new = jnp.concatenate([alpha * acc_prev[:, 0:dv] + pv[:, 0:dv],
                                   alpha * acc_prev[:, dv:2 * dv] + pv[:, dv:2 * dv]], axis=1)
        return m_prev + d32, acc_new

    streams = []
    for a in range(n_q):
        i, base = g * n_q + a, a * 2 * t
        q = q_ref[a * t:(a + 1) * t, :]
        lane = lax.broadcasted_iota(jnp.int32, q.shape, 1)
        zero = jnp.zeros_like(q)
        q2_ref[base:base + t, :] = jnp.where(lane < DIFF_HEAD_DIM, q, zero)
        q2_ref[base + t:base + 2 * t, :] = jnp.where(lane >= DIFF_HEAD_DIM, q, zero)
        for r0 in range(0, 2 * t, rc):
            streams.append((i, base, slice(r0, r0 + rc)))
    state = []
    for i, base, rows in streams:
        m0 = jnp.max(scores(i, base, 0, rows, LANES)[1][0], axis=1, keepdims=True)
        state.append((jnp.broadcast_to(m0, (rc, LANES)), jnp.zeros((rc, 2 * dv), F32)))
    for d in range(nk):
        state = [tile(i, base, d, rows, *st) for (i, base, rows), st in zip(streams, state)]

    for a in range(n_q):
        acc = jnp.concatenate([acc for _, acc in state[a * n_chain:(a + 1) * n_chain]], axis=0)
        o = acc[:, 0:dv] / acc[:, dv:dv + 1]
        o = o[0:t] - lam * o[t:2 * t]
        o = _rms(o, sg_ref[...]) * (1.0 - lambda_init)
        o_ref[a * t:(a + 1) * t, :] = o.astype(o_ref.dtype)


def _attention(qkv, bias_tiles, lq1, lk1, lq2, lk2, subln, *, n_heads, lambda_init, t, n_chain, n_q):
    b, s, _ = qkv.shape
    nk = s // t
    assert nk * t == s and t % LANES == 0 and t + 1 >= 91 and (2 * t) % n_chain == 0 and nk % n_q == 0
    kern = functools.partial(_attn_kernel, t=t, nk=nk, n_q=n_q, n_chain=n_chain, lambda_init=lambda_init)
    vec = lambda n: pl.BlockSpec((1, n), lambda b_, h, i: (0, 0))
    return pl.pallas_call(
        kern, grid=(b, n_heads, nk // n_q),
        in_specs=[
            pl.BlockSpec((None, n_q * t, V_HEAD_DIM), lambda b_, h, i: (b_, i, h)),
            pl.BlockSpec((None, s, V_HEAD_DIM), lambda b_, h, i: (b_, 0, n_heads + h)),
            pl.BlockSpec((None, s, V_HEAD_DIM), lambda b_, h, i: (b_, 0, 2 * n_heads + h)),
            pl.BlockSpec((None, N_BIAS_TILES, t, t), lambda b_, h, i: (h, 0, 0, 0)),
            vec(DIFF_HEAD_DIM), vec(DIFF_HEAD_DIM), vec(DIFF_HEAD_DIM), vec(DIFF_HEAD_DIM),
            vec(V_HEAD_DIM),
        ],
        out_specs=pl.BlockSpec((None, n_q * t, V_HEAD_DIM), lambda b_, h, i: (b_, i, h)),
        out_shape=jax.ShapeDtypeStruct((b, s, n_heads * V_HEAD_DIM), BF16),
        scratch_shapes=[pltpu.VMEM((n_q * 2 * t, V_HEAD_DIM), BF16),
                        pltpu.VMEM((s, 2 * V_HEAD_DIM), BF16)],
        compiler_params=_params("parallel", "parallel", "arbitrary"), name="diff_attention",
    )(qkv, qkv, qkv, bias_tiles, lq1.reshape(1, -1), lk1.reshape(1, -1),
      lq2.reshape(1, -1), lk2.reshape(1, -1), subln.reshape(1, -1))


def _dft_tables(n_hi):
    n = n_hi * N_LO
    half = n_hi // 2
    hi = np.arange(n_hi)
    lo = np.arange(N_LO)
    f_hi = np.exp(-2j * np.pi * ((np.outer(hi, hi) % n_hi) / n_hi))
    fr, fi = f_hi.real, f_hi.imag
    s1_pair = np.block([[fr[:, :half], -fi[:, :half]], [fi[:, :half], fr[:, :half]]])
    s1_real = np.concatenate([fr[:, :half], fi[:, :half]], axis=0)
    ph = (hi[:, None, None] * lo[None, None, :] + n_hi * lo[None, :, None] * lo[None, None, :]) % n
    g = np.exp(-2j * np.pi * ph / n)
    g2 = np.concatenate([np.concatenate([g.real, -g.imag], axis=2),
                         np.concatenate([g.imag, g.real], axis=2)], axis=1)
    gh = np.stack([g2, np.swapaxes(g2, 1, 2)], axis=1)
    s4 = np.block([[fr[:half], fi[:half]], [-fi[:half], fr[:half]]]) / n
    cast = lambda a: jnp.asarray(a.astype(np.float32)).astype(BF16)
    return cast(s1_pair), cast(s1_real), cast(gh), cast(s4)


HALO_ROWS = 16


def _hy_pre_kernel(z_ref, zp_ref, zn_ref, w_ref, b_ref, vv_ref, x0_ref, *, c, n_blk):
    i = pl.program_id(1)
    z = z_ref[...].astype(F32)
    rows = z.shape[0]
    edge = HALO_ROWS
    z_m1 = pltpu.roll(z, 1, 0)
    z_p1 = pltpu.roll(z, rows - 1, 0)

    def emit(rs, zm, zc, zp):
        u = b_ref[...] + zm * w_ref[0:1, :] + zc * w_ref[1:2, :] + zp * w_ref[2:3, :]
        vv_ref[rs, :] = (u[:, 2 * c:3 * c] * u[:, c:2 * c]).astype(vv_ref.dtype)
        x0_ref[rs, :] = u[:, 0:c].astype(x0_ref.dtype)

    emit(slice(0, rows), z_m1, z, z_p1)
    r = lax.broadcasted_iota(jnp.int32, (edge, z.shape[1]), 0)
    prev_row = jnp.where(i > 0, zp_ref[HALO_ROWS - 1:HALO_ROWS, :].astype(F32), 0.0)
    next_row = jnp.where(i < n_blk - 1, zn_ref[0:1, :].astype(F32), 0.0)
    head, tail = slice(0, edge), slice(rows - edge, rows)
    emit(head, jnp.where(r == 0, prev_row, z_m1[head]), z[head], z_p1[head])
    emit(tail, z_m1[tail], z[tail], jnp.where(r == edge - 1, next_row, z_p1[tail]))


def _hy_pre(z, conv_w, conv_b, *, tt=512):
    b, s, c3 = z.shape
    c = c3 // 3
    n_blk = s // tt
    assert n_blk * tt == s and tt % HALO_ROWS == 0
    kern = functools.partial(_hy_pre_kernel, c=c, n_blk=n_blk)
    sub = tt // HALO_ROWS
    last = s // HALO_ROWS - 1
    row = pl.BlockSpec((None, tt, c), lambda b_, i: (b_, i, 0))
    return pl.pallas_call(
        kern, grid=(b, n_blk),
        in_specs=[
            pl.BlockSpec((None, tt, c3), lambda b_, i: (b_, i, 0)),
            pl.BlockSpec((None, HALO_ROWS, c3), lambda b_, i: (b_, jnp.maximum(i * sub - 1, 0), 0)),
            pl.BlockSpec((None, HALO_ROWS, c3), lambda b_, i: (b_, jnp.minimum((i + 1) * sub, last), 0)),
            pl.BlockSpec((3, c3), lambda b_, i: (0, 0)),
            pl.BlockSpec((1, c3), lambda b_, i: (0, 0)),
        ],
        out_specs=[row, row],
        out_shape=[jax.ShapeDtypeStruct((b, s, c), BF16), jax.ShapeDtypeStruct((b, s, c), BF16)],
        compiler_params=_params("parallel", "parallel"), name="hyena_pre",
    )(z, z, z, conv_w, conv_b.reshape(1, c3))


def _hdot(a, b, dims=(((1,), (0,)), ((), ()))):
    (ah, al), (bh, bl) = _split(a), _split(b)
    dg = functools.partial(lax.dot_general, dimension_numbers=dims, preferred_element_type=F32)
    return dg(ah, bh) + dg(ah, bl) + dg(al, bh)


def _filter_kernel(zt_ref, t_ref, w1_ref, b1_ref, w2_ref, b2_ref, w3_ref, b3_ref, w4_ref, fr_ref, dl_ref,
                   hf_ref, hb_ref, *, c):
    i = pl.program_id(0)
    fr = fr_ref[...]
    a = jnp.sin(fr * (_hdot(w1_ref[...], zt_ref[...]) + b1_ref[...]))
    a = jnp.sin(fr * (_hdot(w2_ref[...], a) + b2_ref[...]))
    a = jnp.sin(fr * (_hdot(w3_ref[...], a) + b3_ref[...]))
    hh = _hdot(a, w4_ref[...], (((0,), (0,)), ((), ())))
    decay = jnp.exp(-t_ref[...] * jnp.abs(dl_ref[...]))
    hf_ref[...] = (hh[:, 0:c] * decay).astype(hf_ref.dtype)
    r = lax.broadcasted_iota(jnp.int32, (hh.shape[0], c), 0)
    lag0 = jnp.logical_and(i == 0, r == 0)
    hb_ref[...] = jnp.where(lag0, 0.0, hh[:, c:2 * c] * decay).astype(hb_ref.dtype)


def _filter_taps(z, w1, b1, w2, b2, w3, b3, w4, freq, deltas, *, tb=512):
    s, emb = z.shape
    c = deltas.shape[0]
    assert s % tb == 0
    col = lambda v: v.reshape(-1, 1)
    args = [jnp.pad(z, ((0, 0), (0, LANES - emb))).T, z[:, 0:1],
            jnp.pad(w1, ((0, LANES - emb), (0, 0))).T, col(b1), w2.T, col(b2), w3.T, col(b3), w4,
            col(freq), deltas.reshape(1, c)]
    specs = [pl.BlockSpec((LANES, tb), lambda i: (0, i)), pl.BlockSpec((tb, 1), lambda i: (i, 0))] + [
        pl.BlockSpec(a.shape, lambda i: (0, 0)) for a in args[2:]]
    out = pl.BlockSpec((tb, c), lambda i: (i, 0))
    shp = jax.ShapeDtypeStruct((s, c), BF16)
    return pl.pallas_call(
        functools.partial(_filter_kernel, c=c), grid=(s // tb,),
        in_specs=specs, out_specs=[out, out], out_shape=[shp, shp],
        compiler_params=_params("parallel"), name="hyena_filter_taps",
    )(*args)


def _lo_major(ref):
    return jnp.swapaxes(ref[...], 0, 1)


def _dft1_kernel(x0_ref, x1_ref, f_ref, a_ref):
    n_out = a_ref.shape[0]
    x0, x1 = _lo_major(x0_ref), _lo_major(x1_ref)
    a = jnp.stack([_dot(f_ref[...], jnp.concatenate([x0[j], x1[j]], axis=0)).astype(a_ref.dtype)
                   for j in range(x0.shape[0])], axis=0)
    a = jnp.swapaxes(a, 0, 1)
    a_ref[:, 0] = a[0:n_out]
    a_ref[:, 1] = a[n_out:2 * n_out]


def _dft1_pair(xv, f_mat, *, n_hi, t_lo=16):
    b, half, _, c = xv.shape
    spec = lambda off: pl.BlockSpec((None, half, t_lo, c), lambda p, j: (2 * p + off, 0, j, 0))
    return pl.pallas_call(
        _dft1_kernel, grid=(b // 2, N_LO // t_lo),
        in_specs=[spec(0), spec(1), pl.BlockSpec(f_mat.shape, lambda p, j: (0, 0))],
        out_specs=pl.BlockSpec((None, n_hi, 2, t_lo, c), lambda p, j: (p, 0, 0, j, 0)),
        out_shape=jax.ShapeDtypeStruct((b // 2, n_hi, 2, N_LO, c), BF16),
        compiler_params=_params("parallel", "parallel"), name="hyena_dft1",
    )(xv, xv, f_mat)


def _dft1_filter_kernel(hf_ref, hb_ref, f_ref, fa_ref):
    n_out = fa_ref.shape[0]
    c = hf_ref.shape[2]
    hf, hb = _lo_major(hf_ref), _lo_major(hb_ref)
    a = jnp.stack([_dot(f_ref[...], jnp.concatenate([hf[j], hb[j]], axis=1)).astype(fa_ref.dtype)
                   for j in range(hf.shape[0])], axis=0)
    a = jnp.swapaxes(a, 0, 1)
    for part, (r0, c0) in enumerate(((0, 0), (n_out, 0), (0, c), (n_out, c))):
        fa_ref[:, part] = a[r0:r0 + n_out, :, c0:c0 + c]


def _dft1_filter(hf, hb, f_mat, *, n_hi, t_lo=16):
    half, _, c = hf.shape
    blk = pl.BlockSpec((half, t_lo, c), lambda j: (0, j, 0))
    return pl.pallas_call(
        _dft1_filter_kernel, grid=(N_LO // t_lo,),
        in_specs=[blk, blk, pl.BlockSpec(f_mat.shape, lambda j: (0, 0))],
        out_specs=pl.BlockSpec((n_hi, 4, t_lo, c), lambda j: (0, 0, j, 0)),
        out_shape=jax.ShapeDtypeStruct((n_hi, 4, N_LO, c), BF16),
        compiler_params=_params("parallel"), name="hyena_filter_dft1",
    )(hf, hb, f_mat)


def _spectral_kernel(a_ref, fa_ref, gh_ref, cc_ref):
    c = a_ref.shape[-1]
    for u in range(a_ref.shape[0]):
        a = jnp.concatenate([a_ref[u].reshape(2 * N_LO, c), fa_ref[u, 0:2].reshape(2 * N_LO, c),
                             fa_ref[u, 2:4].reshape(2 * N_LO, c)], axis=1)
        x = _dot(gh_ref[u, 0], a)
        xr, xi = x[0:N_LO, 0:c], x[N_LO:2 * N_LO, 0:c]
        kr = x[0:N_LO, c:2 * c] + x[0:N_LO, 2 * c:3 * c]
        ki = x[N_LO:2 * N_LO, c:2 * c] - x[N_LO:2 * N_LO, 2 * c:3 * c]
        y = jnp.concatenate([xr * kr - xi * ki, xr * ki + xi * kr], axis=0).astype(BF16)
        cc_ref[u] = _dot(gh_ref[u, 1], y).astype(cc_ref.dtype).reshape(2, N_LO, c)


def _spectral(a, fa, gh, *, n_hi, kb=4):
    p, _, _, _, c = a.shape
    assert n_hi % kb == 0
    blk = pl.BlockSpec((None, kb, 2, N_LO, c), lambda q, k: (q, k, 0, 0, 0))
    return pl.pallas_call(
        _spectral_kernel, grid=(p, n_hi // kb),
        in_specs=[blk, pl.BlockSpec((kb, 4, N_LO, c), lambda q, k: (k, 0, 0, 0)),
                  pl.BlockSpec((kb, 2, 2 * N_LO, 2 * N_LO), lambda q, k: (k, 0, 0, 0))],
        out_specs=blk, out_shape=jax.ShapeDtypeStruct(a.shape, BF16),
        compiler_params=_params("parallel", "parallel"), name="hyena_spectral",
    )(a, fa, gh)


def _idft_kernel(cc_ref, m_ref, o_ref):
    half = o_ref.shape[1]
    cr, ci = jnp.swapaxes(cc_ref[:, 0], 0, 1), jnp.swapaxes(cc_ref[:, 1], 0, 1)
    y = jnp.stack([_dot(m_ref[...], jnp.concatenate([cr[j], ci[j]], axis=0)) for j in range(cr.shape[0])],
                  axis=0)
    y = jnp.swapaxes(y, 0, 1)
    o_ref[0] = y[0:half]
    o_ref[1] = y[half:2 * half]


def _idft(cc, s4, *, n_hi, t_lo=16):
    p, _, _, _, c = cc.shape
    half = n_hi // 2
    return pl.pallas_call(
        _idft_kernel, grid=(p, N_LO // t_lo),
        in_specs=[pl.BlockSpec((None, n_hi, 2, t_lo, c), lambda q, j: (q, 0, 0, j, 0)),
                  pl.BlockSpec(s4.shape, lambda q, j: (0, 0))],
        out_specs=pl.BlockSpec((2, half, t_lo, c), lambda q, j: (q, 0, j, 0)),
        out_shape=jax.ShapeDtypeStruct((2 * p, half, N_LO, c), F32),
        compiler_params=_params("parallel", "parallel"), name="hyena_idft",
    )(cc, s4)


def _filter_positions(s):
    t = jnp.linspace(0.0, 1.0, s, dtype=F32)[:, None]
    bands = (HYENA_EMB_DIM - 1) // 2
    w = 2.0 * math.pi * jnp.arange(s, dtype=F32)[:, None] / s
    f = jnp.linspace(1e-4, bands - 1, bands, dtype=F32)[None, :]
    fw = f * w
    return jnp.concatenate([t, jnp.cos(fw), -jnp.sin(fw)], axis=-1)


def _hyena(z_hy, conv_w, conv_b, w1, b1, w2, b2, w3, b3, w4, freq):
    b, s, c3 = z_hy.shape
    c = c3 // 3
    n_hi = 2 * s // N_LO
    half = n_hi // 2
    assert b % 2 == 0 and n_hi * N_LO == 2 * s
    s1_pair, s1_real, gh, s4 = _dft_tables(n_hi)

    max_decay = math.log(HYENA_DECAY_TARGET) / HYENA_FAST_DECAY
    min_decay = math.log(HYENA_DECAY_TARGET) / HYENA_SLOW_DECAY
    deltas = jnp.linspace(min_decay, max_decay, c, dtype=F32)
    hf, hb = _filter_taps(_filter_positions(s), w1, b1, w2, b2, w3, b3, w4, freq, deltas)
    fa = _dft1_filter(hf.reshape(half, N_LO, c), hb.reshape(half, N_LO, c), s1_real, n_hi=n_hi)

    vv, x0 = _hy_pre(z_hy, conv_w, conv_b)
    a = _dft1_pair(vv.reshape(b, half, N_LO, c), s1_pair, n_hi=n_hi)
    cc = _spectral(a, fa, gh, n_hi=n_hi)
    y = _idft(cc, s4, n_hi=n_hi).reshape(b, s, c)
    return y, vv, x0


def kernel(x, rel_bias, ffn1_norm, ffn1_w_gate, ffn1_w_up, ffn1_w_down, mix_norm, w_in,
           lambda_q1, lambda_k1, lambda_q2, lambda_k2, diff_subln,
           hy_conv_w, hy_conv_b, hy_f_w1, hy_f_b1, hy_f_w2, hy_f_b2, hy_f_w3, hy_f_b3,
           hy_f_w4, hy_f_freq, hy_bias, hy_out_norm, w_out,
           ffn2_norm, ffn2_w_gate, ffn2_w_up, ffn2_w_down, final_norm):
    b, s, d = x.shape
    depth = w_in.shape[0]
    d_att = diff_subln.shape[1] * rel_bias.shape[1]
    n_heads = rel_bias.shape[1]
    attn_tile = min(512, s)
    bf = lambda a: a.astype(BF16)

    xf = x.reshape(b * s, d)
    for l in range(depth):
        last = l == depth - 1
        xf = _ffn(xf, ffn1_norm[l], bf(ffn1_w_gate[l]), bf(ffn1_w_up[l]), bf(ffn1_w_down[l]))
        qkv, z_hy = _mix(xf, mix_norm[l], bf(w_in[l]), d_att=d_att)
        lambda_init = 0.8 - 0.6 * math.exp(-0.3 * l)
        att = _attention(qkv.reshape(b, s, -1), _bias_tiles(rel_bias, attn_tile),
                         lambda_q1[l], lambda_k1[l], lambda_q2[l], lambda_k2[l], diff_subln[l],
                         n_heads=n_heads, lambda_init=lambda_init, t=attn_tile, n_chain=4,
                         n_q=2 if (s // attn_tile) % 2 == 0 else 1)
        hy_parts = _hyena(z_hy.reshape(b, s, -1), hy_conv_w[l], hy_conv_b[l], hy_f_w1[l], hy_f_b1[l],
                          hy_f_w2[l], hy_f_b2[l], hy_f_w3[l], hy_f_b3[l], hy_f_w4[l], hy_f_freq[l])
        hy_parts = tuple(a.reshape(b * s, -1) for a in hy_parts) + (hy_bias[l], hy_out_norm[l])
        wo = bf(w_out[l])
        xf = _ffn(xf, ffn2_norm[l], bf(ffn2_w_gate[l]), bf(ffn2_w_up[l]), bf(ffn2_w_down[l]),
                  pre=(att.reshape(b * s, -1), wo[:d_att], wo[d_att:], hy_parts),
                  final_g=final_norm if last else None)
    if depth == 0:
        raise ValueError("depth must be positive")
    return xf.reshape(b, s, d)
```

```python
import functools
import math

import numpy as np
import jax
import jax.numpy as jnp
from jax import lax
from jax.experimental import pallas as pl
from jax.experimental.pallas import tpu as pltpu

F32 = jnp.float32
BF16 = jnp.bfloat16

RMS_EPS = 1e-6
LANES = 128
DIFF_HEAD_DIM = 64
V_HEAD_DIM = 2 * DIFF_HEAD_DIM
REL_BUCKETS = 32
REL_MAX_DIST = 128
N_HYENA_GROUPS = 8
HYENA_EMB_DIM = 33
HYENA_DECAY_TARGET = 1e-2
HYENA_FAST_DECAY = 0.3
HYENA_SLOW_DECAY = 1.5
N_LO = 128
LOG2_E = math.log2(math.e)
VMEM_LIMIT = 56 * 1024 * 1024


def _params(*sem):
    return pltpu.CompilerParams(dimension_semantics=sem, vmem_limit_bytes=VMEM_LIMIT)


def _resident(shape):
    return pl.BlockSpec(shape, lambda *_: (0,) * len(shape), pipeline_mode=pl.Buffered(1))


def _rms(x, g):
    return x * lax.rsqrt(jnp.mean(x * x, axis=-1, keepdims=True) + RMS_EPS) * g


def _dot(a, b):
    return jnp.dot(a, b, preferred_element_type=F32)


def _split(a):
    hi = a.astype(BF16)
    return hi, (a - hi.astype(F32)).astype(BF16)


def _hyena_gate_norm(y_ref, vv_ref, x0_ref, fb_ref, og_ref, mg_ref):
    vv = vv_ref[...].astype(F32)
    y = (y_ref[...] + vv * fb_ref[...]) * x0_ref[...].astype(F32)
    hi, lo = _split(y * y)
    ms = _dot(hi, mg_ref[...]) + _dot(lo, mg_ref[...])
    return y * lax.rsqrt(ms + RMS_EPS) * og_ref[...]


def _ffn_kernel(*refs, ff_chunk, n_chunk, pre, post):
    if pre:
        x_ref, att_ref, woa_ref, woh_ref = refs[:4]
        hy_refs = refs[4:10]
        refs = refs[10:]
    else:
        x_ref = refs[0]
        refs = refs[1:]
    g_ref, wg_ref, wu_ref, wd_ref = refs[:4]
    refs = refs[4:]
    if post:
        fg_ref, o_ref = refs
    else:
        (o_ref,) = refs

    x = x_ref[...]
    if pre:
        hy = _hyena_gate_norm(*hy_refs).astype(BF16)
        x = x + _dot(att_ref[...], woa_ref[...]) + _dot(hy, woh_ref[...])
    xn = _rms(x, g_ref[...]).astype(BF16)
    acc = jnp.zeros(x.shape, F32)
    for c in range(n_chunk):
        sl = slice(c * ff_chunk, (c + 1) * ff_chunk)
        gate = _dot(xn, wg_ref[:, sl])
        up = _dot(xn, wu_ref[:, sl])
        h = (jax.nn.silu(gate) * up).astype(BF16)
        acc = acc + _dot(h, wd_ref[sl, :])
    y = x + 0.5 * acc
    if post:
        y = _rms(y, fg_ref[...])
    o_ref[...] = y


def _ffn(x, norm_g, wg, wu, wd, *, pre=None, final_g=None, tm=512, ff_chunk=256):
    m, d = x.shape
    dff = wg.shape[1]
    n_chunk = dff // ff_chunk
    assert n_chunk * ff_chunk == dff and m % tm == 0
    row = lambda w: pl.BlockSpec((tm, w), lambda i: (i, 0))
    args, specs = [x], [row(d)]
    if pre is not None:
        att, woa, woh, (y, vv, x0, filt_bias, out_g) = pre
        c = y.shape[1]
        gdim = c // N_HYENA_GROUPS
        assert gdim & (gdim - 1) == 0
        grp = np.arange(c) // gdim
        mg = jnp.asarray((grp[:, None] == grp[None, :]).astype(np.float32) / gdim).astype(BF16)
        args += [att, woa, woh, y, vv, x0, filt_bias.reshape(1, c), out_g.reshape(1, c), mg]
        specs += [row(att.shape[1]), _resident(woa.shape), _resident(woh.shape), row(c), row(c), row(c),
                  _resident((1, c)), _resident((1, c)), _resident((c, c))]
    args += [norm_g.reshape(1, d), wg, wu, wd]
    specs += [_resident((1, d)), _resident(wg.shape), _resident(wu.shape), _resident(wd.shape)]
    if final_g is not None:
        args.append(final_g.reshape(1, d))
        specs.append(_resident((1, d)))
    kern = functools.partial(_ffn_kernel, ff_chunk=ff_chunk, n_chunk=n_chunk,
                             pre=pre is not None, post=final_g is not None)
    return pl.pallas_call(
        kern, grid=(m // tm,), in_specs=specs, out_specs=row(d),
        out_shape=jax.ShapeDtypeStruct((m, d), F32),
        compiler_params=_params("parallel"), name="ffn_pre" if pre is not None else "ffn",
    )(*args)


HALO = 8


def _mix_kernel(x_ref, xp_ref, xn_ref, g_ref, w_ref, cw_ref, cb_ref, qkv_ref, vv_ref, x0_ref,
                *, d_att, c, scale, tiles_per_seq):
    i = pl.program_id(0)
    n_att = 3 * d_att
    xn = _rms(x_ref[...], g_ref[...]).astype(BF16)
    for c0 in range(0, n_att, d_att):
        p = _dot(xn, w_ref[:, c0:c0 + d_att])
        if c0 == 0:
            p = p * scale
        qkv_ref[:, c0:c0 + d_att] = p.astype(BF16)

    halo = _rms(jnp.concatenate([xp_ref[...], xn_ref[...]], axis=0), g_ref[...]).astype(BF16)
    first = i % tiles_per_seq == 0
    last = i % tiles_per_seq == tiles_per_seq - 1
    rows = x_ref.shape[0]
    r = lax.broadcasted_iota(jnp.int32, (HALO, c), 0)
    head, tail = slice(0, HALO), slice(rows - HALO, rows)
    u = []
    for part in range(3):
        cs = slice(n_att + part * c, n_att + (part + 1) * c)
        ws = slice(part * c, (part + 1) * c)
        z = _dot(xn, w_ref[:, cs])
        zh = _dot(halo, w_ref[:, cs])
        prev_row = jnp.where(first, 0.0, zh[HALO - 1:HALO])
        next_row = jnp.where(last, 0.0, zh[HALO:HALO + 1])
        z_m1 = pltpu.roll(z, 1, 0)
        z_p1 = pltpu.roll(z, rows - 1, 0)
        conv = lambda zm, zc, zp: (cb_ref[:, ws] + zm * cw_ref[0:1, ws] + zc * cw_ref[1:2, ws]
                                   + zp * cw_ref[2:3, ws])
        mid = conv(z_m1, z, z_p1)
        top = conv(jnp.where(r == 0, prev_row, z_m1[head]), z[head], z_p1[head])
        bot = conv(z_m1[tail], z[tail], jnp.where(r == HALO - 1, next_row, z_p1[tail]))
        u.append(jnp.concatenate([top, mid[HALO:rows - HALO], bot], axis=0))
    x0_ref[...] = u[0].astype(x0_ref.dtype)
    vv_ref[...] = (u[2] * u[1]).astype(vv_ref.dtype)


def _mix(x, norm_g, w_in, conv_w, conv_b, *, d_att, seq, tm=512):
    m, d = x.shape
    n_att = 3 * d_att
    c = (w_in.shape[1] - n_att) // 3
    assert m % tm == 0 and seq % tm == 0 and tm % HALO == 0
    kern = functools.partial(_mix_kernel, d_att=d_att, c=c, scale=DIFF_HEAD_DIM ** -0.5 * LOG2_E,
                             tiles_per_seq=seq // tm)
    sub = tm // HALO
    last_blk = m // HALO - 1
    row = lambda w: pl.BlockSpec((tm, w), lambda i: (i, 0))
    return pl.pallas_call(
        kern, grid=(m // tm,),
        in_specs=[row(d),
                  pl.BlockSpec((HALO, d), lambda i: (jnp.maximum(i * sub - 1, 0), 0)),
                  pl.BlockSpec((HALO, d), lambda i: (jnp.minimum((i + 1) * sub, last_blk), 0)),
                  _resident((1, d)), _resident(w_in.shape), _resident(conv_w.shape), _resident((1, 3 * c))],
        out_specs=[row(n_att), row(c), row(c)],
        out_shape=[jax.ShapeDtypeStruct((m, n_att), BF16), jax.ShapeDtypeStruct((m, c), BF16),
                   jax.ShapeDtypeStruct((m, c), BF16)],
        compiler_params=_params("parallel"), name="mix",
    )(x, x, x, norm_g.reshape(1, d), w_in, conv_w, conv_b.reshape(1, 3 * c))


N_BIAS_TILES = 5


def _bias_kernel(tab_ref, o_ref, *, t):
    h = pl.program_id(0)
    d = pl.program_id(1) - N_BIAS_TILES // 2
    half = REL_BUCKETS // 2
    max_exact = half // 2
    rel = lax.broadcasted_iota(jnp.int32, (8, 2 * t), 1) + (d - 1) * t
    ret = jnp.where(rel > 0, half, 0)
    n = jnp.abs(rel)
    nf = jnp.maximum(n, 1).astype(F32)
    large = max_exact + (jnp.log(nf / max_exact) / math.log(REL_MAX_DIST / max_exact)
                         * (half - max_exact)).astype(jnp.int32)
    large = jnp.minimum(large, half - 1)
    bucket = ret + jnp.where(n < max_exact, n, large)
    row = jnp.zeros(rel.shape, F32)
    for b in range(REL_BUCKETS):
        row = jnp.where(bucket == b, tab_ref[b, h], row)
    rows = jnp.broadcast_to(row[0:1, :] * LOG2_E, (t, 2 * t))
    o_ref[...] = pltpu.roll(rows, 0, 1, stride=1, stride_axis=0)[:, t:2 * t]


def _bias_tiles(rel_bias, t):
    nh = rel_bias.shape[1]
    return pl.pallas_call(
        functools.partial(_bias_kernel, t=t), grid=(nh, N_BIAS_TILES),
        in_specs=[pl.BlockSpec(memory_space=pltpu.SMEM)],
        out_specs=pl.BlockSpec((None, None, t, t), lambda h, d: (h, d, 0, 0)),
        out_shape=jax.ShapeDtypeStruct((nh, N_BIAS_TILES, t, t), F32),
        compiler_params=_params("parallel", "parallel"), name="rel_bias_tiles",
    )(rel_bias)


def _attn_kernel(q_ref, k_ref, v_ref, bt_ref, lq1_ref, lk1_ref, lq2_ref, lk2_ref, sg_ref,
                 o_ref, q2_ref, vx_ref, *, t, nk, n_q, n_chain, lambda_init):
    g = pl.program_id(2)
    dv = v_ref.shape[1]

    @pl.when(g == 0)
    def _():
        vx_ref[:, 0:dv] = v_ref[...]
        ones_lane = lax.broadcasted_iota(jnp.int32, (v_ref.shape[0], dv), 1) == 0
        vx_ref[:, dv:2 * dv] = jnp.where(ones_lane, 1.0, 0.0).astype(vx_ref.dtype)

    far = N_BIAS_TILES // 2
    rc = 2 * t // n_chain
    lam = (jnp.exp(jnp.sum(lq1_ref[...] * lk1_ref[...])) - jnp.exp(jnp.sum(lq2_ref[...] * lk2_ref[...]))
           + lambda_init)

    def scores(i, base, d, rows, n_keys=t):
        wrapped = i + d >= nk
        j = jnp.where(wrapped, i + d - nk, i + d)
        k = k_ref[pl.ds(pl.multiple_of(j * t, t), n_keys), :]
        q2 = q2_ref[base + rows.start:base + rows.stop, :]
        s = lax.dot_general(q2, k, (((1,), (1,)), ((), ())), preferred_element_type=F32)
        cols = [s[:, c0:c0 + LANES] for c0 in range(0, n_keys, LANES)]
        if d <= 1 or d >= nk - 1:
            qr = slice(rows.start % t, rows.start % t + rows.stop - rows.start)
            bias = bt_ref[jnp.clip(j - i, -far, far) + far, qr, :]
            return j, [c + bias[:, n * LANES:(n + 1) * LANES] for n, c in enumerate(cols)], None
        side = jnp.where(wrapped, bt_ref[0, 0:1, 0:LANES], bt_ref[2 * far, 0:1, 0:LANES])
        return j, cols, side

    def tile(i, base, d, rows, m_prev, acc_prev):
        j, cols, side = scores(i, base, d, rows)
        shift = m_prev if side is None else m_prev - side
        rel = [(col - shift).astype(BF16) for col in cols]
        rm = jnp.max(functools.reduce(jnp.maximum, rel), axis=1, keepdims=True)
        delta = jnp.maximum(rm, 0.0)
        p = jnp.concatenate([jnp.exp2(x - delta) for x in rel], axis=1)
        d32 = jnp.broadcast_to(delta.astype(F32), m_prev.shape)
        alpha = jnp.exp2(-d32)
        pv = _dot(p, vx_ref[pl.ds(pl.multiple_of(j * t, t), t), :])
        acc_new = jnp.concatenate([alpha * acc_prev[:, 0:dv] + pv[:, 0:dv],
                                   alpha * acc_prev[:, dv:2 * dv] + pv[:, dv:2 * dv]], axis=1)
        return m_prev + d32, acc_new

    streams = []
    for a in range(n_q):
        i, base = g * n_q + a, a * 2 * t
        q = q_ref[a * t:(a + 1) * t, :]
        lane = lax.broadcasted_iota(jnp.int32, q.shape, 1)
        zero = jnp.zeros_like(q)
        q2_ref[base:base + t, :] = jnp.where(lane < DIFF_HEAD_DIM, q, zero)
        q2_ref[base + t:base + 2 * t, :] = jnp.where(lane >= DIFF_HEAD_DIM, q, zero)
        for r0 in range(0, 2 * t, rc):
            streams.append((i, base, slice(r0, r0 + rc)))
    state = []
    for i, base, rows in streams:
        m0 = jnp.max(scores(i, base, 0, rows, LANES)[1][0], axis=1, keepdims=True)
        state.append((jnp.broadcast_to(m0, (rc, LANES)), jnp.zeros((rc, 2 * dv), F32)))
    for d in range(nk):
        state = [tile(i, base, d, rows, *st) for (i, base, rows), st in zip(streams, state)]

    for a in range(n_q):
        acc = jnp.concatenate([acc for _, acc in state[a * n_chain:(a + 1) * n_chain]], axis=0)
        o = acc[:, 0:dv] / acc[:, dv:dv + 1]
        o = o[0:t] - lam * o[t:2 * t]
        o = _rms(o, sg_ref[...]) * (1.0 - lambda_init)
        o_ref[a * t:(a + 1) * t, :] = o.astype(o_ref.dtype)


def _attention(qkv, bias_tiles, lq1, lk1, lq2, lk2, subln, *, n_heads, lambda_init, t, n_chain, n_q):
    b, s, _ = qkv.shape
    nk = s // t
    assert nk * t == s and t % LANES == 0 and t + 1 >= 91 and (2 * t) % n_chain == 0 and nk % n_q == 0
    kern = functools.partial(_attn_kernel, t=t, nk=nk, n_q=n_q, n_chain=n_chain, lambda_init=lambda_init)
    vec = lambda n: pl.BlockSpec((1, n), lambda b_, h, i: (0, 0))
    return pl.pallas_call(
        kern, grid=(b, n_heads, nk // n_q),
        in_specs=[
            pl.BlockSpec((None, n_q * t, V_HEAD_DIM), lambda b_, h, i: (b_, i, h)),
            pl.BlockSpec((None, s, V_HEAD_DIM), lambda b_, h, i: (b_, 0, n_heads + h)),
            pl.BlockSpec((None, s, V_HEAD_DIM), lambda b_, h, i: (b_, 0, 2 * n_heads + h)),
            pl.BlockSpec((None, N_BIAS_TILES, t, t), lambda b_, h, i: (h, 0, 0, 0)),
            vec(DIFF_HEAD_DIM), vec(DIFF_HEAD_DIM), vec(DIFF_HEAD_DIM), vec(DIFF_HEAD_DIM),
            vec(V_HEAD_DIM),
        ],
        out_specs=pl.BlockSpec((None, n_q * t, V_HEAD_DIM), lambda b_, h, i: (b_, i, h)),
        out_shape=jax.ShapeDtypeStruct((b, s, n_heads * V_HEAD_DIM), BF16),
        scratch_shapes=[pltpu.VMEM((n_q * 2 * t, V_HEAD_DIM), BF16),
                        pltpu.VMEM((s, 2 * V_HEAD_DIM), BF16)],
        compiler_params=_params("parallel", "parallel", "arbitrary"), name="diff_attention",
    )(qkv, qkv, qkv, bias_tiles, lq1.reshape(1, -1), lk1.reshape(1, -1),
      lq2.reshape(1, -1), lk2.reshape(1, -1), subln.reshape(1, -1))


def _dft_tables(n_hi):
    n = n_hi * N_LO
    half = n_hi // 2
    hi = np.arange(n_hi)
    lo = np.arange(N_LO)
    f_hi = np.exp(-2j * np.pi * ((np.outer(hi, hi) % n_hi) / n_hi))
    fr, fi = f_hi.real, f_hi.imag
    s1_pair = np.block([[fr[:, :half], -fi[:, :half]], [fi[:, :half], fr[:, :half]]])
    s1_real = np.concatenate([fr[:, :half], fi[:, :half]], axis=0)
    ph = (hi[:, None, None] * lo[None, None, :] + n_hi * lo[None, :, None] * lo[None, None, :]) % n
    g = np.exp(-2j * np.pi * ph / n)
    g2 = np.concatenate([np.concatenate([g.real, -g.imag], axis=2),
                         np.concatenate([g.imag, g.real], axis=2)], axis=1)
    gh = np.stack([g2, np.swapaxes(g2, 1, 2)], axis=1)
    s4 = np.block([[fr[:half], fi[:half]], [-fi[:half], fr[:half]]]) / n
    cast = lambda a: jnp.asarray(a.astype(np.float32)).astype(BF16)
    return cast(s1_pair), cast(s1_real), cast(gh), cast(s4)


def _hdot(a, b, dims=(((1,), (0,)), ((), ()))):
    (ah, al), (bh, bl) = _split(a), _split(b)
    dg = functools.partial(lax.dot_general, dimension_numbers=dims, preferred_element_type=F32)
    return dg(ah, bh) + dg(ah, bl) + dg(al, bh)


def _filter_kernel(zt_ref, t_ref, w1_ref, b1_ref, w2_ref, b2_ref, w3_ref, b3_ref, w4_ref, fr_ref, dl_ref,
                   hf_ref, hb_ref, *, c):
    i = pl.program_id(0)
    fr = fr_ref[...]
    a = jnp.sin(fr * (_hdot(w1_ref[...], zt_ref[...]) + b1_ref[...]))
    a = jnp.sin(fr * (_hdot(w2_ref[...], a) + b2_ref[...]))
    a = jnp.sin(fr * (_hdot(w3_ref[...], a) + b3_ref[...]))
    hh = _hdot(a, w4_ref[...], (((0,), (0,)), ((), ())))
    decay = jnp.exp(-t_ref[...] * jnp.abs(dl_ref[...]))
    hf_ref[...] = (hh[:, 0:c] * decay).astype(hf_ref.dtype)
    r = lax.broadcasted_iota(jnp.int32, (hh.shape[0], c), 0)
    lag0 = jnp.logical_and(i == 0, r == 0)
    hb_ref[...] = jnp.where(lag0, 0.0, hh[:, c:2 * c] * decay).astype(hb_ref.dtype)


def _filter_taps(z, w1, b1, w2, b2, w3, b3, w4, freq, deltas, *, tb=512):
    s, emb = z.shape
    c = deltas.shape[0]
    assert s % tb == 0
    col = lambda v: v.reshape(-1, 1)
    args = [jnp.pad(z, ((0, 0), (0, LANES - emb))).T, z[:, 0:1],
            jnp.pad(w1, ((0, LANES - emb), (0, 0))).T, col(b1), w2.T, col(b2), w3.T, col(b3), w4,
            col(freq), deltas.reshape(1, c)]
    specs = [pl.BlockSpec((LANES, tb), lambda i: (0, i)), pl.BlockSpec((tb, 1), lambda i: (i, 0))] + [
        pl.BlockSpec(a.shape, lambda i: (0, 0)) for a in args[2:]]
    out = pl.BlockSpec((tb, c), lambda i: (i, 0))
    shp = jax.ShapeDtypeStruct((s, c), BF16)
    return pl.pallas_call(
        functools.partial(_filter_kernel, c=c), grid=(s // tb,),
        in_specs=specs, out_specs=[out, out], out_shape=[shp, shp],
        compiler_params=_params("parallel"), name="hyena_filter_taps",
    )(*args)


def _lo_major(ref):
    return jnp.swapaxes(ref[...], 0, 1)


def _dft1_kernel(x0_ref, x1_ref, f_ref, a_ref):
    n_out = a_ref.shape[0]
    x0, x1 = _lo_major(x0_ref), _lo_major(x1_ref)
    a = jnp.stack([_dot(f_ref[...], jnp.concatenate([x0[j], x1[j]], axis=0)).astype(a_ref.dtype)
                   for j in range(x0.shape[0])], axis=0)
    a = jnp.swapaxes(a, 0, 1)
    a_ref[:, 0] = a[0:n_out]
    a_ref[:, 1] = a[n_out:2 * n_out]


def _dft1_pair(xv, f_mat, *, n_hi, t_lo=16):
    b, half, _, c = xv.shape
    spec = lambda off: pl.BlockSpec((None, half, t_lo, c), lambda p, j: (2 * p + off, 0, j, 0))
    return pl.pallas_call(
        _dft1_kernel, grid=(b // 2, N_LO // t_lo),
        in_specs=[spec(0), spec(1), pl.BlockSpec(f_mat.shape, lambda p, j: (0, 0))],
        out_specs=pl.BlockSpec((None, n_hi, 2, t_lo, c), lambda p, j: (p, 0, 0, j, 0)),
        out_shape=jax.ShapeDtypeStruct((b // 2, n_hi, 2, N_LO, c), BF16),
        compiler_params=_params("parallel", "parallel"), name="hyena_dft1",
    )(xv, xv, f_mat)


def _dft1_filter_kernel(hf_ref, hb_ref, f_ref, fa_ref):
    n_out = fa_ref.shape[0]
    c = hf_ref.shape[2]
    hf, hb = _lo_major(hf_ref), _lo_major(hb_ref)
    a = jnp.stack([_dot(f_ref[...], jnp.concatenate([hf[j], hb[j]], axis=1)).astype(fa_ref.dtype)
                   for j in range(hf.shape[0])], axis=0)
    a = jnp.swapaxes(a, 0, 1)
    for part, (r0, c0) in enumerate(((0, 0), (n_out, 0), (0, c), (n_out, c))):
        fa_ref[:, part] = a[r0:r0 + n_out, :, c0:c0 + c]


def _dft1_filter(hf, hb, f_mat, *, n_hi, t_lo=16):
    half, _, c = hf.shape
    blk = pl.BlockSpec((half, t_lo, c), lambda j: (0, j, 0))
    return pl.pallas_call(
        _dft1_filter_kernel, grid=(N_LO // t_lo,),
        in_specs=[blk, blk, pl.BlockSpec(f_mat.shape, lambda j: (0, 0))],
        out_specs=pl.BlockSpec((n_hi, 4, t_lo, c), lambda j: (0, 0, j, 0)),
        out_shape=jax.ShapeDtypeStruct((n_hi, 4, N_LO, c), BF16),
        compiler_params=_params("parallel"), name="hyena_filter_dft1",
    )(hf, hb, f_mat)


def _spectral_kernel(a_ref, fa_ref, gh_ref, cc_ref):
    c = a_ref.shape[-1]
    for u in range(a_ref.shape[0]):
        a = jnp.concatenate([a_ref[u].reshape(2 * N_LO, c), fa_ref[u, 0:2].reshape(2 * N_LO, c),
                             fa_ref[u, 2:4].reshape(2 * N_LO, c)], axis=1)
        x = _dot(gh_ref[u, 0], a)
        xr, xi = x[0:N_LO, 0:c], x[N_LO:2 * N_LO, 0:c]
        kr = x[0:N_LO, c:2 * c] + x[0:N_LO, 2 * c:3 * c]
        ki = x[N_LO:2 * N_LO, c:2 * c] - x[N_LO:2 * N_LO, 2 * c:3 * c]
        y = jnp.concatenate([xr * kr - xi * ki, xr * ki + xi * kr], axis=0).astype(BF16)
        cc_ref[u] = _dot(gh_ref[u, 1], y).astype(cc_ref.dtype).reshape(2, N_LO, c)


def _spectral(a, fa, gh, *, n_hi, kb=4):
    p, _, _, _, c = a.shape
    assert n_hi % kb == 0
    blk = pl.BlockSpec((None, kb, 2, N_LO, c), lambda q, k: (q, k, 0, 0, 0))
    return pl.pallas_call(
        _spectral_kernel, grid=(p, n_hi // kb),
        in_specs=[blk, pl.BlockSpec((kb, 4, N_LO, c), lambda q, k: (k, 0, 0, 0)),
                  pl.BlockSpec((kb, 2, 2 * N_LO, 2 * N_LO), lambda q, k: (k, 0, 0, 0))],
        out_specs=blk, out_shape=jax.ShapeDtypeStruct(a.shape, BF16),
        compiler_params=_params("parallel", "parallel"), name="hyena_spectral",
    )(a, fa, gh)


def _idft_kernel(cc_ref, m_ref, o_ref):
    half = o_ref.shape[1]
    cr, ci = jnp.swapaxes(cc_ref[:, 0], 0, 1), jnp.swapaxes(cc_ref[:, 1], 0, 1)
    y = jnp.stack([_dot(m_ref[...], jnp.concatenate([cr[j], ci[j]], axis=0)) for j in range(cr.shape[0])],
                  axis=0)
    y = jnp.swapaxes(y, 0, 1)
    o_ref[0] = y[0:half]
    o_ref[1] = y[half:2 * half]


def _idft(cc, s4, *, n_hi, t_lo=16):
    p, _, _, _, c = cc.shape
    half = n_hi // 2
    return pl.pallas_call(
        _idft_kernel, grid=(p, N_LO // t_lo),
        in_specs=[pl.BlockSpec((None, n_hi, 2, t_lo, c), lambda q, j: (q, 0, 0, j, 0)),
                  pl.BlockSpec(s4.shape, lambda q, j: (0, 0))],
        out_specs=pl.BlockSpec((2, half, t_lo, c), lambda q, j: (q, 0, j, 0)),
        out_shape=jax.ShapeDtypeStruct((2 * p, half, N_LO, c), F32),
        compiler_params=_params("parallel", "parallel"), name="hyena_idft",
    )(cc, s4)


def _filter_positions(s):
    t = jnp.linspace(0.0, 1.0, s, dtype=F32)[:, None]
    bands = (HYENA_EMB_DIM - 1) // 2
    w = 2.0 * math.pi * jnp.arange(s, dtype=F32)[:, None] / s
    f = jnp.linspace(1e-4, bands - 1, bands, dtype=F32)[None, :]
    fw = f * w
    return jnp.concatenate([t, jnp.cos(fw), -jnp.sin(fw)], axis=-1)


def _hyena_conv(vv, w1, b1, w2, b2, w3, b3, w4, freq):
    b, s, c = vv.shape
    n_hi = 2 * s // N_LO
    half = n_hi // 2
    assert b % 2 == 0 and n_hi * N_LO == 2 * s
    s1_pair, s1_real, gh, s4 = _dft_tables(n_hi)

    max_decay = math.log(HYENA_DECAY_TARGET) / HYENA_FAST_DECAY
    min_decay = math.log(HYENA_DECAY_TARGET) / HYENA_SLOW_DECAY
    deltas = jnp.linspace(min_decay, max_decay, c, dtype=F32)
    hf, hb = _filter_taps(_filter_positions(s), w1, b1, w2, b2, w3, b3, w4, freq, deltas)
    fa = _dft1_filter(hf.reshape(half, N_LO, c), hb.reshape(half, N_LO, c), s1_real, n_hi=n_hi)

    a = _dft1_pair(vv.reshape(b, half, N_LO, c), s1_pair, n_hi=n_hi)
    cc = _spectral(a, fa, gh, n_hi=n_hi)
    return _idft(cc, s4, n_hi=n_hi).reshape(b, s, c)


def kernel(x, rel_bias, ffn1_norm, ffn1_w_gate, ffn1_w_up, ffn1_w_down, mix_norm, w_in,
           lambda_q1, lambda_k1, lambda_q2, lambda_k2, diff_subln,
           hy_conv_w, hy_conv_b, hy_f_w1, hy_f_b1, hy_f_w2, hy_f_b2, hy_f_w3, hy_f_b3,
           hy_f_w4, hy_f_freq, hy_bias, hy_out_norm, w_out,
           ffn2_norm, ffn2_w_gate, ffn2_w_up, ffn2_w_down, final_norm):
    b, s, d = x.shape
    depth = w_in.shape[0]
    d_att = diff_subln.shape[1] * rel_bias.shape[1]
    n_heads = rel_bias.shape[1]
    attn_tile = min(512, s)
    bf = lambda a: a.astype(BF16)

    xf = x.reshape(b * s, d)
    for l in range(depth):
        last = l == depth - 1
        xf = _ffn(xf, ffn1_norm[l], bf(ffn1_w_gate[l]), bf(ffn1_w_up[l]), bf(ffn1_w_down[l]))
        qkv, vv, x0 = _mix(xf, mix_norm[l], bf(w_in[l]), hy_conv_w[l], hy_conv_b[l], d_att=d_att, seq=s)
        lambda_init = 0.8 - 0.6 * math.exp(-0.3 * l)
        att = _attention(qkv.reshape(b, s, -1), _bias_tiles(rel_bias, attn_tile),
                         lambda_q1[l], lambda_k1[l], lambda_q2[l], lambda_k2[l], diff_subln[l],
                         n_heads=n_heads, lambda_init=lambda_init, t=attn_tile, n_chain=4,
                         n_q=2 if (s // attn_tile) % 2 == 0 else 1)
        y = _hyena_conv(vv.reshape(b, s, -1), hy_f_w1[l], hy_f_b1[l], hy_f_w2[l], hy_f_b2[l],
                        hy_f_w3[l], hy_f_b3[l], hy_f_w4[l], hy_f_freq[l])
        hy_parts = (y.reshape(b * s, -1), vv, x0, hy_bias[l], hy_out_norm[l])
        wo = bf(w_out[l])
        xf = _ffn(xf, ffn2_norm[l], bf(ffn2_w_gate[l]), bf(ffn2_w_up[l]), bf(ffn2_w_down[l]),
                  pre=(att.reshape(b * s, -1), wo[:d_att], wo[d_att:], hy_parts),
                  final_g=final_norm if last else None)
    if depth == 0:
        raise ValueError("depth must be positive")
    return xf.reshape(b, s, d)
```

```python
import functools
import math

import numpy as np
import jax
import jax.numpy as jnp
from jax import lax
from jax.experimental import pallas as pl
from jax.experimental.pallas import tpu as pltpu

F32 = jnp.float32
BF16 = jnp.bfloat16

RMS_EPS = 1e-6
LANES = 128
DIFF_HEAD_DIM = 64
V_HEAD_DIM = 2 * DIFF_HEAD_DIM
REL_BUCKETS = 32
REL_MAX_DIST = 128
N_HYENA_GROUPS = 8
HYENA_EMB_DIM = 33
HYENA_DECAY_TARGET = 1e-2
HYENA_FAST_DECAY = 0.3
HYENA_SLOW_DECAY = 1.5
N_LO = 128
LOG2_E = math.log2(math.e)
VMEM_LIMIT = 56 * 1024 * 1024


def _params(*sem):
    return pltpu.CompilerParams(dimension_semantics=sem, vmem_limit_bytes=VMEM_LIMIT)


def _resident(shape):
    return pl.BlockSpec(shape, lambda *_: (0,) * len(shape), pipeline_mode=pl.Buffered(1))


def _rms(x, g):
    return x * lax.rsqrt(jnp.mean(x * x, axis=-1, keepdims=True) + RMS_EPS) * g


def _dot(a, b):
    return jnp.dot(a, b, preferred_element_type=F32)


def _split(a):
    hi = a.astype(BF16)
    return hi, (a - hi.astype(F32)).astype(BF16)


def _hyena_gate_norm(y_ref, vv_ref, x0_ref, fb_ref, og_ref, mg_ref):
    vv = vv_ref[...].astype(F32)
    y = (y_ref[...] + vv * fb_ref[...]) * x0_ref[...].astype(F32)
    hi, lo = _split(y * y)
    ms = _dot(hi, mg_ref[...]) + _dot(lo, mg_ref[...])
    return y * lax.rsqrt(ms + RMS_EPS) * og_ref[...]


N_STAGE = 8


def _stage_weights(srcs, dsts, stages, sem):
    jobs = [(src, dst, stage, c) for src, dst, stage in zip(srcs, dsts, stages) for c in range(N_STAGE)]

    def copy(k):
        src, _, stage, c = jobs[k]
        rows = stage.shape[1]
        return pltpu.make_async_copy(src.at[pl.ds(c * rows, rows), :], stage.at[k % 2], sem.at[k % 2])

    copy(0).start()
    for k, (_, dst, stage, c) in enumerate(jobs):
        if k + 1 < len(jobs):
            copy(k + 1).start()
        copy(k).wait()
        rows = stage.shape[1]
        dst[pl.ds(c * rows, rows), :] = stage[k % 2].astype(dst.dtype)


def _ffn_kernel(*refs, ff_chunk, n_chunk, pre, post):
    if pre:
        x_ref, att_ref, woa_ref, woh_ref = refs[:4]
        hy_refs = refs[4:10]
        refs = refs[10:]
    else:
        x_ref = refs[0]
        refs = refs[1:]
    g_ref, wg_hbm, wu_hbm, wd_hbm = refs[:4]
    refs = refs[4:]
    if post:
        fg_ref, o_ref = refs[:2]
        refs = refs[2:]
    else:
        o_ref = refs[0]
        refs = refs[1:]
    wg_ref, wu_ref, wd_ref, stage_in, stage_out, sem = refs

    @pl.when(pl.program_id(0) == 0)
    def _():
        _stage_weights((wg_hbm, wu_hbm, wd_hbm), (wg_ref, wu_ref, wd_ref),
                       (stage_in, stage_in, stage_out), sem)

    x = x_ref[...]
    if pre:
        hy = _hyena_gate_norm(*hy_refs).astype(BF16)
        x = x + _dot(att_ref[...], woa_ref[...]) + _dot(hy, woh_ref[...])
    xn = _rms(x, g_ref[...]).astype(BF16)
    acc = jnp.zeros(x.shape, F32)
    for c in range(n_chunk):
        sl = slice(c * ff_chunk, (c + 1) * ff_chunk)
        gate = _dot(xn, wg_ref[:, sl])
        up = _dot(xn, wu_ref[:, sl])
        h = (jax.nn.silu(gate) * up).astype(BF16)
        acc = acc + _dot(h, wd_ref[sl, :])
    y = x + 0.5 * acc
    if post:
        y = _rms(y, fg_ref[...])
    o_ref[...] = y


def _ffn(x, norm_g, wg, wu, wd, *, pre=None, final_g=None, tm=512, ff_chunk=256):
    m, d = x.shape
    dff = wg.shape[1]
    n_chunk = dff // ff_chunk
    assert n_chunk * ff_chunk == dff and m % tm == 0
    assert d % (16 * N_STAGE) == 0 and dff % (16 * N_STAGE) == 0
    row = lambda w: pl.BlockSpec((tm, w), lambda i: (i, 0))
    args, specs = [x], [row(d)]
    if pre is not None:
        att, woa, woh, (y, vv, x0, filt_bias, out_g) = pre
        c = y.shape[1]
        gdim = c // N_HYENA_GROUPS
        assert gdim & (gdim - 1) == 0
        grp = np.arange(c) // gdim
        mg = jnp.asarray((grp[:, None] == grp[None, :]).astype(np.float32) / gdim).astype(BF16)
        args += [att, woa, woh, y, vv, x0, filt_bias.reshape(1, c), out_g.reshape(1, c), mg]
        specs += [row(att.shape[1]), _resident(woa.shape), _resident(woh.shape), row(c), row(c), row(c),
                  _resident((1, c)), _resident((1, c)), _resident((c, c))]
    hbm = pl.BlockSpec(memory_space=pl.ANY)
    args += [norm_g.reshape(1, d), wg, wu, wd]
    specs += [_resident((1, d)), hbm, hbm, hbm]
    if final_g is not None:
        args.append(final_g.reshape(1, d))
        specs.append(_resident((1, d)))
    kern = functools.partial(_ffn_kernel, ff_chunk=ff_chunk, n_chunk=n_chunk,
                             pre=pre is not None, post=final_g is not None)
    return pl.pallas_call(
        kern, grid=(m // tm,), in_specs=specs, out_specs=row(d),
        out_shape=jax.ShapeDtypeStruct((m, d), F32),
        scratch_shapes=[pltpu.VMEM((d, dff), BF16), pltpu.VMEM((d, dff), BF16), pltpu.VMEM((dff, d), BF16),
                        pltpu.VMEM((2, d // N_STAGE, dff), F32), pltpu.VMEM((2, dff // N_STAGE, d), F32),
                        pltpu.SemaphoreType.DMA((2,))],
        compiler_params=_params("arbitrary"), name="ffn_pre" if pre is not None else "ffn",
    )(*args)


HALO = 8


def _mix_kernel(x_ref, xp_ref, xn_ref, g_ref, w_ref, cw_ref, cb_ref, qkv_ref, vv_ref, x0_ref,
                *, d_att, c, scale, tiles_per_seq):
    i = pl.program_id(0)
    n_att = 3 * d_att
    xn = _rms(x_ref[...], g_ref[...]).astype(BF16)
    for c0 in range(0, n_att, d_att):
        p = _dot(xn, w_ref[:, c0:c0 + d_att])
        if c0 == 0:
            p = p * scale
        qkv_ref[:, c0:c0 + d_att] = p.astype(BF16)

    halo = _rms(jnp.concatenate([xp_ref[...], xn_ref[...]], axis=0), g_ref[...]).astype(BF16)
    first = i % tiles_per_seq == 0
    last = i % tiles_per_seq == tiles_per_seq - 1
    rows = x_ref.shape[0]
    r = lax.broadcasted_iota(jnp.int32, (HALO, c), 0)
    head, tail = slice(0, HALO), slice(rows - HALO, rows)
    u = []
    for part in range(3):
        cs = slice(n_att + part * c, n_att + (part + 1) * c)
        ws = slice(part * c, (part + 1) * c)
        z = _dot(xn, w_ref[:, cs])
        zh = _dot(halo, w_ref[:, cs])
        prev_row = jnp.where(first, 0.0, zh[HALO - 1:HALO])
        next_row = jnp.where(last, 0.0, zh[HALO:HALO + 1])
        z_m1 = pltpu.roll(z, 1, 0)
        z_p1 = pltpu.roll(z, rows - 1, 0)
        conv = lambda zm, zc, zp: (cb_ref[:, ws] + zm * cw_ref[0:1, ws] + zc * cw_ref[1:2, ws]
                                   + zp * cw_ref[2:3, ws])
        mid = conv(z_m1, z, z_p1)
        top = conv(jnp.where(r == 0, prev_row, z_m1[head]), z[head], z_p1[head])
        bot = conv(z_m1[tail], z[tail], jnp.where(r == HALO - 1, next_row, z_p1[tail]))
        u.append(jnp.concatenate([top, mid[HALO:rows - HALO], bot], axis=0))
    x0_ref[...] = u[0].astype(x0_ref.dtype)
    vv_ref[...] = (u[2] * u[1]).astype(vv_ref.dtype)


def _mix(x, norm_g, w_in, conv_w, conv_b, *, d_att, seq, tm=512):
    m, d = x.shape
    n_att = 3 * d_att
    c = (w_in.shape[1] - n_att) // 3
    assert m % tm == 0 and seq % tm == 0 and tm % HALO == 0
    kern = functools.partial(_mix_kernel, d_att=d_att, c=c, scale=DIFF_HEAD_DIM ** -0.5 * LOG2_E,
                             tiles_per_seq=seq // tm)
    sub = tm // HALO
    last_blk = m // HALO - 1
    row = lambda w: pl.BlockSpec((tm, w), lambda i: (i, 0))
    return pl.pallas_call(
        kern, grid=(m // tm,),
        in_specs=[row(d),
                  pl.BlockSpec((HALO, d), lambda i: (jnp.maximum(i * sub - 1, 0), 0)),
                  pl.BlockSpec((HALO, d), lambda i: (jnp.minimum((i + 1) * sub, last_blk), 0)),
                  _resident((1, d)), _resident(w_in.shape), _resident(conv_w.shape), _resident((1, 3 * c))],
        out_specs=[row(n_att), row(c), row(c)],
        out_shape=[jax.ShapeDtypeStruct((m, n_att), BF16), jax.ShapeDtypeStruct((m, c), BF16),
                   jax.ShapeDtypeStruct((m, c), BF16)],
        compiler_params=_params("parallel"), name="mix",
    )(x, x, x, norm_g.reshape(1, d), w_in, conv_w, conv_b.reshape(1, 3 * c))


N_BIAS_TILES = 5


def _bias_kernel(tab_ref, o_ref, *, t):
    h = pl.program_id(0)
    d = pl.program_id(1) - N_BIAS_TILES // 2
    half = REL_BUCKETS // 2
    max_exact = half // 2
    rel = lax.broadcasted_iota(jnp.int32, (8, 2 * t), 1) + (d - 1) * t
    ret = jnp.where(rel > 0, half, 0)
    n = jnp.abs(rel)
    nf = jnp.maximum(n, 1).astype(F32)
    large = max_exact + (jnp.log(nf / max_exact) / math.log(REL_MAX_DIST / max_exact)
                         * (half - max_exact)).astype(jnp.int32)
    large = jnp.minimum(large, half - 1)
    bucket = ret + jnp.where(n < max_exact, n, large)
    row = jnp.zeros(rel.shape, F32)
    for b in range(REL_BUCKETS):
        row = jnp.where(bucket == b, tab_ref[b, h], row)
    rows = jnp.broadcast_to(row[0:1, :] * LOG2_E, (t, 2 * t))
    o_ref[...] = pltpu.roll(rows, 0, 1, stride=1, stride_axis=0)[:, t:2 * t]


def _bias_tiles(rel_bias, t):
    nh = rel_bias.shape[1]
    return pl.pallas_call(
        functools.partial(_bias_kernel, t=t), grid=(nh, N_BIAS_TILES),
        in_specs=[pl.BlockSpec(memory_space=pltpu.SMEM)],
        out_specs=pl.BlockSpec((None, None, t, t), lambda h, d: (h, d, 0, 0)),
        out_shape=jax.ShapeDtypeStruct((nh, N_BIAS_TILES, t, t), F32),
        compiler_params=_params("parallel", "parallel"), name="rel_bias_tiles",
    )(rel_bias)


def _attn_kernel(q_ref, k_ref, v_ref, bt_ref, lq1_ref, lk1_ref, lq2_ref, lk2_ref, sg_ref,
                 o_ref, q2_ref, vx_ref, *, t, nk, n_q, n_chain, lambda_init):
    g = pl.program_id(2)
    dv = v_ref.shape[1]

    @pl.when(g == 0)
    def _():
        vx_ref[:, 0:dv] = v_ref[...]
        ones_lane = lax.broadcasted_iota(jnp.int32, (v_ref.shape[0], dv), 1) == 0
        vx_ref[:, dv:2 * dv] = jnp.where(ones_lane, 1.0, 0.0).astype(vx_ref.dtype)

    far = N_BIAS_TILES // 2
    rc = 2 * t // n_chain
    lam = (jnp.exp(jnp.sum(lq1_ref[...] * lk1_ref[...])) - jnp.exp(jnp.sum(lq2_ref[...] * lk2_ref[...]))
           + lambda_init)

    def scores(i, base, d, rows, n_keys=t):
        wrapped = i + d >= nk
        j = jnp.where(wrapped, i + d - nk, i + d)
        k = k_ref[pl.ds(pl.multiple_of(j * t, t), n_keys), :]
        q2 = q2_ref[base + rows.start:base + rows.stop, :]
        s = lax.dot_general(q2, k, (((1,), (1,)), ((), ())), preferred_element_type=F32)
        cols = [s[:, c0:c0 + LANES] for c0 in range(0, n_keys, LANES)]
        if d <= 1 or d >= nk - 1:
            qr = slice(rows.start % t, rows.start % t + rows.stop - rows.start)
            bias = bt_ref[jnp.clip(j - i, -far, far) + far, qr, :]
            return j, [c + bias[:, n * LANES:(n + 1) * LANES] for n, c in enumerate(cols)], None
        side = jnp.where(wrapped, bt_ref[0, 0:1, 0:LANES], bt_ref[2 * far, 0:1, 0:LANES])
        return j, cols, side

    def tile(i, base, d, rows, m_prev, acc_prev):
        j, cols, side = scores(i, base, d, rows)
        shift = m_prev if side is None else m_prev - side
        rel = [(col - shift).astype(BF16) for col in cols]
        rm = jnp.max(functools.reduce(jnp.maximum, rel), axis=1, keepdims=True)
        delta = jnp.maximum(rm, 0.0)
        p = jnp.concatenate([jnp.exp2(x - delta) for x in rel], axis=1)
        d32 = jnp.broadcast_to(delta.astype(F32), m_prev.shape)
        alpha = jnp.exp2(-d32)
        pv = _dot(p, vx_ref[pl.ds(pl.multiple_of(j * t, t), t), :])
        acc_new = jnp.concatenate([alpha * acc_prev[:, 0:dv] + pv[:, 0:dv],
                                   alpha * acc_prev[:, dv:2 * dv] + pv[:, dv:2 * dv]], axis=1)
        return m_prev + d32, acc_new

    streams = []
    for a in range(n_q):
        i, base = g * n_q + a, a * 2 * t
        q = q_ref[a * t:(a + 1) * t, :]
        lane = lax.broadcasted_iota(jnp.int32, q.shape, 1)
        zero = jnp.zeros_like(q)
        q2_ref[base:base + t, :] = jnp.where(lane < DIFF_HEAD_DIM, q, zero)
        q2_ref[base + t:base + 2 * t, :] = jnp.where(lane >= DIFF_HEAD_DIM, q, zero)
        for r0 in range(0, 2 * t, rc):
            streams.append((i, base, slice(r0, r0 + rc)))
    state = []
    for i, base, rows in streams:
        m0 = jnp.max(scores(i, base, 0, rows, LANES)[1][0], axis=1, keepdims=True)
        state.append((jnp.broadcast_to(m0, (rc, LANES)), jnp.zeros((rc, 2 * dv), F32)))
    for d in range(nk):
        state = [tile(i, base, d, rows, *st) for (i, base, rows), st in zip(streams, state)]

    for a in range(n_q):
        acc = jnp.concatenate([acc for _, acc in state[a * n_chain:(a + 1) * n_chain]], axis=0)
        o = acc[:, 0:dv] / acc[:, dv:dv + 1]
        o = o[0:t] - lam * o[t:2 * t]
        o = _rms(o, sg_ref[...]) * (1.0 - lambda_init)
        o_ref[a * t:(a + 1) * t, :] = o.astype(o_ref.dtype)


def _attention(qkv, bias_tiles, lq1, lk1, lq2, lk2, subln, *, n_heads, lambda_init, t, n_chain, n_q):
    b, s, _ = qkv.shape
    nk = s // t
    assert nk * t == s and t % LANES == 0 and t + 1 >= 91 and (2 * t) % n_chain == 0 and nk % n_q == 0
    kern = functools.partial(_attn_kernel, t=t, nk=nk, n_q=n_q, n_chain=n_chain, lambda_init=lambda_init)
    vec = lambda n: pl.BlockSpec((1, n), lambda b_, h, i: (0, 0))
    return pl.pallas_call(
        kern, grid=(b, n_heads, nk // n_q),
        in_specs=[
            pl.BlockSpec((None, n_q * t, V_HEAD_DIM), lambda b_, h, i: (b_, i, h)),
            pl.BlockSpec((None, s, V_HEAD_DIM), lambda b_, h, i: (b_, 0, n_heads + h)),
            pl.BlockSpec((None, s, V_HEAD_DIM), lambda b_, h, i: (b_, 0, 2 * n_heads + h)),
            pl.BlockSpec((None, N_BIAS_TILES, t, t), lambda b_, h, i: (h, 0, 0, 0)),
            vec(DIFF_HEAD_DIM), vec(DIFF_HEAD_DIM), vec(DIFF_HEAD_DIM), vec(DIFF_HEAD_DIM),
            vec(V_HEAD_DIM),
        ],
        out_specs=pl.BlockSpec((None, n_q * t, V_HEAD_DIM), lambda b_, h, i: (b_, i, h)),
        out_shape=jax.ShapeDtypeStruct((b, s, n_heads * V_HEAD_DIM), BF16),
        scratch_shapes=[pltpu.VMEM((n_q * 2 * t, V_HEAD_DIM), BF16),
                        pltpu.VMEM((s, 2 * V_HEAD_DIM), BF16)],
        compiler_params=_params("parallel", "parallel", "arbitrary"), name="diff_attention",
    )(qkv, qkv, qkv, bias_tiles, lq1.reshape(1, -1), lk1.reshape(1, -1),
      lq2.reshape(1, -1), lk2.reshape(1, -1), subln.reshape(1, -1))


def _dft_tables(n_hi):
    n = n_hi * N_LO
    half = n_hi // 2
    hi = np.arange(n_hi)
    lo = np.arange(N_LO)
    f_hi = np.exp(-2j * np.pi * ((np.outer(hi, hi) % n_hi) / n_hi))
    fr, fi = f_hi.real, f_hi.imag
    s1_pair = np.block([[fr[:, :half], -fi[:, :half]], [fi[:, :half], fr[:, :half]]])
    s1_real = np.concatenate([fr[:, :half], fi[:, :half]], axis=0)
    ph = (hi[:, None, None] * lo[None, None, :] + n_hi * lo[None, :, None] * lo[None, None, :]) % n
    g = np.exp(-2j * np.pi * ph / n)
    g2 = np.concatenate([np.concatenate([g.real, -g.imag], axis=2),
                         np.concatenate([g.imag, g.real], axis=2)], axis=1)
    gh = np.stack([g2, np.swapaxes(g2, 1, 2)], axis=1)
    s4 = np.block([[fr[:half], fi[:half]], [-fi[:half], fr[:half]]]) / n
    cast = lambda a: jnp.asarray(a.astype(np.float32)).astype(BF16)
    return cast(s1_pair), cast(s1_real), cast(gh), cast(s4)


def _hdot(a, b, dims=(((1,), (0,)), ((), ()))):
    (ah, al), (bh, bl) = _split(a), _split(b)
    dg = functools.partial(lax.dot_general, dimension_numbers=dims, preferred_element_type=F32)
    return dg(ah, bh) + dg(ah, bl) + dg(al, bh)


def _filter_kernel(zt_ref, t_ref, w1_ref, b1_ref, w2_ref, b2_ref, w3_ref, b3_ref, w4_ref, fr_ref, dl_ref,
                   hf_ref, hb_ref, *, c):
    i = pl.program_id(0)
    fr = fr_ref[...]
    a = jnp.sin(fr * (_hdot(w1_ref[...], zt_ref[...]) + b1_ref[...]))
    a = jnp.sin(fr * (_hdot(w2_ref[...], a) + b2_ref[...]))
    a = jnp.sin(fr * (_hdot(w3_ref[...], a) + b3_ref[...]))
    hh = _hdot(a, w4_ref[...], (((0,), (0,)), ((), ())))
    decay = jnp.exp(-t_ref[...] * jnp.abs(dl_ref[...]))
    hf_ref[...] = (hh[:, 0:c] * decay).astype(hf_ref.dtype)
    r = lax.broadcasted_iota(jnp.int32, (hh.shape[0], c), 0)
    lag0 = jnp.logical_and(i == 0, r == 0)
    hb_ref[...] = jnp.where(lag0, 0.0, hh[:, c:2 * c] * decay).astype(hb_ref.dtype)


def _filter_taps(z, w1, b1, w2, b2, w3, b3, w4, freq, deltas, *, tb=512):
    s, emb = z.shape
    c = deltas.shape[0]
    assert s % tb == 0
    col = lambda v: v.reshape(-1, 1)
    args = [jnp.pad(z, ((0, 0), (0, LANES - emb))).T, z[:, 0:1],
            jnp.pad(w1, ((0, LANES - emb), (0, 0))).T, col(b1), w2.T, col(b2), w3.T, col(b3), w4,
            col(freq), deltas.reshape(1, c)]
    specs = [pl.BlockSpec((LANES, tb), lambda i: (0, i)), pl.BlockSpec((tb, 1), lambda i: (i, 0))] + [
        pl.BlockSpec(a.shape, lambda i: (0, 0)) for a in args[2:]]
    out = pl.BlockSpec((tb, c), lambda i: (i, 0))
    shp = jax.ShapeDtypeStruct((s, c), BF16)
    return pl.pallas_call(
        functools.partial(_filter_kernel, c=c), grid=(s // tb,),
        in_specs=specs, out_specs=[out, out], out_shape=[shp, shp],
        compiler_params=_params("parallel"), name="hyena_filter_taps",
    )(*args)


def _lo_major(ref):
    return jnp.swapaxes(ref[...], 0, 1)


def _dft1_kernel(x0_ref, x1_ref, f_ref, a_ref):
    n_out = a_ref.shape[0]
    x0, x1 = _lo_major(x0_ref), _lo_major(x1_ref)
    a = jnp.stack([_dot(f_ref[...], jnp.concatenate([x0[j], x1[j]], axis=0)).astype(a_ref.dtype)
                   for j in range(x0.shape[0])], axis=0)
    a = jnp.swapaxes(a, 0, 1)
    a_ref[:, 0] = a[0:n_out]
    a_ref[:, 1] = a[n_out:2 * n_out]


def _dft1_pair(xv, f_mat, *, n_hi, t_lo=16):
    b, half, _, c = xv.shape
    spec = lambda off: pl.BlockSpec((None, half, t_lo, c), lambda p, j: (2 * p + off, 0, j, 0))
    return pl.pallas_call(
        _dft1_kernel, grid=(b // 2, N_LO // t_lo),
        in_specs=[spec(0), spec(1), pl.BlockSpec(f_mat.shape, lambda p, j: (0, 0))],
        out_specs=pl.BlockSpec((None, n_hi, 2, t_lo, c), lambda p, j: (p, 0, 0, j, 0)),
        out_shape=jax.ShapeDtypeStruct((b // 2, n_hi, 2, N_LO, c), BF16),
        compiler_params=_params("parallel", "parallel"), name="hyena_dft1",
    )(xv, xv, f_mat)


def _dft1_filter_kernel(hf_ref, hb_ref, f_ref, fa_ref):
    n_out = fa_ref.shape[0]
    c = hf_ref.shape[2]
    hf, hb = _lo_major(hf_ref), _lo_major(hb_ref)
    a = jnp.stack([_dot(f_ref[...], jnp.concatenate([hf[j], hb[j]], axis=1)).astype(fa_ref.dtype)
                   for j in range(hf.shape[0])], axis=0)
    a = jnp.swapaxes(a, 0, 1)
    for part, (r0, c0) in enumerate(((0, 0), (n_out, 0), (0, c), (n_out, c))):
        fa_ref[:, part] = a[r0:r0 + n_out, :, c0:c0 + c]


def _dft1_filter(hf, hb, f_mat, *, n_hi, t_lo=16):
    half, _, c = hf.shape
    blk = pl.BlockSpec((half, t_lo, c), lambda j: (0, j, 0))
    return pl.pallas_call(
        _dft1_filter_kernel, grid=(N_LO // t_lo,),
        in_specs=[blk, blk, pl.BlockSpec(f_mat.shape, lambda j: (0, 0))],
        out_specs=pl.BlockSpec((n_hi, 4, t_lo, c), lambda j: (0, 0, j, 0)),
        out_shape=jax.ShapeDtypeStruct((n_hi, 4, N_LO, c), BF16),
        compiler_params=_params("parallel"), name="hyena_filter_dft1",
    )(hf, hb, f_mat)


def _spectral_kernel(a_ref, fa_ref, gh_ref, cc_ref):
    c = a_ref.shape[-1]
    for u in range(a_ref.shape[0]):
        a = jnp.concatenate([a_ref[u].reshape(2 * N_LO, c), fa_ref[u, 0:2].reshape(2 * N_LO, c),
                             fa_ref[u, 2:4].reshape(2 * N_LO, c)], axis=1)
        x = _dot(gh_ref[u, 0], a)
        xr, xi = x[0:N_LO, 0:c], x[N_LO:2 * N_LO, 0:c]
        kr = x[0:N_LO, c:2 * c] + x[0:N_LO, 2 * c:3 * c]
        ki = x[N_LO:2 * N_LO, c:2 * c] - x[N_LO:2 * N_LO, 2 * c:3 * c]
        y = jnp.concatenate([xr * kr - xi * ki, xr * ki + xi * kr], axis=0).astype(BF16)
        cc_ref[u] = _dot(gh_ref[u, 1], y).astype(cc_ref.dtype).reshape(2, N_LO, c)


def _spectral(a, fa, gh, *, n_hi, kb=4):
    p, _, _, _, c = a.shape
    assert n_hi % kb == 0
    blk = pl.BlockSpec((None, kb, 2, N_LO, c), lambda q, k: (q, k, 0, 0, 0))
    return pl.pallas_call(
        _spectral_kernel, grid=(p, n_hi // kb),
        in_specs=[blk, pl.BlockSpec((kb, 4, N_LO, c), lambda q, k: (k, 0, 0, 0)),
                  pl.BlockSpec((kb, 2, 2 * N_LO, 2 * N_LO), lambda q, k: (k, 0, 0, 0))],
        out_specs=blk, out_shape=jax.ShapeDtypeStruct(a.shape, BF16),
        compiler_params=_params("parallel", "parallel"), name="hyena_spectral",
    )(a, fa, gh)


def _idft_kernel(cc_ref, m_ref, o_ref):
    half = o_ref.shape[1]
    cr, ci = jnp.swapaxes(cc_ref[:, 0], 0, 1), jnp.swapaxes(cc_ref[:, 1], 0, 1)
    y = jnp.stack([_dot(m_ref[...], jnp.concatenate([cr[j], ci[j]], axis=0)) for j in range(cr.shape[0])],
                  axis=0)
    y = jnp.swapaxes(y, 0, 1)
    o_ref[0] = y[0:half]
    o_ref[1] = y[half:2 * half]


def _idft(cc, s4, *, n_hi, t_lo=16):
    p, _, _, _, c = cc.shape
    half = n_hi // 2
    return pl.pallas_call(
        _idft_kernel, grid=(p, N_LO // t_lo),
        in_specs=[pl.BlockSpec((None, n_hi, 2, t_lo, c), lambda q, j: (q, 0, 0, j, 0)),
                  pl.BlockSpec(s4.shape, lambda q, j: (0, 0))],
        out_specs=pl.BlockSpec((2, half, t_lo, c), lambda q, j: (q, 0, j, 0)),
        out_shape=jax.ShapeDtypeStruct((2 * p, half, N_LO, c), F32),
        compiler_params=_params("parallel", "parallel"), name="hyena_idft",
    )(cc, s4)


def _filter_positions(s):
    t = jnp.linspace(0.0, 1.0, s, dtype=F32)[:, None]
    bands = (HYENA_EMB_DIM - 1) // 2
    w = 2.0 * math.pi * jnp.arange(s, dtype=F32)[:, None] / s
    f = jnp.linspace(1e-4, bands - 1, bands, dtype=F32)[None, :]
    fw = f * w
    return jnp.concatenate([t, jnp.cos(fw), -jnp.sin(fw)], axis=-1)


def _hyena_conv(vv, w1, b1, w2, b2, w3, b3, w4, freq):
    b, s, c = vv.shape
    n_hi = 2 * s // N_LO
    half = n_hi // 2
    assert b % 2 == 0 and n_hi * N_LO == 2 * s
    s1_pair, s1_real, gh, s4 = _dft_tables(n_hi)

    max_decay = math.log(HYENA_DECAY_TARGET) / HYENA_FAST_DECAY
    min_decay = math.log(HYENA_DECAY_TARGET) / HYENA_SLOW_DECAY
    deltas = jnp.linspace(min_decay, max_decay, c, dtype=F32)
    hf, hb = _filter_taps(_filter_positions(s), w1, b1, w2, b2, w3, b3, w4, freq, deltas)
    fa = _dft1_filter(hf.reshape(half, N_LO, c), hb.reshape(half, N_LO, c), s1_real, n_hi=n_hi)

    a = _dft1_pair(vv.reshape(b, half, N_LO, c), s1_pair, n_hi=n_hi)
    cc = _spectral(a, fa, gh, n_hi=n_hi)
    return _idft(cc, s4, n_hi=n_hi).reshape(b, s, c)


def kernel(x, rel_bias, ffn1_norm, ffn1_w_gate, ffn1_w_up, ffn1_w_down, mix_norm, w_in,
           lambda_q1, lambda_k1, lambda_q2, lambda_k2, diff_subln,
           hy_conv_w, hy_conv_b, hy_f_w1, hy_f_b1, hy_f_w2, hy_f_b2, hy_f_w3, hy_f_b3,
           hy_f_w4, hy_f_freq, hy_bias, hy_out_norm, w_out,
           ffn2_norm, ffn2_w_gate, ffn2_w_up, ffn2_w_down, final_norm):
    b, s, d = x.shape
    depth = w_in.shape[0]
    d_att = diff_subln.shape[1] * rel_bias.shape[1]
    n_heads = rel_bias.shape[1]
    attn_tile = min(512, s)
    bf = lambda a: a.astype(BF16)

    xf = x.reshape(b * s, d)
    for l in range(depth):
        last = l == depth - 1
        xf = _ffn(xf, ffn1_norm[l], ffn1_w_gate[l], ffn1_w_up[l], ffn1_w_down[l])
        qkv, vv, x0 = _mix(xf, mix_norm[l], bf(w_in[l]), hy_conv_w[l], hy_conv_b[l], d_att=d_att, seq=s)
        lambda_init = 0.8 - 0.6 * math.exp(-0.3 * l)
        att = _attention(qkv.reshape(b, s, -1), _bias_tiles(rel_bias, attn_tile),
                         lambda_q1[l], lambda_k1[l], lambda_q2[l], lambda_k2[l], diff_subln[l],
                         n_heads=n_heads, lambda_init=lambda_init, t=attn_tile, n_chain=4,
                         n_q=2 if (s // attn_tile) % 2 == 0 else 1)
        y = _hyena_conv(vv.reshape(b, s, -1), hy_f_w1[l], hy_f_b1[l], hy_f_w2[l], hy_f_b2[l],
                        hy_f_w3[l], hy_f_b3[l], hy_f_w4[l], hy_f_freq[l])
        hy_parts = (y.reshape(b * s, -1), vv, x0, hy_bias[l], hy_out_norm[l])
        wo = bf(w_out[l])
        xf = _ffn(xf, ffn2_norm[l], ffn2_w_gate[l], ffn2_w_up[l], ffn2_w_down[l],
                  pre=(att.reshape(b * s, -1), wo[:d_att], wo[d_att:], hy_parts),
                  final_g=final_norm if last else None)
    if depth == 0:
        raise ValueError("depth must be positive")
    return xf.reshape(b, s, d)
```

```python
import functools
import math

import numpy as np
import jax
import jax.numpy as jnp
from jax import lax
from jax.experimental import pallas as pl
from jax.experimental.pallas import tpu as pltpu

F32 = jnp.float32
BF16 = jnp.bfloat16

RMS_EPS = 1e-6
LANES = 128
DIFF_HEAD_DIM = 64
V_HEAD_DIM = 2 * DIFF_HEAD_DIM
REL_BUCKETS = 32
REL_MAX_DIST = 128
N_HYENA_GROUPS = 8
HYENA_EMB_DIM = 33
HYENA_DECAY_TARGET = 1e-2
HYENA_FAST_DECAY = 0.3
HYENA_SLOW_DECAY = 1.5
N_LO = 128
LOG2_E = math.log2(math.e)
VMEM_LIMIT = 56 * 1024 * 1024


def _params(*sem):
    return pltpu.CompilerParams(dimension_semantics=sem, vmem_limit_bytes=VMEM_LIMIT)


def _resident(shape):
    return pl.BlockSpec(shape, lambda *_: (0,) * len(shape), pipeline_mode=pl.Buffered(1))


def _rms(x, g):
    return x * lax.rsqrt(jnp.mean(x * x, axis=-1, keepdims=True) + RMS_EPS) * g


def _dot(a, b):
    return jnp.dot(a, b, preferred_element_type=F32)


def _split(a):
    hi = a.astype(BF16)
    return hi, (a - hi.astype(F32)).astype(BF16)


def _hyena_gate_norm(y_ref, vv_ref, x0_ref, fb_ref, og_ref, mg_ref):
    vv = vv_ref[...].astype(F32)
    y = (y_ref[...] + vv * fb_ref[...]) * x0_ref[...].astype(F32)
    hi, lo = _split(y * y)
    ms = _dot(hi, mg_ref[...]) + _dot(lo, mg_ref[...])
    return y * lax.rsqrt(ms + RMS_EPS) * og_ref[...]


N_STAGE = 8


def _stage_weights(srcs, dsts, stages, sem):
    jobs = [(src, dst, stage, c) for src, dst, stage in zip(srcs, dsts, stages) for c in range(N_STAGE)]

    def copy(k):
        src, _, stage, c = jobs[k]
        rows = stage.shape[1]
        return pltpu.make_async_copy(src.at[pl.ds(c * rows, rows), :], stage.at[k % 2], sem.at[k % 2])

    copy(0).start()
    for k, (_, dst, stage, c) in enumerate(jobs):
        if k + 1 < len(jobs):
            copy(k + 1).start()
        copy(k).wait()
        rows = stage.shape[1]
        dst[pl.ds(c * rows, rows), :] = stage[k % 2].astype(dst.dtype)


def _ffn_kernel(*refs, ff_chunk, n_chunk, pre, post):
    if pre:
        x_ref, att_ref, woa_ref, woh_ref = refs[:4]
        hy_refs = refs[4:10]
        refs = refs[10:]
    else:
        x_ref = refs[0]
        refs = refs[1:]
    g_ref, wg_hbm, wu_hbm, wd_hbm = refs[:4]
    refs = refs[4:]
    if post:
        fg_ref, o_ref = refs[:2]
        refs = refs[2:]
    else:
        o_ref = refs[0]
        refs = refs[1:]
    wg_ref, wu_ref, wd_ref, stage_in, stage_out, sem = refs

    @pl.when(pl.program_id(0) == 0)
    def _():
        _stage_weights((wg_hbm, wu_hbm, wd_hbm), (wg_ref, wu_ref, wd_ref),
                       (stage_in, stage_in, stage_out), sem)

    x = x_ref[...]
    if pre:
        hy = _hyena_gate_norm(*hy_refs).astype(BF16)
        x = x + _dot(att_ref[...], woa_ref[...]) + _dot(hy, woh_ref[...])
    xn = _rms(x, g_ref[...]).astype(BF16)
    acc = jnp.zeros(x.shape, F32)
    for c in range(n_chunk):
        sl = slice(c * ff_chunk, (c + 1) * ff_chunk)
        gate = _dot(xn, wg_ref[:, sl])
        up = _dot(xn, wu_ref[:, sl])
        h = (jax.nn.silu(gate) * up).astype(BF16)
        acc = acc + _dot(h, wd_ref[sl, :])
    y = x + 0.5 * acc
    if post:
        y = _rms(y, fg_ref[...])
    o_ref[...] = y


def _ffn(x, norm_g, wg, wu, wd, *, pre=None, final_g=None, tm=512, ff_chunk=256):
    m, d = x.shape
    dff = wg.shape[1]
    n_chunk = dff // ff_chunk
    assert n_chunk * ff_chunk == dff and m % tm == 0
    assert d % (16 * N_STAGE) == 0 and dff % (16 * N_STAGE) == 0
    row = lambda w: pl.BlockSpec((tm, w), lambda i: (i, 0))
    args, specs = [x], [row(d)]
    if pre is not None:
        att, woa, woh, (y, vv, x0, filt_bias, out_g) = pre
        c = y.shape[1]
        gdim = c // N_HYENA_GROUPS
        assert gdim & (gdim - 1) == 0
        grp = np.arange(c) // gdim
        mg = jnp.asarray((grp[:, None] == grp[None, :]).astype(np.float32) / gdim).astype(BF16)
        args += [att, woa, woh, y, vv, x0, filt_bias.reshape(1, c), out_g.reshape(1, c), mg]
        specs += [row(att.shape[1]), _resident(woa.shape), _resident(woh.shape), row(c), row(c), row(c),
                  _resident((1, c)), _resident((1, c)), _resident((c, c))]
    hbm = pl.BlockSpec(memory_space=pl.ANY)
    args += [norm_g.reshape(1, d), wg, wu, wd]
    specs += [_resident((1, d)), hbm, hbm, hbm]
    if final_g is not None:
        args.append(final_g.reshape(1, d))
        specs.append(_resident((1, d)))
    kern = functools.partial(_ffn_kernel, ff_chunk=ff_chunk, n_chunk=n_chunk,
                             pre=pre is not None, post=final_g is not None)
    return pl.pallas_call(
        kern, grid=(m // tm,), in_specs=specs, out_specs=row(d),
        out_shape=jax.ShapeDtypeStruct((m, d), F32),
        scratch_shapes=[pltpu.VMEM((d, dff), BF16), pltpu.VMEM((d, dff), BF16), pltpu.VMEM((dff, d), BF16),
                        pltpu.VMEM((2, d // N_STAGE, dff), F32), pltpu.VMEM((2, dff // N_STAGE, d), F32),
                        pltpu.SemaphoreType.DMA((2,))],
        compiler_params=_params("arbitrary"), name="ffn_pre" if pre is not None else "ffn",
    )(*args)


HALO = 8


def _mix_kernel(x_ref, xp_ref, xn_ref, g_ref, w_ref, cw_ref, cb_ref, qkv_ref, vv_ref, x0_ref,
                *, d_att, c, scale, tiles_per_seq):
    i = pl.program_id(0)
    n_att = 3 * d_att
    xn = _rms(x_ref[...], g_ref[...]).astype(BF16)
    for c0 in range(0, n_att, d_att):
        p = _dot(xn, w_ref[:, c0:c0 + d_att])
        if c0 == 0:
            p = p * scale
        qkv_ref[:, c0:c0 + d_att] = p.astype(BF16)

    halo = _rms(jnp.concatenate([xp_ref[...], xn_ref[...]], axis=0), g_ref[...]).astype(BF16)
    first = i % tiles_per_seq == 0
    last = i % tiles_per_seq == tiles_per_seq - 1
    rows = x_ref.shape[0]
    r = lax.broadcasted_iota(jnp.int32, (HALO, c), 0)
    head, tail = slice(0, HALO), slice(rows - HALO, rows)
    u = []
    for part in range(3):
        cs = slice(n_att + part * c, n_att + (part + 1) * c)
        ws = slice(part * c, (part + 1) * c)
        z = _dot(xn, w_ref[:, cs])
        zh = _dot(halo, w_ref[:, cs])
        prev_row = jnp.where(first, 0.0, zh[HALO - 1:HALO])
        next_row = jnp.where(last, 0.0, zh[HALO:HALO + 1])
        z_m1 = pltpu.roll(z, 1, 0)
        z_p1 = pltpu.roll(z, rows - 1, 0)
        conv = lambda zm, zc, zp: (cb_ref[:, ws] + zm * cw_ref[0:1, ws] + zc * cw_ref[1:2, ws]
                                   + zp * cw_ref[2:3, ws])
        mid = conv(z_m1, z, z_p1)
        top = conv(jnp.where(r == 0, prev_row, z_m1[head]), z[head], z_p1[head])
        bot = conv(z_m1[tail], z[tail], jnp.where(r == HALO - 1, next_row, z_p1[tail]))
        u.append(jnp.concatenate([top, mid[HALO:rows - HALO], bot], axis=0))
    x0_ref[...] = u[0].astype(x0_ref.dtype)
    vv_ref[...] = (u[2] * u[1]).astype(vv_ref.dtype)


def _mix(x, norm_g, w_in, conv_w, conv_b, *, d_att, seq, tm=512):
    m, d = x.shape
    n_att = 3 * d_att
    c = (w_in.shape[1] - n_att) // 3
    assert m % tm == 0 and seq % tm == 0 and tm % HALO == 0
    kern = functools.partial(_mix_kernel, d_att=d_att, c=c, scale=DIFF_HEAD_DIM ** -0.5 * LOG2_E,
                             tiles_per_seq=seq // tm)
    sub = tm // HALO
    last_blk = m // HALO - 1
    row = lambda w: pl.BlockSpec((tm, w), lambda i: (i, 0))
    return pl.pallas_call(
        kern, grid=(m // tm,),
        in_specs=[row(d),
                  pl.BlockSpec((HALO, d), lambda i: (jnp.maximum(i * sub - 1, 0), 0)),
                  pl.BlockSpec((HALO, d), lambda i: (jnp.minimum((i + 1) * sub, last_blk), 0)),
                  _resident((1, d)), _resident(w_in.shape), _resident(conv_w.shape), _resident((1, 3 * c))],
        out_specs=[row(n_att), row(c), row(c)],
        out_shape=[jax.ShapeDtypeStruct((m, n_att), BF16), jax.ShapeDtypeStruct((m, c), BF16),
                   jax.ShapeDtypeStruct((m, c), BF16)],
        compiler_params=_params("parallel"), name="mix",
    )(x, x, x, norm_g.reshape(1, d), w_in, conv_w, conv_b.reshape(1, 3 * c))


N_BIAS_TILES = 5


def _bias_kernel(tab_ref, o_ref, *, t):
    h = pl.program_id(0)
    d = pl.program_id(1) - N_BIAS_TILES // 2
    half = REL_BUCKETS // 2
    max_exact = half // 2
    rel = lax.broadcasted_iota(jnp.int32, (8, 2 * t), 1) + (d - 1) * t
    ret = jnp.where(rel > 0, half, 0)
    n = jnp.abs(rel)
    nf = jnp.maximum(n, 1).astype(F32)
    large = max_exact + (jnp.log(nf / max_exact) / math.log(REL_MAX_DIST / max_exact)
                         * (half - max_exact)).astype(jnp.int32)
    large = jnp.minimum(large, half - 1)
    bucket = ret + jnp.where(n < max_exact, n, large)
    row = jnp.zeros(rel.shape, F32)
    for b in range(REL_BUCKETS):
        row = jnp.where(bucket == b, tab_ref[b, h], row)
    rows = jnp.broadcast_to(row[0:1, :] * LOG2_E, (t, 2 * t))
    o_ref[...] = pltpu.roll(rows, 0, 1, stride=1, stride_axis=0)[:, t:2 * t]


def _bias_tiles(rel_bias, t):
    nh = rel_bias.shape[1]
    return pl.pallas_call(
        functools.partial(_bias_kernel, t=t), grid=(nh, N_BIAS_TILES),
        in_specs=[pl.BlockSpec(memory_space=pltpu.SMEM)],
        out_specs=pl.BlockSpec((None, None, t, t), lambda h, d: (h, d, 0, 0)),
        out_shape=jax.ShapeDtypeStruct((nh, N_BIAS_TILES, t, t), F32),
        compiler_params=_params("parallel", "parallel"), name="rel_bias_tiles",
    )(rel_bias)


def _attn_kernel(q_ref, k_ref, v_ref, bt_ref, lq1_ref, lk1_ref, lq2_ref, lk2_ref, sg_ref,
                 o_ref, q2_ref, vx_ref, *, t, nk, n_q, n_chain, lambda_init):
    g = pl.program_id(2)
    dv = v_ref.shape[1]

    @pl.when(g == 0)
    def _():
        vx_ref[:, 0:dv] = v_ref[...]
        ones_lane = lax.broadcasted_iota(jnp.int32, (v_ref.shape[0], dv), 1) == 0
        vx_ref[:, dv:2 * dv] = jnp.where(ones_lane, 1.0, 0.0).astype(vx_ref.dtype)

    far = N_BIAS_TILES // 2
    rc = 2 * t // n_chain
    lam = (jnp.exp(jnp.sum(lq1_ref[...] * lk1_ref[...])) - jnp.exp(jnp.sum(lq2_ref[...] * lk2_ref[...]))
           + lambda_init)

    def scores(i, base, d, rows, n_keys=t):
        wrapped = i + d >= nk
        j = jnp.where(wrapped, i + d - nk, i + d)
        k = k_ref[pl.ds(pl.multiple_of(j * t, t), n_keys), :]
        q2 = q2_ref[base + rows.start:base + rows.stop, :]
        s = lax.dot_general(q2, k, (((1,), (1,)), ((), ())), preferred_element_type=F32)
        cols = [s[:, c0:c0 + LANES] for c0 in range(0, n_keys, LANES)]
        if d <= 1 or d >= nk - 1:
            qr = slice(rows.start % t, rows.start % t + rows.stop - rows.start)
            bias = bt_ref[jnp.clip(j - i, -far, far) + far, qr, :]
            return j, [c + bias[:, n * LANES:(n + 1) * LANES] for n, c in enumerate(cols)], None
        side = jnp.where(wrapped, bt_ref[0, 0:1, 0:LANES], bt_ref[2 * far, 0:1, 0:LANES])
        return j, cols, side

    def tile(i, base, d, rows, m_prev, acc_prev):
        j, cols, side = scores(i, base, d, rows)
        shift = m_prev if side is None else m_prev - side
        rel = [(col - shift).astype(BF16) for col in cols]
        rm = jnp.max(functools.reduce(jnp.maximum, rel), axis=1, keepdims=True)
        delta = jnp.maximum(rm, 0.0)
        p = jnp.concatenate([jnp.exp2(x - delta) for x in rel], axis=1)
        d32 = jnp.broadcast_to(delta.astype(F32), m_prev.shape)
        alpha = jnp.exp2(-d32)
        pv = _dot(p, vx_ref[pl.ds(pl.multiple_of(j * t, t), t), :])
        acc_new = jnp.concatenate([alpha * acc_prev[:, 0:dv] + pv[:, 0:dv],
                                   alpha * acc_prev[:, dv:2 * dv] + pv[:, dv:2 * dv]], axis=1)
        return m_prev + d32, acc_new

    streams = []
    for a in range(n_q):
        i, base = g * n_q + a, a * 2 * t
        q = q_ref[a * t:(a + 1) * t, :]
        lane = lax.broadcasted_iota(jnp.int32, q.shape, 1)
        zero = jnp.zeros_like(q)
        q2_ref[base:base + t, :] = jnp.where(lane < DIFF_HEAD_DIM, q, zero)
        q2_ref[base + t:base + 2 * t, :] = jnp.where(lane >= DIFF_HEAD_DIM, q, zero)
        for r0 in range(0, 2 * t, rc):
            streams.append((i, base, slice(r0, r0 + rc)))
    state = []
    for i, base, rows in streams:
        m0 = jnp.max(scores(i, base, 0, rows, LANES)[1][0], axis=1, keepdims=True)
        state.append((jnp.broadcast_to(m0, (rc, LANES)), jnp.zeros((rc, 2 * dv), F32)))
    for d in range(nk):
        state = [tile(i, base, d, rows, *st) for (i, base, rows), st in zip(streams, state)]

    for a in range(n_q):
        acc = jnp.concatenate([acc for _, acc in state[a * n_chain:(a + 1) * n_chain]], axis=0)
        o = acc[:, 0:dv] / acc[:, dv:dv + 1]
        o = o[0:t] - lam * o[t:2 * t]
        o = _rms(o, sg_ref[...]) * (1.0 - lambda_init)
        o_ref[a * t:(a + 1) * t, :] = o.astype(o_ref.dtype)


def _attention(qkv, bias_tiles, lq1, lk1, lq2, lk2, subln, *, n_heads, lambda_init, t, n_chain, n_q):
    b, s, _ = qkv.shape
    nk = s // t
    assert nk * t == s and t % LANES == 0 and t + 1 >= 91 and (2 * t) % n_chain == 0 and nk % n_q == 0
    kern = functools.partial(_attn_kernel, t=t, nk=nk, n_q=n_q, n_chain=n_chain, lambda_init=lambda_init)
    vec = lambda n: pl.BlockSpec((1, n), lambda b_, h, i: (0, 0))
    return pl.pallas_call(
        kern, grid=(b, n_heads, nk // n_q),
        in_specs=[
            pl.BlockSpec((None, n_q * t, V_HEAD_DIM), lambda b_, h, i: (b_, i, h)),
            pl.BlockSpec((None, s, V_HEAD_DIM), lambda b_, h, i: (b_, 0, n_heads + h)),
            pl.BlockSpec((None, s, V_HEAD_DIM), lambda b_, h, i: (b_, 0, 2 * n_heads + h)),
            pl.BlockSpec((None, N_BIAS_TILES, t, t), lambda b_, h, i: (h, 0, 0, 0)),
            vec(DIFF_HEAD_DIM), vec(DIFF_HEAD_DIM), vec(DIFF_HEAD_DIM), vec(DIFF_HEAD_DIM),
            vec(V_HEAD_DIM),
        ],
        out_specs=pl.BlockSpec((None, n_q * t, V_HEAD_DIM), lambda b_, h, i: (b_, i, h)),
        out_shape=jax.ShapeDtypeStruct((b, s, n_heads * V_HEAD_DIM), BF16),
        scratch_shapes=[pltpu.VMEM((n_q * 2 * t, V_HEAD_DIM), BF16),
                        pltpu.VMEM((s, 2 * V_HEAD_DIM), BF16)],
        compiler_params=_params("parallel", "parallel", "arbitrary"), name="diff_attention",
    )(qkv, qkv, qkv, bias_tiles, lq1.reshape(1, -1), lk1.reshape(1, -1),
      lq2.reshape(1, -1), lk2.reshape(1, -1), subln.reshape(1, -1))


def _dft_tables(n_hi):
    n = n_hi * N_LO
    half = n_hi // 2
    hi = np.arange(n_hi)
    lo = np.arange(N_LO)
    f_hi = np.exp(-2j * np.pi * ((np.outer(hi, hi) % n_hi) / n_hi))
    fr, fi = f_hi.real, f_hi.imag
    s1_pair = np.block([[fr[:, :half], -fi[:, :half]], [fi[:, :half], fr[:, :half]]])
    s1_real = np.concatenate([fr[:, :half], fi[:, :half]], axis=0)
    ph = (hi[:, None, None] * lo[None, None, :] + n_hi * lo[None, :, None] * lo[None, None, :]) % n
    g = np.exp(-2j * np.pi * ph / n)
    g2 = np.concatenate([np.concatenate([g.real, -g.imag], axis=2),
                         np.concatenate([g.imag, g.real], axis=2)], axis=1)
    gh = np.stack([g2, np.swapaxes(g2, 1, 2)], axis=1)
    s4 = np.block([[fr[:half], fi[:half]], [-fi[:half], fr[:half]]]) / n
    cast = lambda a: jnp.asarray(a.astype(np.float32)).astype(BF16)
    return cast(s1_pair), cast(s1_real), cast(gh), cast(s4)


def _hdot(a, b, dims=(((1,), (0,)), ((), ()))):
    (ah, al), (bh, bl) = _split(a), _split(b)
    dg = functools.partial(lax.dot_general, dimension_numbers=dims, preferred_element_type=F32)
    return dg(ah, bh) + dg(ah, bl) + dg(al, bh)


def _filter_kernel(zt_ref, t_ref, w1_ref, b1_ref, w2_ref, b2_ref, w3_ref, b3_ref, w4_ref, fr_ref, dl_ref,
                   hf_ref, hb_ref, *, c):
    i = pl.program_id(0)
    fr = fr_ref[...]
    a = jnp.sin(fr * (_hdot(w1_ref[...], zt_ref[...]) + b1_ref[...]))
    a = jnp.sin(fr * (_hdot(w2_ref[...], a) + b2_ref[...]))
    a = jnp.sin(fr * (_hdot(w3_ref[...], a) + b3_ref[...]))
    hh = _hdot(a, w4_ref[...], (((0,), (0,)), ((), ())))
    decay = jnp.exp(-t_ref[...] * jnp.abs(dl_ref[...]))
    hf_ref[...] = (hh[:, 0:c] * decay).astype(hf_ref.dtype)
    r = lax.broadcasted_iota(jnp.int32, (hh.shape[0], c), 0)
    lag0 = jnp.logical_and(i == 0, r == 0)
    hb_ref[...] = jnp.where(lag0, 0.0, hh[:, c:2 * c] * decay).astype(hb_ref.dtype)


def _filter_taps(z, w1, b1, w2, b2, w3, b3, w4, freq, deltas, *, tb=512):
    s, emb = z.shape
    c = deltas.shape[0]
    assert s % tb == 0
    col = lambda v: v.reshape(-1, 1)
    args = [jnp.pad(z, ((0, 0), (0, LANES - emb))).T, z[:, 0:1],
            jnp.pad(w1, ((0, LANES - emb), (0, 0))).T, col(b1), w2.T, col(b2), w3.T, col(b3), w4,
            col(freq), deltas.reshape(1, c)]
    specs = [pl.BlockSpec((LANES, tb), lambda i: (0, i)), pl.BlockSpec((tb, 1), lambda i: (i, 0))] + [
        pl.BlockSpec(a.shape, lambda i: (0, 0)) for a in args[2:]]
    out = pl.BlockSpec((tb, c), lambda i: (i, 0))
    shp = jax.ShapeDtypeStruct((s, c), BF16)
    return pl.pallas_call(
        functools.partial(_filter_kernel, c=c), grid=(s // tb,),
        in_specs=specs, out_specs=[out, out], out_shape=[shp, shp],
        compiler_params=_params("parallel"), name="hyena_filter_taps",
    )(*args)


def _lo_major(ref):
    return jnp.swapaxes(ref[...], 0, 1)


def _dft1_kernel(x0_ref, x1_ref, f_ref, a_ref):
    n_out = a_ref.shape[0]
    x0, x1 = _lo_major(x0_ref), _lo_major(x1_ref)
    a = jnp.stack([_dot(f_ref[...], jnp.concatenate([x0[j], x1[j]], axis=0)).astype(a_ref.dtype)
                   for j in range(x0.shape[0])], axis=0)
    a = jnp.swapaxes(a, 0, 1)
    a_ref[:, 0] = a[0:n_out]
    a_ref[:, 1] = a[n_out:2 * n_out]


def _dft1_pair(xv, f_mat, *, n_hi, t_lo=16):
    b, half, _, c = xv.shape
    spec = lambda off: pl.BlockSpec((None, half, t_lo, c), lambda p, j: (2 * p + off, 0, j, 0))
    return pl.pallas_call(
        _dft1_kernel, grid=(b // 2, N_LO // t_lo),
        in_specs=[spec(0), spec(1), pl.BlockSpec(f_mat.shape, lambda p, j: (0, 0))],
        out_specs=pl.BlockSpec((None, n_hi, 2, t_lo, c), lambda p, j: (p, 0, 0, j, 0)),
        out_shape=jax.ShapeDtypeStruct((b // 2, n_hi, 2, N_LO, c), BF16),
        compiler_params=_params("parallel", "parallel"), name="hyena_dft1",
    )(xv, xv, f_mat)


def _dft1_filter_kernel(hf_ref, hb_ref, f_ref, fa_ref):
    n_out = fa_ref.shape[0]
    c = hf_ref.shape[2]
    hf, hb = _lo_major(hf_ref), _lo_major(hb_ref)
    a = jnp.stack([_dot(f_ref[...], jnp.concatenate([hf[j], hb[j]], axis=1)).astype(fa_ref.dtype)
                   for j in range(hf.shape[0])], axis=0)
    a = jnp.swapaxes(a, 0, 1)
    for part, (r0, c0) in enumerate(((0, 0), (n_out, 0), (0, c), (n_out, c))):
        fa_ref[:, part] = a[r0:r0 + n_out, :, c0:c0 + c]


def _dft1_filter(hf, hb, f_mat, *, n_hi, t_lo=16):
    half, _, c = hf.shape
    blk = pl.BlockSpec((half, t_lo, c), lambda j: (0, j, 0))
    return pl.pallas_call(
        _dft1_filter_kernel, grid=(N_LO // t_lo,),
        in_specs=[blk, blk, pl.BlockSpec(f_mat.shape, lambda j: (0, 0))],
        out_specs=pl.BlockSpec((n_hi, 4, t_lo, c), lambda j: (0, 0, j, 0)),
        out_shape=jax.ShapeDtypeStruct((n_hi, 4, N_LO, c), BF16),
        compiler_params=_params("parallel"), name="hyena_filter_dft1",
    )(hf, hb, f_mat)


def _spectral_kernel(a_ref, fa_ref, gh_ref, cc_ref):
    c = a_ref.shape[-1]
    for u in range(a_ref.shape[0]):
        a = jnp.concatenate([a_ref[u].reshape(2 * N_LO, c), fa_ref[u, 0:2].reshape(2 * N_LO, c),
                             fa_ref[u, 2:4].reshape(2 * N_LO, c)], axis=1)
        x = _dot(gh_ref[u, 0], a)
        xr, xi = x[0:N_LO, 0:c], x[N_LO:2 * N_LO, 0:c]
        kr = x[0:N_LO, c:2 * c] + x[0:N_LO, 2 * c:3 * c]
        ki = x[N_LO:2 * N_LO, c:2 * c] - x[N_LO:2 * N_LO, 2 * c:3 * c]
        y = jnp.concatenate([xr * kr - xi * ki, xr * ki + xi * kr], axis=0).astype(BF16)
        cc_ref[u] = _dot(gh_ref[u, 1], y).astype(cc_ref.dtype).reshape(2, N_LO, c)


def _spectral(a, fa, gh, *, n_hi, kb=8):
    p, _, _, _, c = a.shape
    assert n_hi % kb == 0
    blk = pl.BlockSpec((None, kb, 2, N_LO, c), lambda q, k: (q, k, 0, 0, 0))
    return pl.pallas_call(
        _spectral_kernel, grid=(p, n_hi // kb),
        in_specs=[blk, pl.BlockSpec((kb, 4, N_LO, c), lambda q, k: (k, 0, 0, 0)),
                  pl.BlockSpec((kb, 2, 2 * N_LO, 2 * N_LO), lambda q, k: (k, 0, 0, 0))],
        out_specs=blk, out_shape=jax.ShapeDtypeStruct(a.shape, BF16),
        compiler_params=_params("parallel", "parallel"), name="hyena_spectral",
    )(a, fa, gh)


def _idft_kernel(cc_ref, m_ref, o_ref):
    half = o_ref.shape[1]
    cr, ci = jnp.swapaxes(cc_ref[:, 0], 0, 1), jnp.swapaxes(cc_ref[:, 1], 0, 1)
    y = jnp.stack([_dot(m_ref[...], jnp.concatenate([cr[j], ci[j]], axis=0)) for j in range(cr.shape[0])],
                  axis=0)
    y = jnp.swapaxes(y, 0, 1)
    o_ref[0] = y[0:half]
    o_ref[1] = y[half:2 * half]


def _idft(cc, s4, *, n_hi, t_lo=16):
    p, _, _, _, c = cc.shape
    half = n_hi // 2
    return pl.pallas_call(
        _idft_kernel, grid=(p, N_LO // t_lo),
        in_specs=[pl.BlockSpec((None, n_hi, 2, t_lo, c), lambda q, j: (q, 0, 0, j, 0)),
                  pl.BlockSpec(s4.shape, lambda q, j: (0, 0))],
        out_specs=pl.BlockSpec((2, half, t_lo, c), lambda q, j: (q, 0, j, 0)),
        out_shape=jax.ShapeDtypeStruct((2 * p, half, N_LO, c), F32),
        compiler_params=_params("parallel", "parallel"), name="hyena_idft",
    )(cc, s4)


def _filter_positions(s):
    t = jnp.linspace(0.0, 1.0, s, dtype=F32)[:, None]
    bands = (HYENA_EMB_DIM - 1) // 2
    w = 2.0 * math.pi * jnp.arange(s, dtype=F32)[:, None] / s
    f = jnp.linspace(1e-4, bands - 1, bands, dtype=F32)[None, :]
    fw = f * w
    return jnp.concatenate([t, jnp.cos(fw), -jnp.sin(fw)], axis=-1)


def _hyena_conv(vv, w1, b1, w2, b2, w3, b3, w4, freq):
    b, s, c = vv.shape
    n_hi = 2 * s // N_LO
    half = n_hi // 2
    assert b % 2 == 0 and n_hi * N_LO == 2 * s
    s1_pair, s1_real, gh, s4 = _dft_tables(n_hi)

    max_decay = math.log(HYENA_DECAY_TARGET) / HYENA_FAST_DECAY
    min_decay = math.log(HYENA_DECAY_TARGET) / HYENA_SLOW_DECAY
    deltas = jnp.linspace(min_decay, max_decay, c, dtype=F32)
    hf, hb = _filter_taps(_filter_positions(s), w1, b1, w2, b2, w3, b3, w4, freq, deltas)
    fa = _dft1_filter(hf.reshape(half, N_LO, c), hb.reshape(half, N_LO, c), s1_real, n_hi=n_hi)

    a = _dft1_pair(vv.reshape(b, half, N_LO, c), s1_pair, n_hi=n_hi)
    cc = _spectral(a, fa, gh, n_hi=n_hi)
    return _idft(cc, s4, n_hi=n_hi).reshape(b, s, c)


def kernel(x, rel_bias, ffn1_norm, ffn1_w_gate, ffn1_w_up, ffn1_w_down, mix_norm, w_in,
           lambda_q1, lambda_k1, lambda_q2, lambda_k2, diff_subln,
           hy_conv_w, hy_conv_b, hy_f_w1, hy_f_b1, hy_f_w2, hy_f_b2, hy_f_w3, hy_f_b3,
           hy_f_w4, hy_f_freq, hy_bias, hy_out_norm, w_out,
           ffn2_norm, ffn2_w_gate, ffn2_w_up, ffn2_w_down, final_norm):
    b, s, d = x.shape
    depth = w_in.shape[0]
    d_att = diff_subln.shape[1] * rel_bias.shape[1]
    n_heads = rel_bias.shape[1]
    attn_tile = min(512, s)
    bf = lambda a: a.astype(BF16)

    xf = x.reshape(b * s, d)
    for l in range(depth):
        last = l == depth - 1
        xf = _ffn(xf, ffn1_norm[l], ffn1_w_gate[l], ffn1_w_up[l], ffn1_w_down[l])
        qkv, vv, x0 = _mix(xf, mix_norm[l], bf(w_in[l]), hy_conv_w[l], hy_conv_b[l], d_att=d_att, seq=s)
        lambda_init = 0.8 - 0.6 * math.exp(-0.3 * l)
        att = _attention(qkv.reshape(b, s, -1), _bias_tiles(rel_bias, attn_tile),
                         lambda_q1[l], lambda_k1[l], lambda_q2[l], lambda_k2[l], diff_subln[l],
                         n_heads=n_heads, lambda_init=lambda_init, t=attn_tile, n_chain=4,
                         n_q=max(n for n in (4, 2, 1) if (s // attn_tile) % n == 0))
        y = _hyena_conv(vv.reshape(b, s, -1), hy_f_w1[l], hy_f_b1[l], hy_f_w2[l], hy_f_b2[l],
                        hy_f_w3[l], hy_f_b3[l], hy_f_w4[l], hy_f_freq[l])
        hy_parts = (y.reshape(b * s, -1), vv, x0, hy_bias[l], hy_out_norm[l])
        wo = bf(w_out[l])
        xf = _ffn(xf, ffn2_norm[l], ffn2_w_gate[l], ffn2_w_up[l], ffn2_w_down[l],
                  pre=(att.reshape(b * s, -1), wo[:d_att], wo[d_att:], hy_parts),
                  final_g=final_norm if last else None)
    if depth == 0:
        raise ValueError("depth must be positive")
    return xf.reshape(b, s, d)
```

```python
import functools
import math

import numpy as np
import jax
import jax.numpy as jnp
from jax import lax
from jax.experimental import pallas as pl
from jax.experimental.pallas import tpu as pltpu

F32 = jnp.float32
BF16 = jnp.bfloat16

RMS_EPS = 1e-6
LANES = 128
DIFF_HEAD_DIM = 64
V_HEAD_DIM = 2 * DIFF_HEAD_DIM
REL_BUCKETS = 32
REL_MAX_DIST = 128
N_HYENA_GROUPS = 8
HYENA_EMB_DIM = 33
HYENA_DECAY_TARGET = 1e-2
HYENA_FAST_DECAY = 0.3
HYENA_SLOW_DECAY = 1.5
N_LO = 128
LOG2_E = math.log2(math.e)
VMEM_LIMIT = 56 * 1024 * 1024


def _params(*sem):
    return pltpu.CompilerParams(dimension_semantics=sem, vmem_limit_bytes=VMEM_LIMIT)


def _resident(shape):
    return pl.BlockSpec(shape, lambda *_: (0,) * len(shape), pipeline_mode=pl.Buffered(1))


def _rms(x, g):
    return x * lax.rsqrt(jnp.mean(x * x, axis=-1, keepdims=True) + RMS_EPS) * g


def _dot(a, b):
    return jnp.dot(a, b, preferred_element_type=F32)


def _split(a):
    hi = a.astype(BF16)
    return hi, (a - hi.astype(F32)).astype(BF16)


def _hyena_gate_norm(y_ref, vv_ref, x0_ref, fb_ref, og_ref, mg_ref):
    vv = vv_ref[...].astype(F32)
    y = (y_ref[...] + vv * fb_ref[...]) * x0_ref[...].astype(F32)
    hi, lo = _split(y * y)
    ms = _dot(hi, mg_ref[...]) + _dot(lo, mg_ref[...])
    return y * lax.rsqrt(ms + RMS_EPS) * og_ref[...]


N_STAGE = 4


def _stage_weights(srcs, dsts, stages, sem):
    jobs = [(src, dst, stage, c) for src, dst, stage in zip(srcs, dsts, stages) for c in range(N_STAGE)]

    def copy(k):
        src, _, stage, c = jobs[k]
        rows = stage.shape[1]
        return pltpu.make_async_copy(src.at[pl.ds(c * rows, rows), :], stage.at[k % 2], sem.at[k % 2])

    copy(0).start()
    for k, (_, dst, stage, c) in enumerate(jobs):
        if k + 1 < len(jobs):
            copy(k + 1).start()
        copy(k).wait()
        rows = stage.shape[1]
        dst[pl.ds(c * rows, rows), :] = stage[k % 2].astype(dst.dtype)


def _ffn_kernel(*refs, ff_chunk, n_chunk, pre, post):
    if pre:
        x_ref, att_ref, woa_ref, woh_ref = refs[:4]
        hy_refs = refs[4:10]
        refs = refs[10:]
    else:
        x_ref = refs[0]
        refs = refs[1:]
    g_ref, wg_hbm, wu_hbm, wd_hbm = refs[:4]
    refs = refs[4:]
    if post:
        fg_ref, o_ref = refs[:2]
        refs = refs[2:]
    else:
        o_ref = refs[0]
        refs = refs[1:]
    wg_ref, wu_ref, wd_ref, stage_in, stage_out, sem = refs

    @pl.when(pl.program_id(0) == 0)
    def _():
        _stage_weights((wg_hbm, wu_hbm, wd_hbm), (wg_ref, wu_ref, wd_ref),
                       (stage_in, stage_in, stage_out), sem)

    x = x_ref[...]
    if pre:
        hy = _hyena_gate_norm(*hy_refs).astype(BF16)
        x = x + _dot(att_ref[...], woa_ref[...]) + _dot(hy, woh_ref[...])
    xn = _rms(x, g_ref[...]).astype(BF16)
    acc = jnp.zeros(x.shape, F32)
    for c in range(n_chunk):
        sl = slice(c * ff_chunk, (c + 1) * ff_chunk)
        gate = _dot(xn, wg_ref[:, sl])
        up = _dot(xn, wu_ref[:, sl])
        h = (jax.nn.silu(gate) * up).astype(BF16)
        acc = acc + _dot(h, wd_ref[sl, :])
    y = x + 0.5 * acc
    if post:
        y = _rms(y, fg_ref[...])
    o_ref[...] = y


def _ffn(x, norm_g, wg, wu, wd, *, pre=None, final_g=None, tm=512, ff_chunk=256):
    m, d = x.shape
    dff = wg.shape[1]
    n_chunk = dff // ff_chunk
    assert n_chunk * ff_chunk == dff and m % tm == 0
    assert d % (16 * N_STAGE) == 0 and dff % (16 * N_STAGE) == 0
    row = lambda w: pl.BlockSpec((tm, w), lambda i: (i, 0))
    args, specs = [x], [row(d)]
    if pre is not None:
        att, woa, woh, (y, vv, x0, filt_bias, out_g) = pre
        c = y.shape[1]
        gdim = c // N_HYENA_GROUPS
        assert gdim & (gdim - 1) == 0
        grp = np.arange(c) // gdim
        mg = jnp.asarray((grp[:, None] == grp[None, :]).astype(np.float32) / gdim).astype(BF16)
        args += [att, woa, woh, y, vv, x0, filt_bias.reshape(1, c), out_g.reshape(1, c), mg]
        specs += [row(att.shape[1]), _resident(woa.shape), _resident(woh.shape), row(c), row(c), row(c),
                  _resident((1, c)), _resident((1, c)), _resident((c, c))]
    hbm = pl.BlockSpec(memory_space=pl.ANY)
    args += [norm_g.reshape(1, d), wg, wu, wd]
    specs += [_resident((1, d)), hbm, hbm, hbm]
    if final_g is not None:
        args.append(final_g.reshape(1, d))
        specs.append(_resident((1, d)))
    kern = functools.partial(_ffn_kernel, ff_chunk=ff_chunk, n_chunk=n_chunk,
                             pre=pre is not None, post=final_g is not None)
    return pl.pallas_call(
        kern, grid=(m // tm,), in_specs=specs, out_specs=row(d),
        out_shape=jax.ShapeDtypeStruct((m, d), F32),
        scratch_shapes=[pltpu.VMEM((d, dff), BF16), pltpu.VMEM((d, dff), BF16), pltpu.VMEM((dff, d), BF16),
                        pltpu.VMEM((2, d // N_STAGE, dff), F32), pltpu.VMEM((2, dff // N_STAGE, d), F32),
                        pltpu.SemaphoreType.DMA((2,))],
        compiler_params=_params("arbitrary"), name="ffn_pre" if pre is not None else "ffn",
    )(*args)


HALO = 8


def _mix_kernel(x_ref, xp_ref, xn_ref, g_ref, w_ref, cw_ref, cb_ref, qkv_ref, vv_ref, x0_ref,
                *, d_att, c, scale, tiles_per_seq):
    i = pl.program_id(0)
    n_att = 3 * d_att
    xn = _rms(x_ref[...], g_ref[...]).astype(BF16)
    for c0 in range(0, n_att, d_att):
        p = _dot(xn, w_ref[:, c0:c0 + d_att])
        if c0 == 0:
            p = p * scale
        qkv_ref[:, c0:c0 + d_att] = p.astype(BF16)

    halo = _rms(jnp.concatenate([xp_ref[...], xn_ref[...]], axis=0), g_ref[...]).astype(BF16)
    first = i % tiles_per_seq == 0
    last = i % tiles_per_seq == tiles_per_seq - 1
    rows = x_ref.shape[0]
    r = lax.broadcasted_iota(jnp.int32, (HALO, c), 0)
    head, tail = slice(0, HALO), slice(rows - HALO, rows)
    u = []
    for part in range(3):
        cs = slice(n_att + part * c, n_att + (part + 1) * c)
        ws = slice(part * c, (part + 1) * c)
        z = _dot(xn, w_ref[:, cs])
        zh = _dot(halo, w_ref[:, cs])
        prev_row = jnp.where(first, 0.0, zh[HALO - 1:HALO])
        next_row = jnp.where(last, 0.0, zh[HALO:HALO + 1])
        z_m1 = pltpu.roll(z, 1, 0)
        z_p1 = pltpu.roll(z, rows - 1, 0)
        conv = lambda zm, zc, zp: (cb_ref[:, ws] + zm * cw_ref[0:1, ws] + zc * cw_ref[1:2, ws]
                                   + zp * cw_ref[2:3, ws])
        mid = conv(z_m1, z, z_p1)
        top = conv(jnp.where(r == 0, prev_row, z_m1[head]), z[head], z_p1[head])
        bot = conv(z_m1[tail], z[tail], jnp.where(r == HALO - 1, next_row, z_p1[tail]))
        u.append(jnp.concatenate([top, mid[HALO:rows - HALO], bot], axis=0))
    x0_ref[...] = u[0].astype(x0_ref.dtype)
    vv_ref[...] = (u[2] * u[1]).astype(vv_ref.dtype)


def _mix(x, norm_g, w_in, conv_w, conv_b, *, d_att, seq, tm=512):
    m, d = x.shape
    n_att = 3 * d_att
    c = (w_in.shape[1] - n_att) // 3
    assert m % tm == 0 and seq % tm == 0 and tm % HALO == 0
    kern = functools.partial(_mix_kernel, d_att=d_att, c=c, scale=DIFF_HEAD_DIM ** -0.5 * LOG2_E,
                             tiles_per_seq=seq // tm)
    sub = tm // HALO
    last_blk = m // HALO - 1
    row = lambda w: pl.BlockSpec((tm, w), lambda i: (i, 0))
    return pl.pallas_call(
        kern, grid=(m // tm,),
        in_specs=[row(d),
                  pl.BlockSpec((HALO, d), lambda i: (jnp.maximum(i * sub - 1, 0), 0)),
                  pl.BlockSpec((HALO, d), lambda i: (jnp.minimum((i + 1) * sub, last_blk), 0)),
                  _resident((1, d)), _resident(w_in.shape), _resident(conv_w.shape), _resident((1, 3 * c))],
        out_specs=[row(n_att), row(c), row(c)],
        out_shape=[jax.ShapeDtypeStruct((m, n_att), BF16), jax.ShapeDtypeStruct((m, c), BF16),
                   jax.ShapeDtypeStruct((m, c), BF16)],
        compiler_params=_params("parallel"), name="mix",
    )(x, x, x, norm_g.reshape(1, d), w_in, conv_w, conv_b.reshape(1, 3 * c))


N_BIAS_TILES = 5


def _bias_kernel(tab_ref, o_ref, *, t):
    h = pl.program_id(0)
    d = pl.program_id(1) - N_BIAS_TILES // 2
    half = REL_BUCKETS // 2
    max_exact = half // 2
    rel = lax.broadcasted_iota(jnp.int32, (8, 2 * t), 1) + (d - 1) * t
    ret = jnp.where(rel > 0, half, 0)
    n = jnp.abs(rel)
    nf = jnp.maximum(n, 1).astype(F32)
    large = max_exact + (jnp.log(nf / max_exact) / math.log(REL_MAX_DIST / max_exact)
                         * (half - max_exact)).astype(jnp.int32)
    large = jnp.minimum(large, half - 1)
    bucket = ret + jnp.where(n < max_exact, n, large)
    row = jnp.zeros(rel.shape, F32)
    for b in range(REL_BUCKETS):
        row = jnp.where(bucket == b, tab_ref[b, h], row)
    rows = jnp.broadcast_to(row[0:1, :] * LOG2_E, (t, 2 * t))
    o_ref[...] = pltpu.roll(rows, 0, 1, stride=1, stride_axis=0)[:, t:2 * t]


def _bias_tiles(rel_bias, t):
    nh = rel_bias.shape[1]
    return pl.pallas_call(
        functools.partial(_bias_kernel, t=t), grid=(nh, N_BIAS_TILES),
        in_specs=[pl.BlockSpec(memory_space=pltpu.SMEM)],
        out_specs=pl.BlockSpec((None, None, t, t), lambda h, d: (h, d, 0, 0)),
        out_shape=jax.ShapeDtypeStruct((nh, N_BIAS_TILES, t, t), F32),
        compiler_params=_params("parallel", "parallel"), name="rel_bias_tiles",
    )(rel_bias)


def _attn_kernel(q_ref, k_ref, v_ref, bt_ref, lq1_ref, lk1_ref, lq2_ref, lk2_ref, sg_ref,
                 o_ref, q2_ref, vx_ref, *, t, nk, n_q, n_chain, lambda_init):
    g = pl.program_id(2)
    dv = v_ref.shape[1]

    @pl.when(g == 0)
    def _():
        vx_ref[:, 0:dv] = v_ref[...]
        ones_lane = lax.broadcasted_iota(jnp.int32, (v_ref.shape[0], dv), 1) == 0
        vx_ref[:, dv:2 * dv] = jnp.where(ones_lane, 1.0, 0.0).astype(vx_ref.dtype)

    far = N_BIAS_TILES // 2
    rc = 2 * t // n_chain
    lam = (jnp.exp(jnp.sum(lq1_ref[...] * lk1_ref[...])) - jnp.exp(jnp.sum(lq2_ref[...] * lk2_ref[...]))
           + lambda_init)

    def scores(i, base, d, rows, n_keys=t):
        wrapped = i + d >= nk
        j = jnp.where(wrapped, i + d - nk, i + d)
        k = k_ref[pl.ds(pl.multiple_of(j * t, t), n_keys), :]
        q2 = q2_ref[base + rows.start:base + rows.stop, :]
        s = lax.dot_general(q2, k, (((1,), (1,)), ((), ())), preferred_element_type=F32)
        cols = [s[:, c0:c0 + LANES] for c0 in range(0, n_keys, LANES)]
        if d <= 1 or d >= nk - 1:
            qr = slice(rows.start % t, rows.start % t + rows.stop - rows.start)
            bias = bt_ref[jnp.clip(j - i, -far, far) + far, qr, :]
            return j, [c + bias[:, n * LANES:(n + 1) * LANES] for n, c in enumerate(cols)], None
        side = jnp.where(wrapped, bt_ref[0, 0:1, 0:LANES], bt_ref[2 * far, 0:1, 0:LANES])
        return j, cols, side

    def tile(i, base, d, rows, m_prev, acc_prev):
        j, cols, side = scores(i, base, d, rows)
        shift = m_prev if side is None else m_prev - side
        rel = [(col - shift).astype(BF16) for col in cols]
        rm = jnp.max(functools.reduce(jnp.maximum, rel), axis=1, keepdims=True)
        delta = jnp.maximum(rm, 0.0)
        p = jnp.concatenate([jnp.exp2(x - delta) for x in rel], axis=1)
        d32 = jnp.broadcast_to(delta.astype(F32), m_prev.shape)
        alpha = jnp.exp2(-d32)
        pv = _dot(p, vx_ref[pl.ds(pl.multiple_of(j * t, t), t), :])
        acc_new = jnp.concatenate([alpha * acc_prev[:, 0:dv] + pv[:, 0:dv],
                                   alpha * acc_prev[:, dv:2 * dv] + pv[:, dv:2 * dv]], axis=1)
        return m_prev + d32, acc_new

    streams = []
    for a in range(n_q):
        i, base = g * n_q + a, a * 2 * t
        q = q_ref[a * t:(a + 1) * t, :]
        lane = lax.broadcasted_iota(jnp.int32, q.shape, 1)
        zero = jnp.zeros_like(q)
        q2_ref[base:base + t, :] = jnp.where(lane < DIFF_HEAD_DIM, q, zero)
        q2_ref[base + t:base + 2 * t, :] = jnp.where(lane >= DIFF_HEAD_DIM, q, zero)
        for r0 in range(0, 2 * t, rc):
            streams.append((i, base, slice(r0, r0 + rc)))
    state = []
    for i, base, rows in streams:
        m0 = jnp.max(scores(i, base, 0, rows, LANES)[1][0], axis=1, keepdims=True)
        state.append((jnp.broadcast_to(m0, (rc, LANES)), jnp.zeros((rc, 2 * dv), F32)))
    for d in range(nk):
        state = [tile(i, base, d, rows, *st) for (i, base, rows), st in zip(streams, state)]

    for a in range(n_q):
        acc = jnp.concatenate([acc for _, acc in state[a * n_chain:(a + 1) * n_chain]], axis=0)
        o = acc[:, 0:dv] / acc[:, dv:dv + 1]
        o = o[0:t] - lam * o[t:2 * t]
        o = _rms(o, sg_ref[...]) * (1.0 - lambda_init)
        o_ref[a * t:(a + 1) * t, :] = o.astype(o_ref.dtype)


def _attention(qkv, bias_tiles, lq1, lk1, lq2, lk2, subln, *, n_heads, lambda_init, t, n_chain, n_q):
    b, s, _ = qkv.shape
    nk = s // t
    assert nk * t == s and t % LANES == 0 and t + 1 >= 91 and (2 * t) % n_chain == 0 and nk % n_q == 0
    kern = functools.partial(_attn_kernel, t=t, nk=nk, n_q=n_q, n_chain=n_chain, lambda_init=lambda_init)
    vec = lambda n: pl.BlockSpec((1, n), lambda b_, h, i: (0, 0))
    return pl.pallas_call(
        kern, grid=(b, n_heads, nk // n_q),
        in_specs=[
            pl.BlockSpec((None, n_q * t, V_HEAD_DIM), lambda b_, h, i: (b_, i, h)),
            pl.BlockSpec((None, s, V_HEAD_DIM), lambda b_, h, i: (b_, 0, n_heads + h)),
            pl.BlockSpec((None, s, V_HEAD_DIM), lambda b_, h, i: (b_, 0, 2 * n_heads + h)),
            pl.BlockSpec((None, N_BIAS_TILES, t, t), lambda b_, h, i: (h, 0, 0, 0)),
            vec(DIFF_HEAD_DIM), vec(DIFF_HEAD_DIM), vec(DIFF_HEAD_DIM), vec(DIFF_HEAD_DIM),
            vec(V_HEAD_DIM),
        ],
        out_specs=pl.BlockSpec((None, n_q * t, V_HEAD_DIM), lambda b_, h, i: (b_, i, h)),
        out_shape=jax.ShapeDtypeStruct((b, s, n_heads * V_HEAD_DIM), BF16),
        scratch_shapes=[pltpu.VMEM((n_q * 2 * t, V_HEAD_DIM), BF16),
                        pltpu.VMEM((s, 2 * V_HEAD_DIM), BF16)],
        compiler_params=_params("parallel", "parallel", "arbitrary"), name="diff_attention",
    )(qkv, qkv, qkv, bias_tiles, lq1.reshape(1, -1), lk1.reshape(1, -1),
      lq2.reshape(1, -1), lk2.reshape(1, -1), subln.reshape(1, -1))


def _dft_tables(n_hi):
    n = n_hi * N_LO
    half = n_hi // 2
    hi = np.arange(n_hi)
    lo = np.arange(N_LO)
    f_hi = np.exp(-2j * np.pi * ((np.outer(hi, hi) % n_hi) / n_hi))
    fr, fi = f_hi.real, f_hi.imag
    s1_pair = np.block([[fr[:, :half], -fi[:, :half]], [fi[:, :half], fr[:, :half]]])
    s1_real = np.concatenate([fr[:, :half], fi[:, :half]], axis=0)
    ph = (hi[:, None, None] * lo[None, None, :] + n_hi * lo[None, :, None] * lo[None, None, :]) % n
    g = np.exp(-2j * np.pi * ph / n)
    g2 = np.concatenate([np.concatenate([g.real, -g.imag], axis=2),
                         np.concatenate([g.imag, g.real], axis=2)], axis=1)
    gh = np.stack([g2, np.swapaxes(g2, 1, 2)], axis=1)
    s4 = np.block([[fr[:half], fi[:half]], [-fi[:half], fr[:half]]]) / n
    cast = lambda a: jnp.asarray(a.astype(np.float32)).astype(BF16)
    return cast(s1_pair), cast(s1_real), cast(gh), cast(s4)


def _hdot(a, b, dims=(((1,), (0,)), ((), ()))):
    (ah, al), (bh, bl) = _split(a), _split(b)
    dg = functools.partial(lax.dot_general, dimension_numbers=dims, preferred_element_type=F32)
    return dg(ah, bh) + dg(ah, bl) + dg(al, bh)


def _filter_kernel(zt_ref, t_ref, w1_ref, b1_ref, w2_ref, b2_ref, w3_ref, b3_ref, w4_ref, fr_ref, dl_ref,
                   hf_ref, hb_ref, *, c):
    i = pl.program_id(0)
    fr = fr_ref[...]
    a = jnp.sin(fr * (_hdot(w1_ref[...], zt_ref[...]) + b1_ref[...]))
    a = jnp.sin(fr * (_hdot(w2_ref[...], a) + b2_ref[...]))
    a = jnp.sin(fr * (_hdot(w3_ref[...], a) + b3_ref[...]))
    hh = _hdot(a, w4_ref[...], (((0,), (0,)), ((), ())))
    decay = jnp.exp(-t_ref[...] * jnp.abs(dl_ref[...]))
    hf_ref[...] = (hh[:, 0:c] * decay).astype(hf_ref.dtype)
    r = lax.broadcasted_iota(jnp.int32, (hh.shape[0], c), 0)
    lag0 = jnp.logical_and(i == 0, r == 0)
    hb_ref[...] = jnp.where(lag0, 0.0, hh[:, c:2 * c] * decay).astype(hb_ref.dtype)


def _filter_taps(z, w1, b1, w2, b2, w3, b3, w4, freq, deltas, *, tb=512):
    s, emb = z.shape
    c = deltas.shape[0]
    assert s % tb == 0
    col = lambda v: v.reshape(-1, 1)
    args = [jnp.pad(z, ((0, 0), (0, LANES - emb))).T, z[:, 0:1],
            jnp.pad(w1, ((0, LANES - emb), (0, 0))).T, col(b1), w2.T, col(b2), w3.T, col(b3), w4,
            col(freq), deltas.reshape(1, c)]
    specs = [pl.BlockSpec((LANES, tb), lambda i: (0, i)), pl.BlockSpec((tb, 1), lambda i: (i, 0))] + [
        pl.BlockSpec(a.shape, lambda i: (0, 0)) for a in args[2:]]
    out = pl.BlockSpec((tb, c), lambda i: (i, 0))
    shp = jax.ShapeDtypeStruct((s, c), BF16)
    return pl.pallas_call(
        functools.partial(_filter_kernel, c=c), grid=(s // tb,),
        in_specs=specs, out_specs=[out, out], out_shape=[shp, shp],
        compiler_params=_params("parallel"), name="hyena_filter_taps",
    )(*args)


def _lo_major(ref):
    return jnp.swapaxes(ref[...], 0, 1)


def _dft1_kernel(x0_ref, x1_ref, f_ref, a_ref):
    n_out = a_ref.shape[0]
    x0, x1 = _lo_major(x0_ref), _lo_major(x1_ref)
    a = jnp.stack([_dot(f_ref[...], jnp.concatenate([x0[j], x1[j]], axis=0)).astype(a_ref.dtype)
                   for j in range(x0.shape[0])], axis=0)
    a = jnp.swapaxes(a, 0, 1)
    a_ref[:, 0] = a[0:n_out]
    a_ref[:, 1] = a[n_out:2 * n_out]


def _dft1_pair(xv, f_mat, *, n_hi, t_lo=16):
    b, half, _, c = xv.shape
    spec = lambda off: pl.BlockSpec((None, half, t_lo, c), lambda p, j: (2 * p + off, 0, j, 0))
    return pl.pallas_call(
        _dft1_kernel, grid=(b // 2, N_LO // t_lo),
        in_specs=[spec(0), spec(1), pl.BlockSpec(f_mat.shape, lambda p, j: (0, 0))],
        out_specs=pl.BlockSpec((None, n_hi, 2, t_lo, c), lambda p, j: (p, 0, 0, j, 0)),
        out_shape=jax.ShapeDtypeStruct((b // 2, n_hi, 2, N_LO, c), BF16),
        compiler_params=_params("parallel", "parallel"), name="hyena_dft1",
    )(xv, xv, f_mat)


def _dft1_filter_kernel(hf_ref, hb_ref, f_ref, fa_ref):
    n_out = fa_ref.shape[0]
    c = hf_ref.shape[2]
    hf, hb = _lo_major(hf_ref), _lo_major(hb_ref)
    a = jnp.stack([_dot(f_ref[...], jnp.concatenate([hf[j], hb[j]], axis=1)).astype(fa_ref.dtype)
                   for j in range(hf.shape[0])], axis=0)
    a = jnp.swapaxes(a, 0, 1)
    for part, (r0, c0) in enumerate(((0, 0), (n_out, 0), (0, c), (n_out, c))):
        fa_ref[:, part] = a[r0:r0 + n_out, :, c0:c0 + c]


def _dft1_filter(hf, hb, f_mat, *, n_hi, t_lo=16):
    half, _, c = hf.shape
    blk = pl.BlockSpec((half, t_lo, c), lambda j: (0, j, 0))
    return pl.pallas_call(
        _dft1_filter_kernel, grid=(N_LO // t_lo,),
        in_specs=[blk, blk, pl.BlockSpec(f_mat.shape, lambda j: (0, 0))],
        out_specs=pl.BlockSpec((n_hi, 4, t_lo, c), lambda j: (0, 0, j, 0)),
        out_shape=jax.ShapeDtypeStruct((n_hi, 4, N_LO, c), BF16),
        compiler_params=_params("parallel"), name="hyena_filter_dft1",
    )(hf, hb, f_mat)


def _spectral_kernel(a_ref, fa_ref, gh_ref, cc_ref):
    c = a_ref.shape[-1]
    for u in range(a_ref.shape[0]):
        a = jnp.concatenate([a_ref[u].reshape(2 * N_LO, c), fa_ref[u, 0:2].reshape(2 * N_LO, c),
                             fa_ref[u, 2:4].reshape(2 * N_LO, c)], axis=1)
        x = _dot(gh_ref[u, 0], a)
        xr, xi = x[0:N_LO, 0:c], x[N_LO:2 * N_LO, 0:c]
        kr = x[0:N_LO, c:2 * c] + x[0:N_LO, 2 * c:3 * c]
        ki = x[N_LO:2 * N_LO, c:2 * c] - x[N_LO:2 * N_LO, 2 * c:3 * c]
        y = jnp.concatenate([xr * kr - xi * ki, xr * ki + xi * kr], axis=0).astype(BF16)
        cc_ref[u] = _dot(gh_ref[u, 1], y).astype(cc_ref.dtype).reshape(2, N_LO, c)


def _spectral(a, fa, gh, *, n_hi, kb=8):
    p, _, _, _, c = a.shape
    assert n_hi % kb == 0
    blk = pl.BlockSpec((None, kb, 2, N_LO, c), lambda q, k: (q, k, 0, 0, 0))
    return pl.pallas_call(
        _spectral_kernel, grid=(p, n_hi // kb),
        in_specs=[blk, pl.BlockSpec((kb, 4, N_LO, c), lambda q, k: (k, 0, 0, 0)),
                  pl.BlockSpec((kb, 2, 2 * N_LO, 2 * N_LO), lambda q, k: (k, 0, 0, 0))],
        out_specs=blk, out_shape=jax.ShapeDtypeStruct(a.shape, BF16),
        compiler_params=_params("parallel", "parallel"), name="hyena_spectral",
    )(a, fa, gh)


def _idft_kernel(cc_ref, m_ref, o_ref):
    half = o_ref.shape[1]
    cr, ci = jnp.swapaxes(cc_ref[:, 0], 0, 1), jnp.swapaxes(cc_ref[:, 1], 0, 1)
    y = jnp.stack([_dot(m_ref[...], jnp.concatenate([cr[j], ci[j]], axis=0)) for j in range(cr.shape[0])],
                  axis=0)
    y = jnp.swapaxes(y, 0, 1)
    o_ref[0] = y[0:half]
    o_ref[1] = y[half:2 * half]


def _idft(cc, s4, *, n_hi, t_lo=16):
    p, _, _, _, c = cc.shape
    half = n_hi // 2
    return pl.pallas_call(
        _idft_kernel, grid=(p, N_LO // t_lo),
        in_specs=[pl.BlockSpec((None, n_hi, 2, t_lo, c), lambda q, j: (q, 0, 0, j, 0)),
                  pl.BlockSpec(s4.shape, lambda q, j: (0, 0))],
        out_specs=pl.BlockSpec((2, half, t_lo, c), lambda q, j: (q, 0, j, 0)),
        out_shape=jax.ShapeDtypeStruct((2 * p, half, N_LO, c), F32),
        compiler_params=_params("parallel", "parallel"), name="hyena_idft",
    )(cc, s4)


def _filter_positions(s):
    t = jnp.linspace(0.0, 1.0, s, dtype=F32)[:, None]
    bands = (HYENA_EMB_DIM - 1) // 2
    w = 2.0 * math.pi * jnp.arange(s, dtype=F32)[:, None] / s
    f = jnp.linspace(1e-4, bands - 1, bands, dtype=F32)[None, :]
    fw = f * w
    return jnp.concatenate([t, jnp.cos(fw), -jnp.sin(fw)], axis=-1)


def _hyena_conv(vv, w1, b1, w2, b2, w3, b3, w4, freq):
    b, s, c = vv.shape
    n_hi = 2 * s // N_LO
    half = n_hi // 2
    assert b % 2 == 0 and n_hi * N_LO == 2 * s
    s1_pair, s1_real, gh, s4 = _dft_tables(n_hi)

    max_decay = math.log(HYENA_DECAY_TARGET) / HYENA_FAST_DECAY
    min_decay = math.log(HYENA_DECAY_TARGET) / HYENA_SLOW_DECAY
    deltas = jnp.linspace(min_decay, max_decay, c, dtype=F32)
    hf, hb = _filter_taps(_filter_positions(s), w1, b1, w2, b2, w3, b3, w4, freq, deltas)
    fa = _dft1_filter(hf.reshape(half, N_LO, c), hb.reshape(half, N_LO, c), s1_real, n_hi=n_hi)

    a = _dft1_pair(vv.reshape(b, half, N_LO, c), s1_pair, n_hi=n_hi)
    cc = _spectral(a, fa, gh, n_hi=n_hi)
    return _idft(cc, s4, n_hi=n_hi).reshape(b, s, c)


def kernel(x, rel_bias, ffn1_norm, ffn1_w_gate, ffn1_w_up, ffn1_w_down, mix_norm, w_in,
           lambda_q1, lambda_k1, lambda_q2, lambda_k2, diff_subln,
           hy_conv_w, hy_conv_b, hy_f_w1, hy_f_b1, hy_f_w2, hy_f_b2, hy_f_w3, hy_f_b3,
           hy_f_w4, hy_f_freq, hy_bias, hy_out_norm, w_out,
           ffn2_norm, ffn2_w_gate, ffn2_w_up, ffn2_w_down, final_norm):
    b, s, d = x.shape
    depth = w_in.shape[0]
    d_att = diff_subln.shape[1] * rel_bias.shape[1]
    n_heads = rel_bias.shape[1]
    attn_tile = min(512, s)
    bf = lambda a: a.astype(BF16)

    xf = x.reshape(b * s, d)
    for l in range(depth):
        last = l == depth - 1
        xf = _ffn(xf, ffn1_norm[l], ffn1_w_gate[l], ffn1_w_up[l], ffn1_w_down[l])
        qkv, vv, x0 = _mix(xf, mix_norm[l], bf(w_in[l]), hy_conv_w[l], hy_conv_b[l], d_att=d_att, seq=s)
        lambda_init = 0.8 - 0.6 * math.exp(-0.3 * l)
        att = _attention(qkv.reshape(b, s, -1), _bias_tiles(rel_bias, attn_tile),
                         lambda_q1[l], lambda_k1[l], lambda_q2[l], lambda_k2[l], diff_subln[l],
                         n_heads=n_heads, lambda_init=lambda_init, t=attn_tile, n_chain=4,
                         n_q=2 if (s // attn_tile) % 2 == 0 else 1)
        y = _hyena_conv(vv.reshape(b, s, -1), hy_f_w1[l], hy_f_b1[l], hy_f_w2[l], hy_f_b2[l],
                        hy_f_w3[l], hy_f_b3[l], hy_f_w4[l], hy_f_freq[l])
        hy_parts = (y.reshape(b * s, -1), vv, x0, hy_bias[l], hy_out_norm[l])
        wo = bf(w_out[l])
        xf = _ffn(xf, ffn2_norm[l], ffn2_w_gate[l], ffn2_w_up[l], ffn2_w_down[l],
                  pre=(att.reshape(b * s, -1), wo[:d_att], wo[d_att:], hy_parts),
                  final_g=final_norm if last else None)
    if depth == 0:
        raise ValueError("depth must be positive")
    return xf.reshape(b, s, d)
```

```python
import functools
import math

import numpy as np
import jax
import jax.numpy as jnp
from jax import lax
from jax.experimental import pallas as pl
from jax.experimental.pallas import tpu as pltpu

F32 = jnp.float32
BF16 = jnp.bfloat16

RMS_EPS = 1e-6
LANES = 128
DIFF_HEAD_DIM = 64
V_HEAD_DIM = 2 * DIFF_HEAD_DIM
REL_BUCKETS = 32
REL_MAX_DIST = 128
N_HYENA_GROUPS = 8
HYENA_EMB_DIM = 33
HYENA_DECAY_TARGET = 1e-2
HYENA_FAST_DECAY = 0.3
HYENA_SLOW_DECAY = 1.5
N_LO = 128
LOG2_E = math.log2(math.e)
VMEM_LIMIT = 56 * 1024 * 1024


def _params(*sem):
    return pltpu.CompilerParams(dimension_semantics=sem, vmem_limit_bytes=VMEM_LIMIT)


def _resident(shape):
    return pl.BlockSpec(shape, lambda *_: (0,) * len(shape), pipeline_mode=pl.Buffered(1))


def _rms(x, g):
    return x * lax.rsqrt(jnp.mean(x * x, axis=-1, keepdims=True) + RMS_EPS) * g


def _dot(a, b):
    return jnp.dot(a, b, preferred_element_type=F32)


def _split(a):
    hi = a.astype(BF16)
    return hi, (a - hi.astype(F32)).astype(BF16)


def _hyena_gate_norm(y_ref, vv_ref, x0_ref, fb_ref, og_ref, mg_ref):
    vv = vv_ref[...].astype(F32)
    y = (y_ref[...] + vv * fb_ref[...]) * x0_ref[...].astype(F32)
    hi, lo = _split(y * y)
    ms = _dot(hi, mg_ref[...]) + _dot(lo, mg_ref[...])
    return y * lax.rsqrt(ms + RMS_EPS) * og_ref[...]


N_STAGE = 4


def _stage_weights(srcs, dsts, stages, sem):
    jobs = [(src, dst, stage, c) for src, dst, stage in zip(srcs, dsts, stages) for c in range(N_STAGE)]

    def copy(k):
        src, _, stage, c = jobs[k]
        rows = stage.shape[1]
        return pltpu.make_async_copy(src.at[pl.ds(c * rows, rows), :], stage.at[k % 2], sem.at[k % 2])

    copy(0).start()
    for k, (_, dst, stage, c) in enumerate(jobs):
        if k + 1 < len(jobs):
            copy(k + 1).start()
        copy(k).wait()
        rows = stage.shape[1]
        dst[pl.ds(c * rows, rows), :] = stage[k % 2].astype(dst.dtype)


def _ffn_kernel(*refs, ff_chunk, n_chunk, pre, post):
    if pre:
        x_ref, att_ref, woa_ref, woh_ref = refs[:4]
        hy_refs = refs[4:10]
        refs = refs[10:]
    else:
        x_ref = refs[0]
        refs = refs[1:]
    g_ref, wg_hbm, wu_hbm, wd_hbm = refs[:4]
    refs = refs[4:]
    if post:
        fg_ref, o_ref = refs[:2]
        refs = refs[2:]
    else:
        o_ref = refs[0]
        refs = refs[1:]
    wg_ref, wu_ref, wd_ref, stage_in, stage_out, sem = refs

    @pl.when(pl.program_id(0) == 0)
    def _():
        _stage_weights((wg_hbm, wu_hbm, wd_hbm), (wg_ref, wu_ref, wd_ref),
                       (stage_in, stage_in, stage_out), sem)

    x = x_ref[...]
    if pre:
        hy = _hyena_gate_norm(*hy_refs).astype(BF16)
        x = x + _dot(att_ref[...], woa_ref[...]) + _dot(hy, woh_ref[...])
    xn = _rms(x, g_ref[...]).astype(BF16)
    acc = jnp.zeros(x.shape, F32)
    for c in range(n_chunk):
        sl = slice(c * ff_chunk, (c + 1) * ff_chunk)
        gate = _dot(xn, wg_ref[:, sl])
        up = _dot(xn, wu_ref[:, sl])
        h = (jax.nn.silu(gate) * up).astype(BF16)
        acc = acc + _dot(h, wd_ref[sl, :])
    y = x + 0.5 * acc
    if post:
        y = _rms(y, fg_ref[...])
    o_ref[...] = y


def _ffn(x, norm_g, wg, wu, wd, *, pre=None, final_g=None, tm=512, ff_chunk=256):
    m, d = x.shape
    dff = wg.shape[1]
    n_chunk = dff // ff_chunk
    assert n_chunk * ff_chunk == dff and m % tm == 0
    assert d % (16 * N_STAGE) == 0 and dff % (16 * N_STAGE) == 0
    row = lambda w: pl.BlockSpec((tm, w), lambda i: (i, 0))
    args, specs = [x], [row(d)]
    if pre is not None:
        att, woa, woh, (y, vv, x0, filt_bias, out_g) = pre
        c = y.shape[1]
        gdim = c // N_HYENA_GROUPS
        assert gdim & (gdim - 1) == 0
        grp = np.arange(c) // gdim
        mg = jnp.asarray((grp[:, None] == grp[None, :]).astype(np.float32) / gdim).astype(BF16)
        args += [att, woa, woh, y, vv, x0, filt_bias.reshape(1, c), out_g.reshape(1, c), mg]
        specs += [row(att.shape[1]), _resident(woa.shape), _resident(woh.shape), row(c), row(c), row(c),
                  _resident((1, c)), _resident((1, c)), _resident((c, c))]
    hbm = pl.BlockSpec(memory_space=pl.ANY)
    args += [norm_g.reshape(1, d), wg, wu, wd]
    specs += [_resident((1, d)), hbm, hbm, hbm]
    if final_g is not None:
        args.append(final_g.reshape(1, d))
        specs.append(_resident((1, d)))
    kern = functools.partial(_ffn_kernel, ff_chunk=ff_chunk, n_chunk=n_chunk,
                             pre=pre is not None, post=final_g is not None)
    return pl.pallas_call(
        kern, grid=(m // tm,), in_specs=specs, out_specs=row(d),
        out_shape=jax.ShapeDtypeStruct((m, d), F32),
        scratch_shapes=[pltpu.VMEM((d, dff), BF16), pltpu.VMEM((d, dff), BF16), pltpu.VMEM((dff, d), BF16),
                        pltpu.VMEM((2, d // N_STAGE, dff), F32), pltpu.VMEM((2, dff // N_STAGE, d), F32),
                        pltpu.SemaphoreType.DMA((2,))],
        compiler_params=_params("arbitrary"), name="ffn_pre" if pre is not None else "ffn",
    )(*args)


HALO = 8


def _mix_kernel(x_ref, xp_ref, xn_ref, g_ref, w_ref, cw_ref, cb_ref, qkv_ref, vv_ref, x0_ref,
                *, d_att, c, scale, tiles_per_seq, hy_chunk):
    i = pl.program_id(0)
    n_att = 3 * d_att
    xn = _rms(x_ref[...], g_ref[...]).astype(BF16)
    halo = _rms(jnp.concatenate([xp_ref[...], xn_ref[...]], axis=0), g_ref[...]).astype(BF16)
    first = i % tiles_per_seq == 0
    last = i % tiles_per_seq == tiles_per_seq - 1
    rows = x_ref.shape[0]
    cw = hy_chunk
    r = lax.broadcasted_iota(jnp.int32, (HALO, cw), 0)
    head, tail = slice(0, HALO), slice(rows - HALO, rows)

    def conv_cols(part, k0):
        cs = slice(n_att + part * c + k0, n_att + part * c + k0 + cw)
        ws = slice(part * c + k0, part * c + k0 + cw)
        z = _dot(xn, w_ref[:, cs])
        zh = _dot(halo, w_ref[:, cs])
        prev_row = jnp.where(first, 0.0, zh[HALO - 1:HALO])
        next_row = jnp.where(last, 0.0, zh[HALO:HALO + 1])
        z_m1 = pltpu.roll(z, 1, 0)
        z_p1 = pltpu.roll(z, rows - 1, 0)
        conv = lambda zm, zc, zp: (cb_ref[:, ws] + zm * cw_ref[0:1, ws] + zc * cw_ref[1:2, ws]
                                   + zp * cw_ref[2:3, ws])
        mid = conv(z_m1, z, z_p1)
        top = conv(jnp.where(r == 0, prev_row, z_m1[head]), z[head], z_p1[head])
        bot = conv(z_m1[tail], z[tail], jnp.where(r == HALO - 1, next_row, z_p1[tail]))
        return jnp.concatenate([top, mid[HALO:rows - HALO], bot], axis=0)

    for k0 in range(0, c, cw):
        vv_ref[:, k0:k0 + cw] = (conv_cols(2, k0) * conv_cols(1, k0)).astype(vv_ref.dtype)
        x0_ref[:, k0:k0 + cw] = conv_cols(0, k0).astype(x0_ref.dtype)

    for c0 in range(0, n_att, d_att):
        p = _dot(xn, w_ref[:, c0:c0 + d_att])
        if c0 == 0:
            p = p * scale
        qkv_ref[:, c0:c0 + d_att] = p.astype(BF16)


def _mix(x, norm_g, w_in, conv_w, conv_b, *, d_att, seq, tm=512, hy_chunk=256):
    m, d = x.shape
    n_att = 3 * d_att
    c = (w_in.shape[1] - n_att) // 3
    assert m % tm == 0 and seq % tm == 0 and tm % HALO == 0
    kern = functools.partial(_mix_kernel, d_att=d_att, c=c, scale=DIFF_HEAD_DIM ** -0.5 * LOG2_E,
                             tiles_per_seq=seq // tm, hy_chunk=hy_chunk)
    sub = tm // HALO
    last_blk = m // HALO - 1
    row = lambda w: pl.BlockSpec((tm, w), lambda i: (i, 0))
    return pl.pallas_call(
        kern, grid=(m // tm,),
        in_specs=[row(d),
                  pl.BlockSpec((HALO, d), lambda i: (jnp.maximum(i * sub - 1, 0), 0)),
                  pl.BlockSpec((HALO, d), lambda i: (jnp.minimum((i + 1) * sub, last_blk), 0)),
                  _resident((1, d)), _resident(w_in.shape), _resident(conv_w.shape), _resident((1, 3 * c))],
        out_specs=[row(n_att), row(c), row(c)],
        out_shape=[jax.ShapeDtypeStruct((m, n_att), BF16), jax.ShapeDtypeStruct((m, c), BF16),
                   jax.ShapeDtypeStruct((m, c), BF16)],
        compiler_params=_params("parallel"), name="mix",
    )(x, x, x, norm_g.reshape(1, d), w_in, conv_w, conv_b.reshape(1, 3 * c))


N_BIAS_TILES = 5


def _bias_kernel(tab_ref, o_ref, *, t):
    h = pl.program_id(0)
    d = pl.program_id(1) - N_BIAS_TILES // 2
    half = REL_BUCKETS // 2
    max_exact = half // 2
    rel = lax.broadcasted_iota(jnp.int32, (8, 2 * t), 1) + (d - 1) * t
    ret = jnp.where(rel > 0, half, 0)
    n = jnp.abs(rel)
    nf = jnp.maximum(n, 1).astype(F32)
    large = max_exact + (jnp.log(nf / max_exact) / math.log(REL_MAX_DIST / max_exact)
                         * (half - max_exact)).astype(jnp.int32)
    large = jnp.minimum(large, half - 1)
    bucket = ret + jnp.where(n < max_exact, n, large)
    row = jnp.zeros(rel.shape, F32)
    for b in range(REL_BUCKETS):
        row = jnp.where(bucket == b, tab_ref[b, h], row)
    rows = jnp.broadcast_to(row[0:1, :] * LOG2_E, (t, 2 * t))
    o_ref[...] = pltpu.roll(rows, 0, 1, stride=1, stride_axis=0)[:, t:2 * t]


def _bias_tiles(rel_bias, t):
    nh = rel_bias.shape[1]
    return pl.pallas_call(
        functools.partial(_bias_kernel, t=t), grid=(nh, N_BIAS_TILES),
        in_specs=[pl.BlockSpec(memory_space=pltpu.SMEM)],
        out_specs=pl.BlockSpec((None, None, t, t), lambda h, d: (h, d, 0, 0)),
        out_shape=jax.ShapeDtypeStruct((nh, N_BIAS_TILES, t, t), F32),
        compiler_params=_params("parallel", "parallel"), name="rel_bias_tiles",
    )(rel_bias)


def _attn_kernel(q_ref, k_ref, v_ref, bt_ref, lq1_ref, lk1_ref, lq2_ref, lk2_ref, sg_ref,
                 o_ref, q2_ref, vx_ref, *, t, nk, n_q, n_chain, lambda_init):
    g = pl.program_id(2)
    dv = v_ref.shape[1]

    @pl.when(g == 0)
    def _():
        vx_ref[:, 0:dv] = v_ref[...]
        ones_lane = lax.broadcasted_iota(jnp.int32, (v_ref.shape[0], dv), 1) == 0
        vx_ref[:, dv:2 * dv] = jnp.where(ones_lane, 1.0, 0.0).astype(vx_ref.dtype)

    far = N_BIAS_TILES // 2
    rc = 2 * t // n_chain
    lam = (jnp.exp(jnp.sum(lq1_ref[...] * lk1_ref[...])) - jnp.exp(jnp.sum(lq2_ref[...] * lk2_ref[...]))
           + lambda_init)

    def scores(i, base, d, rows, n_keys=t):
        wrapped = i + d >= nk
        j = jnp.where(wrapped, i + d - nk, i + d)
        k = k_ref[pl.ds(pl.multiple_of(j * t, t), n_keys), :]
        q2 = q2_ref[base + rows.start:base + rows.stop, :]
        s = lax.dot_general(q2, k, (((1,), (1,)), ((), ())), preferred_element_type=F32)
        cols = [s[:, c0:c0 + LANES] for c0 in range(0, n_keys, LANES)]
        if d <= 1 or d >= nk - 1:
            qr = slice(rows.start % t, rows.start % t + rows.stop - rows.start)
            bias = bt_ref[jnp.clip(j - i, -far, far) + far, qr, :]
            return j, [c + bias[:, n * LANES:(n + 1) * LANES] for n, c in enumerate(cols)], None
        side = jnp.where(wrapped, bt_ref[0, 0:1, 0:LANES], bt_ref[2 * far, 0:1, 0:LANES])
        return j, cols, side

    def tile(i, base, d, rows, m_prev, acc_prev):
        j, cols, side = scores(i, base, d, rows)
        shift = m_prev if side is None else m_prev - side
        rel = [(col - shift).astype(BF16) for col in cols]
        rm = jnp.max(functools.reduce(jnp.maximum, rel), axis=1, keepdims=True)
        delta = jnp.maximum(rm, 0.0)
        p = jnp.concatenate([jnp.exp2(x - delta) for x in rel], axis=1)
        d32 = jnp.broadcast_to(delta.astype(F32), m_prev.shape)
        alpha = jnp.exp2(-d32)
        pv = _dot(p, vx_ref[pl.ds(pl.multiple_of(j * t, t), t), :])
        acc_new = jnp.concatenate([alpha * acc_prev[:, 0:dv] + pv[:, 0:dv],
                                   alpha * acc_prev[:, dv:2 * dv] + pv[:, dv:2 * dv]], axis=1)
        return m_prev + d32, acc_new

    streams = []
    for a in range(n_q):
        i, base = g * n_q + a, a * 2 * t
        q = q_ref[a * t:(a + 1) * t, :]
        lane = lax.broadcasted_iota(jnp.int32, q.shape, 1)
        zero = jnp.zeros_like(q)
        q2_ref[base:base + t, :] = jnp.where(lane < DIFF_HEAD_DIM, q, zero)
        q2_ref[base + t:base + 2 * t, :] = jnp.where(lane >= DIFF_HEAD_DIM, q, zero)
        for r0 in range(0, 2 * t, rc):
            streams.append((i, base, slice(r0, r0 + rc)))
    state = []
    for i, base, rows in streams:
        m0 = jnp.max(scores(i, base, 0, rows, LANES)[1][0], axis=1, keepdims=True)
        state.append((jnp.broadcast_to(m0, (rc, LANES)), jnp.zeros((rc, 2 * dv), F32)))
    for d in range(nk):
        state = [tile(i, base, d, rows, *st) for (i, base, rows), st in zip(streams, state)]

    for a in range(n_q):
        acc = jnp.concatenate([acc for _, acc in state[a * n_chain:(a + 1) * n_chain]], axis=0)
        o = acc[:, 0:dv] / acc[:, dv:dv + 1]
        o = o[0:t] - lam * o[t:2 * t]
        o = _rms(o, sg_ref[...]) * (1.0 - lambda_init)
        o_ref[a * t:(a + 1) * t, :] = o.astype(o_ref.dtype)


def _attention(qkv, bias_tiles, lq1, lk1, lq2, lk2, subln, *, n_heads, lambda_init, t, n_chain, n_q):
    b, s, _ = qkv.shape
    nk = s // t
    assert nk * t == s and t % LANES == 0 and t + 1 >= 91 and (2 * t) % n_chain == 0 and nk % n_q == 0
    kern = functools.partial(_attn_kernel, t=t, nk=nk, n_q=n_q, n_chain=n_chain, lambda_init=lambda_init)
    vec = lambda n: pl.BlockSpec((1, n), lambda b_, h, i: (0, 0))
    return pl.pallas_call(
        kern, grid=(b, n_heads, nk // n_q),
        in_specs=[
            pl.BlockSpec((None, n_q * t, V_HEAD_DIM), lambda b_, h, i: (b_, i, h)),
            pl.BlockSpec((None, s, V_HEAD_DIM), lambda b_, h, i: (b_, 0, n_heads + h)),
            pl.BlockSpec((None, s, V_HEAD_DIM), lambda b_, h, i: (b_, 0, 2 * n_heads + h)),
            pl.BlockSpec((None, N_BIAS_TILES, t, t), lambda b_, h, i: (h, 0, 0, 0)),
            vec(DIFF_HEAD_DIM), vec(DIFF_HEAD_DIM), vec(DIFF_HEAD_DIM), vec(DIFF_HEAD_DIM),
            vec(V_HEAD_DIM),
        ],
        out_specs=pl.BlockSpec((None, n_q * t, V_HEAD_DIM), lambda b_, h, i: (b_, i, h)),
        out_shape=jax.ShapeDtypeStruct((b, s, n_heads * V_HEAD_DIM), BF16),
        scratch_shapes=[pltpu.VMEM((n_q * 2 * t, V_HEAD_DIM), BF16),
                        pltpu.VMEM((s, 2 * V_HEAD_DIM), BF16)],
        compiler_params=_params("parallel", "parallel", "arbitrary"), name="diff_attention",
    )(qkv, qkv, qkv, bias_tiles, lq1.reshape(1, -1), lk1.reshape(1, -1),
      lq2.reshape(1, -1), lk2.reshape(1, -1), subln.reshape(1, -1))


def _dft_tables(n_hi):
    n = n_hi * N_LO
    half = n_hi // 2
    hi = np.arange(n_hi)
    lo = np.arange(N_LO)
    f_hi = np.exp(-2j * np.pi * ((np.outer(hi, hi) % n_hi) / n_hi))
    fr, fi = f_hi.real, f_hi.imag
    s1_pair = np.block([[fr[:, :half], -fi[:, :half]], [fi[:, :half], fr[:, :half]]])
    s1_real = np.concatenate([fr[:, :half], fi[:, :half]], axis=0)
    ph = (hi[:, None, None] * lo[None, None, :] + n_hi * lo[None, :, None] * lo[None, None, :]) % n
    g = np.exp(-2j * np.pi * ph / n)
    g2 = np.concatenate([np.concatenate([g.real, -g.imag], axis=2),
                         np.concatenate([g.imag, g.real], axis=2)], axis=1)
    gh = np.stack([g2, np.swapaxes(g2, 1, 2)], axis=1)
    s4 = np.block([[fr[:half], fi[:half]], [-fi[:half], fr[:half]]]) / n
    cast = lambda a: jnp.asarray(a.astype(np.float32)).astype(BF16)
    return cast(s1_pair), cast(s1_real), cast(gh), cast(s4)


def _hdot(a, b, dims=(((1,), (0,)), ((), ()))):
    (ah, al), (bh, bl) = _split(a), _split(b)
    dg = functools.partial(lax.dot_general, dimension_numbers=dims, preferred_element_type=F32)
    return dg(ah, bh) + dg(ah, bl) + dg(al, bh)


def _filter_kernel(zt_ref, t_ref, w1_ref, b1_ref, w2_ref, b2_ref, w3_ref, b3_ref, w4_ref, fr_ref, dl_ref,
                   hf_ref, hb_ref, *, c):
    i = pl.program_id(0)
    fr = fr_ref[...]
    a = jnp.sin(fr * (_hdot(w1_ref[...], zt_ref[...]) + b1_ref[...]))
    a = jnp.sin(fr * (_hdot(w2_ref[...], a) + b2_ref[...]))
    a = jnp.sin(fr * (_hdot(w3_ref[...], a) + b3_ref[...]))
    hh = _hdot(a, w4_ref[...], (((0,), (0,)), ((), ())))
    decay = jnp.exp(-t_ref[...] * jnp.abs(dl_ref[...]))
    hf_ref[...] = (hh[:, 0:c] * decay).astype(hf_ref.dtype)
    r = lax.broadcasted_iota(jnp.int32, (hh.shape[0], c), 0)
    lag0 = jnp.logical_and(i == 0, r == 0)
    hb_ref[...] = jnp.where(lag0, 0.0, hh[:, c:2 * c] * decay).astype(hb_ref.dtype)


def _filter_taps(z, w1, b1, w2, b2, w3, b3, w4, freq, deltas, *, tb=512):
    s, emb = z.shape
    c = deltas.shape[0]
    assert s % tb == 0
    col = lambda v: v.reshape(-1, 1)
    args = [jnp.pad(z, ((0, 0), (0, LANES - emb))).T, z[:, 0:1],
            jnp.pad(w1, ((0, LANES - emb), (0, 0))).T, col(b1), w2.T, col(b2), w3.T, col(b3), w4,
            col(freq), deltas.reshape(1, c)]
    specs = [pl.BlockSpec((LANES, tb), lambda i: (0, i)), pl.BlockSpec((tb, 1), lambda i: (i, 0))] + [
        pl.BlockSpec(a.shape, lambda i: (0, 0)) for a in args[2:]]
    out = pl.BlockSpec((tb, c), lambda i: (i, 0))
    shp = jax.ShapeDtypeStruct((s, c), BF16)
    return pl.pallas_call(
        functools.partial(_filter_kernel, c=c), grid=(s // tb,),
        in_specs=specs, out_specs=[out, out], out_shape=[shp, shp],
        compiler_params=_params("parallel"), name="hyena_filter_taps",
    )(*args)


def _lo_major(ref):
    return jnp.swapaxes(ref[...], 0, 1)


def _dft1_kernel(x0_ref, x1_ref, f_ref, a_ref):
    n_out = a_ref.shape[0]
    x0, x1 = _lo_major(x0_ref), _lo_major(x1_ref)
    a = jnp.stack([_dot(f_ref[...], jnp.concatenate([x0[j], x1[j]], axis=0)).astype(a_ref.dtype)
                   for j in range(x0.shape[0])], axis=0)
    a = jnp.swapaxes(a, 0, 1)
    a_ref[:, 0] = a[0:n_out]
    a_ref[:, 1] = a[n_out:2 * n_out]


def _dft1_pair(xv, f_mat, *, n_hi, t_lo=16):
    b, half, _, c = xv.shape
    spec = lambda off: pl.BlockSpec((None, half, t_lo, c), lambda p, j: (2 * p + off, 0, j, 0))
    return pl.pallas_call(
        _dft1_kernel, grid=(b // 2, N_LO // t_lo),
        in_specs=[spec(0), spec(1), pl.BlockSpec(f_mat.shape, lambda p, j: (0, 0))],
        out_specs=pl.BlockSpec((None, n_hi, 2, t_lo, c), lambda p, j: (p, 0, 0, j, 0)),
        out_shape=jax.ShapeDtypeStruct((b // 2, n_hi, 2, N_LO, c), BF16),
        compiler_params=_params("parallel", "parallel"), name="hyena_dft1",
    )(xv, xv, f_mat)


def _dft1_filter_kernel(hf_ref, hb_ref, f_ref, fa_ref):
    n_out = fa_ref.shape[0]
    c = hf_ref.shape[2]
    hf, hb = _lo_major(hf_ref), _lo_major(hb_ref)
    a = jnp.stack([_dot(f_ref[...], jnp.concatenate([hf[j], hb[j]], axis=1)).astype(fa_ref.dtype)
                   for j in range(hf.shape[0])], axis=0)
    a = jnp.swapaxes(a, 0, 1)
    for part, (r0, c0) in enumerate(((0, 0), (n_out, 0), (0, c), (n_out, c))):
        fa_ref[:, part] = a[r0:r0 + n_out, :, c0:c0 + c]


def _dft1_filter(hf, hb, f_mat, *, n_hi, t_lo=16):
    half, _, c = hf.shape
    blk = pl.BlockSpec((half, t_lo, c), lambda j: (0, j, 0))
    return pl.pallas_call(
        _dft1_filter_kernel, grid=(N_LO // t_lo,),
        in_specs=[blk, blk, pl.BlockSpec(f_mat.shape, lambda j: (0, 0))],
        out_specs=pl.BlockSpec((n_hi, 4, t_lo, c), lambda j: (0, 0, j, 0)),
        out_shape=jax.ShapeDtypeStruct((n_hi, 4, N_LO, c), BF16),
        compiler_params=_params("parallel"), name="hyena_filter_dft1",
    )(hf, hb, f_mat)


def _spectral_kernel(a_ref, fa_ref, gh_ref, cc_ref):
    c = a_ref.shape[-1]
    for u in range(a_ref.shape[0]):
        a = jnp.concatenate([a_ref[u].reshape(2 * N_LO, c), fa_ref[u, 0:2].reshape(2 * N_LO, c),
                             fa_ref[u, 2:4].reshape(2 * N_LO, c)], axis=1)
        x = _dot(gh_ref[u, 0], a)
        xr, xi = x[0:N_LO, 0:c], x[N_LO:2 * N_LO, 0:c]
        kr = x[0:N_LO, c:2 * c] + x[0:N_LO, 2 * c:3 * c]
        ki = x[N_LO:2 * N_LO, c:2 * c] - x[N_LO:2 * N_LO, 2 * c:3 * c]
        y = jnp.concatenate([xr * kr - xi * ki, xr * ki + xi * kr], axis=0).astype(BF16)
        cc_ref[u] = _dot(gh_ref[u, 1], y).astype(cc_ref.dtype).reshape(2, N_LO, c)


def _spectral(a, fa, gh, *, n_hi, kb=8):
    p, _, _, _, c = a.shape
    assert n_hi % kb == 0
    blk = pl.BlockSpec((None, kb, 2, N_LO, c), lambda q, k: (q, k, 0, 0, 0))
    return pl.pallas_call(
        _spectral_kernel, grid=(p, n_hi // kb),
        in_specs=[blk, pl.BlockSpec((kb, 4, N_LO, c), lambda q, k: (k, 0, 0, 0)),
                  pl.BlockSpec((kb, 2, 2 * N_LO, 2 * N_LO), lambda q, k: (k, 0, 0, 0))],
        out_specs=blk, out_shape=jax.ShapeDtypeStruct(a.shape, BF16),
        compiler_params=_params("parallel", "parallel"), name="hyena_spectral",
    )(a, fa, gh)


def _idft_kernel(cc_ref, m_ref, o_ref):
    half = o_ref.shape[1]
    cr, ci = jnp.swapaxes(cc_ref[:, 0], 0, 1), jnp.swapaxes(cc_ref[:, 1], 0, 1)
    y = jnp.stack([_dot(m_ref[...], jnp.concatenate([cr[j], ci[j]], axis=0)) for j in range(cr.shape[0])],
                  axis=0)
    y = jnp.swapaxes(y, 0, 1)
    o_ref[0] = y[0:half]
    o_ref[1] = y[half:2 * half]


def _idft(cc, s4, *, n_hi, t_lo=16):
    p, _, _, _, c = cc.shape
    half = n_hi // 2
    return pl.pallas_call(
        _idft_kernel, grid=(p, N_LO // t_lo),
        in_specs=[pl.BlockSpec((None, n_hi, 2, t_lo, c), lambda q, j: (q, 0, 0, j, 0)),
                  pl.BlockSpec(s4.shape, lambda q, j: (0, 0))],
        out_specs=pl.BlockSpec((2, half, t_lo, c), lambda q, j: (q, 0, j, 0)),
        out_shape=jax.ShapeDtypeStruct((2 * p, half, N_LO, c), F32),
        compiler_params=_params("parallel", "parallel"), name="hyena_idft",
    )(cc, s4)


def _filter_positions(s):
    t = jnp.linspace(0.0, 1.0, s, dtype=F32)[:, None]
    bands = (HYENA_EMB_DIM - 1) // 2
    w = 2.0 * math.pi * jnp.arange(s, dtype=F32)[:, None] / s
    f = jnp.linspace(1e-4, bands - 1, bands, dtype=F32)[None, :]
    fw = f * w
    return jnp.concatenate([t, jnp.cos(fw), -jnp.sin(fw)], axis=-1)


def _hyena_conv(vv, w1, b1, w2, b2, w3, b3, w4, freq):
    b, s, c = vv.shape
    n_hi = 2 * s // N_LO
    half = n_hi // 2
    assert b % 2 == 0 and n_hi * N_LO == 2 * s
    s1_pair, s1_real, gh, s4 = _dft_tables(n_hi)

    max_decay = math.log(HYENA_DECAY_TARGET) / HYENA_FAST_DECAY
    min_decay = math.log(HYENA_DECAY_TARGET) / HYENA_SLOW_DECAY
    deltas = jnp.linspace(min_decay, max_decay, c, dtype=F32)
    hf, hb = _filter_taps(_filter_positions(s), w1, b1, w2, b2, w3, b3, w4, freq, deltas)
    fa = _dft1_filter(hf.reshape(half, N_LO, c), hb.reshape(half, N_LO, c), s1_real, n_hi=n_hi)

    a = _dft1_pair(vv.reshape(b, half, N_LO, c), s1_pair, n_hi=n_hi)
    cc = _spectral(a, fa, gh, n_hi=n_hi)
    return _idft(cc, s4, n_hi=n_hi).reshape(b, s, c)


def kernel(x, rel_bias, ffn1_norm, ffn1_w_gate, ffn1_w_up, ffn1_w_down, mix_norm, w_in,
           lambda_q1, lambda_k1, lambda_q2, lambda_k2, diff_subln,
           hy_conv_w, hy_conv_b, hy_f_w1, hy_f_b1, hy_f_w2, hy_f_b2, hy_f_w3, hy_f_b3,
           hy_f_w4, hy_f_freq, hy_bias, hy_out_norm, w_out,
           ffn2_norm, ffn2_w_gate, ffn2_w_up, ffn2_w_down, final_norm):
    b, s, d = x.shape
    depth = w_in.shape[0]
    d_att = diff_subln.shape[1] * rel_bias.shape[1]
    n_heads = rel_bias.shape[1]
    attn_tile = min(512, s)
    bf = lambda a: a.astype(BF16)

    xf = x.reshape(b * s, d)
    for l in range(depth):
        last = l == depth - 1
        xf = _ffn(xf, ffn1_norm[l], ffn1_w_gate[l], ffn1_w_up[l], ffn1_w_down[l])
        qkv, vv, x0 = _mix(xf, mix_norm[l], bf(w_in[l]), hy_conv_w[l], hy_conv_b[l], d_att=d_att, seq=s)
        lambda_init = 0.8 - 0.6 * math.exp(-0.3 * l)
        att = _attention(qkv.reshape(b, s, -1), _bias_tiles(rel_bias, attn_tile),
                         lambda_q1[l], lambda_k1[l], lambda_q2[l], lambda_k2[l], diff_subln[l],
                         n_heads=n_heads, lambda_init=lambda_init, t=attn_tile, n_chain=4,
                         n_q=2 if (s // attn_tile) % 2 == 0 else 1)
        y = _hyena_conv(vv.reshape(b, s, -1), hy_f_w1[l], hy_f_b1[l], hy_f_w2[l], hy_f_b2[l],
                        hy_f_w3[l], hy_f_b3[l], hy_f_w4[l], hy_f_freq[l])
        hy_parts = (y.reshape(b * s, -1), vv, x0, hy_bias[l], hy_out_norm[l])
        wo = bf(w_out[l])
        xf = _ffn(xf, ffn2_norm[l], ffn2_w_gate[l], ffn2_w_up[l], ffn2_w_down[l],
                  pre=(att.reshape(b * s, -1), wo[:d_att], wo[d_att:], hy_parts),
                  final_g=final_norm if last else None)
    if depth == 0:
        raise ValueError("depth must be positive")
    return xf.reshape(b, s, d)
```

```python
import functools
import math

import numpy as np
import jax
import jax.numpy as jnp
from jax import lax
from jax.experimental import pallas as pl
from jax.experimental.pallas import tpu as pltpu

F32 = jnp.float32
BF16 = jnp.bfloat16

RMS_EPS = 1e-6
LANES = 128
BF16_ROWS = 16
DIFF_HEAD_DIM = 64
V_HEAD_DIM = 2 * DIFF_HEAD_DIM
REL_BUCKETS = 32
REL_MAX_DIST = 128
N_HYENA_GROUPS = 8
HYENA_EMB_DIM = 33
HYENA_DECAY_TARGET = 1e-2
HYENA_FAST_DECAY = 0.3
HYENA_SLOW_DECAY = 1.5
N_LO = 128
LOG2_E = math.log2(math.e)
VMEM_LIMIT = 56 * 1024 * 1024


def _params(*sem):
    return pltpu.CompilerParams(dimension_semantics=sem, vmem_limit_bytes=VMEM_LIMIT)


def _resident(shape):
    return pl.BlockSpec(shape, lambda *_: (0,) * len(shape), pipeline_mode=pl.Buffered(1))


def _rms(x, g):
    return x * lax.rsqrt(jnp.mean(x * x, axis=-1, keepdims=True) + RMS_EPS) * g


def _dot(a, b):
    return jnp.dot(a, b, preferred_element_type=F32)


def _split(a):
    hi = a.astype(BF16)
    return hi, (a - hi.astype(F32)).astype(BF16)


def _hyena_gate_norm(y_ref, vv_ref, x0_ref, fb_ref, og_ref, mg_ref):
    vv = vv_ref[...].astype(F32)
    y = (y_ref[...] + vv * fb_ref[...]) * x0_ref[...].astype(F32)
    hi, lo = _split(y * y)
    ms = _dot(hi, mg_ref[...]) + _dot(lo, mg_ref[...])
    return y * lax.rsqrt(ms + RMS_EPS) * og_ref[...]


N_STAGE = 4


def _stage_weights(srcs, dsts, stages, sem):
    jobs = [(src, dst, stage, c) for src, dst, stage in zip(srcs, dsts, stages) for c in range(N_STAGE)]

    def copy(k):
        src, _, stage, c = jobs[k]
        rows = stage.shape[1]
        return pltpu.make_async_copy(src.at[pl.ds(c * rows, rows), :], stage.at[k % 2], sem.at[k % 2])

    copy(0).start()
    for k, (_, dst, stage, c) in enumerate(jobs):
        if k + 1 < len(jobs):
            copy(k + 1).start()
        copy(k).wait()
        rows = stage.shape[1]
        dst[pl.ds(c * rows, rows), :] = stage[k % 2].astype(dst.dtype)


def _ffn_kernel(*refs, ff_chunk, n_chunk, pre, post):
    if pre:
        x_ref, att_ref, woa_ref, woh_ref = refs[:4]
        hy_refs = refs[4:10]
        refs = refs[10:]
    else:
        x_ref = refs[0]
        refs = refs[1:]
    g_ref, wg_hbm, wu_hbm, wd_hbm = refs[:4]
    refs = refs[4:]
    if post:
        fg_ref, o_ref = refs[:2]
        refs = refs[2:]
    else:
        o_ref = refs[0]
        refs = refs[1:]
    wg_ref, wu_ref, wd_ref, stage_in, stage_out, sem = refs

    @pl.when(pl.program_id(0) == 0)
    def _():
        _stage_weights((wg_hbm, wu_hbm, wd_hbm), (wg_ref, wu_ref, wd_ref),
                       (stage_in, stage_in, stage_out), sem)

    x = x_ref[...]
    if pre:
        hy = _hyena_gate_norm(*hy_refs).astype(BF16)
        x = x + _dot(att_ref[...], woa_ref[...]) + _dot(hy, woh_ref[...])
    xn = _rms(x, g_ref[...]).astype(BF16)
    acc = jnp.zeros(x.shape, F32)
    for c in range(n_chunk):
        sl = slice(c * ff_chunk, (c + 1) * ff_chunk)
        gate = _dot(xn, wg_ref[:, sl])
        up = _dot(xn, wu_ref[:, sl])
        h = (jax.nn.silu(gate) * up).astype(BF16)
        acc = acc + _dot(h, wd_ref[sl, :])
    y = x + 0.5 * acc
    if post:
        y = _rms(y, fg_ref[...])
    o_ref[...] = y


def _ffn(x, norm_g, wg, wu, wd, *, pre=None, final_g=None, tm=512, ff_chunk=256):
    m, d = x.shape
    dff = wg.shape[1]
    n_chunk = dff // ff_chunk
    assert n_chunk * ff_chunk == dff and m % tm == 0
    assert d % (BF16_ROWS * N_STAGE) == 0 and dff % (BF16_ROWS * N_STAGE) == 0
    row = lambda w: pl.BlockSpec((tm, w), lambda i: (i, 0))
    args, specs = [x], [row(d)]
    if pre is not None:
        att, woa, woh, (y, vv, x0, filt_bias, out_g) = pre
        c = y.shape[1]
        gdim = c // N_HYENA_GROUPS
        assert gdim & (gdim - 1) == 0
        grp = np.arange(c) // gdim
        mg = jnp.asarray((grp[:, None] == grp[None, :]).astype(np.float32) / gdim).astype(BF16)
        args += [att, woa, woh, y, vv, x0, filt_bias.reshape(1, c), out_g.reshape(1, c), mg]
        specs += [row(att.shape[1]), _resident(woa.shape), _resident(woh.shape), row(c), row(c), row(c),
                  _resident((1, c)), _resident((1, c)), _resident((c, c))]
    hbm = pl.BlockSpec(memory_space=pl.ANY)
    args += [norm_g.reshape(1, d), wg, wu, wd]
    specs += [_resident((1, d)), hbm, hbm, hbm]
    if final_g is not None:
        args.append(final_g.reshape(1, d))
        specs.append(_resident((1, d)))
    kern = functools.partial(_ffn_kernel, ff_chunk=ff_chunk, n_chunk=n_chunk,
                             pre=pre is not None, post=final_g is not None)
    return pl.pallas_call(
        kern, grid=(m // tm,), in_specs=specs, out_specs=row(d),
        out_shape=jax.ShapeDtypeStruct((m, d), F32),
        scratch_shapes=[pltpu.VMEM((d, dff), BF16), pltpu.VMEM((d, dff), BF16), pltpu.VMEM((dff, d), BF16),
                        pltpu.VMEM((2, d // N_STAGE, dff), F32), pltpu.VMEM((2, dff // N_STAGE, d), F32),
                        pltpu.SemaphoreType.DMA((2,))],
        compiler_params=_params("arbitrary"), name="ffn_pre" if pre is not None else "ffn",
    )(*args)


HALO = 8


def _mix_kernel(x_ref, xp_ref, xn_ref, g_ref, w_hbm, cw_ref, cb_ref, qkv_ref, vv_ref, x0_ref,
                w_ref, stage, sem, *, d_att, c, scale, tiles_per_seq, hy_chunk):
    i = pl.program_id(0)
    n_att = 3 * d_att

    @pl.when(i == 0)
    def _():
        _stage_weights((w_hbm,), (w_ref,), (stage,), sem)

    xn = _rms(x_ref[...], g_ref[...]).astype(BF16)
    halo = _rms(jnp.concatenate([xp_ref[...], xn_ref[...]], axis=0), g_ref[...]).astype(BF16)
    first = i % tiles_per_seq == 0
    last = i % tiles_per_seq == tiles_per_seq - 1
    rows = x_ref.shape[0]
    cw = hy_chunk
    r = lax.broadcasted_iota(jnp.int32, (HALO, cw), 0)
    head, tail = slice(0, HALO), slice(rows - HALO, rows)

    def conv_cols(part, k0):
        cs = slice(n_att + part * c + k0, n_att + part * c + k0 + cw)
        ws = slice(part * c + k0, part * c + k0 + cw)
        z = _dot(xn, w_ref[:, cs])
        zh = _dot(halo, w_ref[:, cs])
        prev_row = jnp.where(first, 0.0, zh[HALO - 1:HALO])
        next_row = jnp.where(last, 0.0, zh[HALO:HALO + 1])
        z_m1 = pltpu.roll(z, 1, 0)
        z_p1 = pltpu.roll(z, rows - 1, 0)
        conv = lambda zm, zc, zp: (cb_ref[:, ws] + zm * cw_ref[0:1, ws] + zc * cw_ref[1:2, ws]
                                   + zp * cw_ref[2:3, ws])
        mid = conv(z_m1, z, z_p1)
        top = conv(jnp.where(r == 0, prev_row, z_m1[head]), z[head], z_p1[head])
        bot = conv(z_m1[tail], z[tail], jnp.where(r == HALO - 1, next_row, z_p1[tail]))
        return jnp.concatenate([top, mid[HALO:rows - HALO], bot], axis=0)

    for k0 in range(0, c, cw):
        vv_ref[:, k0:k0 + cw] = (conv_cols(2, k0) * conv_cols(1, k0)).astype(vv_ref.dtype)
        x0_ref[:, k0:k0 + cw] = conv_cols(0, k0).astype(x0_ref.dtype)

    for c0 in range(0, n_att, d_att):
        p = _dot(xn, w_ref[:, c0:c0 + d_att])
        if c0 == 0:
            p = p * scale
        qkv_ref[:, c0:c0 + d_att] = p.astype(BF16)


def _mix(x, norm_g, w_in, conv_w, conv_b, *, d_att, seq, tm=512, hy_chunk=256):
    m, d = x.shape
    n_att = 3 * d_att
    c = (w_in.shape[1] - n_att) // 3
    assert m % tm == 0 and seq % tm == 0 and tm % HALO == 0 and d % (BF16_ROWS * N_STAGE) == 0
    kern = functools.partial(_mix_kernel, d_att=d_att, c=c, scale=DIFF_HEAD_DIM ** -0.5 * LOG2_E,
                             tiles_per_seq=seq // tm, hy_chunk=hy_chunk)
    sub = tm // HALO
    last_blk = m // HALO - 1
    row = lambda w: pl.BlockSpec((tm, w), lambda i: (i, 0))
    return pl.pallas_call(
        kern, grid=(m // tm,),
        in_specs=[row(d),
                  pl.BlockSpec((HALO, d), lambda i: (jnp.maximum(i * sub - 1, 0), 0)),
                  pl.BlockSpec((HALO, d), lambda i: (jnp.minimum((i + 1) * sub, last_blk), 0)),
                  _resident((1, d)), pl.BlockSpec(memory_space=pl.ANY), _resident(conv_w.shape),
                  _resident((1, 3 * c))],
        out_specs=[row(n_att), row(c), row(c)],
        out_shape=[jax.ShapeDtypeStruct((m, n_att), BF16), jax.ShapeDtypeStruct((m, c), BF16),
                   jax.ShapeDtypeStruct((m, c), BF16)],
        scratch_shapes=[pltpu.VMEM(w_in.shape, BF16), pltpu.VMEM((2, d // N_STAGE, w_in.shape[1]), F32),
                        pltpu.SemaphoreType.DMA((2,))],
        compiler_params=_params("arbitrary"), name="mix",
    )(x, x, x, norm_g.reshape(1, d), w_in, conv_w, conv_b.reshape(1, 3 * c))


N_BIAS_TILES = 5


def _bias_kernel(tab_ref, o_ref, *, t):
    h = pl.program_id(0)
    d = pl.program_id(1) - N_BIAS_TILES // 2
    half = REL_BUCKETS // 2
    max_exact = half // 2
    rel = lax.broadcasted_iota(jnp.int32, (8, 2 * t), 1) + (d - 1) * t
    ret = jnp.where(rel > 0, half, 0)
    n = jnp.abs(rel)
    nf = jnp.maximum(n, 1).astype(F32)
    large = max_exact + (jnp.log(nf / max_exact) / math.log(REL_MAX_DIST / max_exact)
                         * (half - max_exact)).astype(jnp.int32)
    large = jnp.minimum(large, half - 1)
    bucket = ret + jnp.where(n < max_exact, n, large)
    row = jnp.zeros(rel.shape, F32)
    for b in range(REL_BUCKETS):
        row = jnp.where(bucket == b, tab_ref[b, h], row)
    rows = jnp.broadcast_to(row[0:1, :] * LOG2_E, (t, 2 * t))
    o_ref[...] = pltpu.roll(rows, 0, 1, stride=1, stride_axis=0)[:, t:2 * t]


def _bias_tiles(rel_bias, t):
    nh = rel_bias.shape[1]
    return pl.pallas_call(
        functools.partial(_bias_kernel, t=t), grid=(nh, N_BIAS_TILES),
        in_specs=[pl.BlockSpec(memory_space=pltpu.SMEM)],
        out_specs=pl.BlockSpec((None, None, t, t), lambda h, d: (h, d, 0, 0)),
        out_shape=jax.ShapeDtypeStruct((nh, N_BIAS_TILES, t, t), F32),
        compiler_params=_params("parallel", "parallel"), name="rel_bias_tiles",
    )(rel_bias)


def _attn_kernel(q_ref, k_ref, v_ref, bt_ref, lq1_ref, lk1_ref, lq2_ref, lk2_ref, sg_ref,
                 o_ref, q2_ref, vx_ref, *, t, nk, n_q, n_chain, lambda_init):
    g = pl.program_id(2)
    dv = v_ref.shape[1]

    @pl.when(g == 0)
    def _():
        vx_ref[:, 0:dv] = v_ref[...]
        ones_lane = lax.broadcasted_iota(jnp.int32, (v_ref.shape[0], dv), 1) == 0
        vx_ref[:, dv:2 * dv] = jnp.where(ones_lane, 1.0, 0.0).astype(vx_ref.dtype)

    far = N_BIAS_TILES // 2
    rc = 2 * t // n_chain
    lam = (jnp.exp(jnp.sum(lq1_ref[...] * lk1_ref[...])) - jnp.exp(jnp.sum(lq2_ref[...] * lk2_ref[...]))
           + lambda_init)

    def scores(i, base, d, rows, n_keys=t):
        wrapped = i + d >= nk
        j = jnp.where(wrapped, i + d - nk, i + d)
        k = k_ref[pl.ds(pl.multiple_of(j * t, t), n_keys), :]
        q2 = q2_ref[base + rows.start:base + rows.stop, :]
        s = lax.dot_general(q2, k, (((1,), (1,)), ((), ())), preferred_element_type=F32)
        cols = [s[:, c0:c0 + LANES] for c0 in range(0, n_keys, LANES)]
        if d <= 1 or d >= nk - 1:
            qr = slice(rows.start % t, rows.start % t + rows.stop - rows.start)
            bias = bt_ref[jnp.clip(j - i, -far, far) + far, qr, :]
            return j, [c + bias[:, n * LANES:(n + 1) * LANES] for n, c in enumerate(cols)], None
        side = jnp.where(wrapped, bt_ref[0, 0:1, 0:LANES], bt_ref[2 * far, 0:1, 0:LANES])
        return j, cols, side

    def tile(i, base, d, rows, m_prev, acc_prev):
        j, cols, side = scores(i, base, d, rows)
        shift = m_prev if side is None else m_prev - side
        rel = [(col - shift).astype(BF16) for col in cols]
        rm = jnp.max(functools.reduce(jnp.maximum, rel), axis=1, keepdims=True)
        delta = jnp.maximum(rm, 0.0)
        p = jnp.concatenate([jnp.exp2(x - delta) for x in rel], axis=1)
        d32 = jnp.broadcast_to(delta.astype(F32), m_prev.shape)
        alpha = jnp.exp2(-d32)
        pv = _dot(p, vx_ref[pl.ds(pl.multiple_of(j * t, t), t), :])
        acc_new = jnp.concatenate([alpha * acc_prev[:, 0:dv] + pv[:, 0:dv],
                                   alpha * acc_prev[:, dv:2 * dv] + pv[:, dv:2 * dv]], axis=1)
        return m_prev + d32, acc_new

    streams = []
    for a in range(n_q):
        i, base = g * n_q + a, a * 2 * t
        q = q_ref[a * t:(a + 1) * t, :]
        lane = lax.broadcasted_iota(jnp.int32, q.shape, 1)
        zero = jnp.zeros_like(q)
        q2_ref[base:base + t, :] = jnp.where(lane < DIFF_HEAD_DIM, q, zero)
        q2_ref[base + t:base + 2 * t, :] = jnp.where(lane >= DIFF_HEAD_DIM, q, zero)
        for r0 in range(0, 2 * t, rc):
            streams.append((i, base, slice(r0, r0 + rc)))
    state = []
    for i, base, rows in streams:
        m0 = jnp.max(scores(i, base, 0, rows, LANES)[1][0], axis=1, keepdims=True)
        state.append((jnp.broadcast_to(m0, (rc, LANES)), jnp.zeros((rc, 2 * dv), F32)))
    for d in range(nk):
        state = [tile(i, base, d, rows, *st) for (i, base, rows), st in zip(streams, state)]

    for a in range(n_q):
        acc = jnp.concatenate([acc for _, acc in state[a * n_chain:(a + 1) * n_chain]], axis=0)
        o = acc[:, 0:dv] / acc[:, dv:dv + 1]
        o = o[0:t] - lam * o[t:2 * t]
        o = _rms(o, sg_ref[...]) * (1.0 - lambda_init)
        o_ref[a * t:(a + 1) * t, :] = o.astype(o_ref.dtype)


def _attention(qkv, bias_tiles, lq1, lk1, lq2, lk2, subln, *, n_heads, lambda_init, t, n_chain, n_q):
    b, s, _ = qkv.shape
    nk = s // t
    assert nk * t == s and t % LANES == 0 and t + 1 >= 91 and (2 * t) % n_chain == 0 and nk % n_q == 0
    kern = functools.partial(_attn_kernel, t=t, nk=nk, n_q=n_q, n_chain=n_chain, lambda_init=lambda_init)
    vec = lambda n: pl.BlockSpec((1, n), lambda b_, h, i: (0, 0))
    return pl.pallas_call(
        kern, grid=(b, n_heads, nk // n_q),
        in_specs=[
            pl.BlockSpec((None, n_q * t, V_HEAD_DIM), lambda b_, h, i: (b_, i, h)),
            pl.BlockSpec((None, s, V_HEAD_DIM), lambda b_, h, i: (b_, 0, n_heads + h)),
            pl.BlockSpec((None, s, V_HEAD_DIM), lambda b_, h, i: (b_, 0, 2 * n_heads + h)),
            pl.BlockSpec((None, N_BIAS_TILES, t, t), lambda b_, h, i: (h, 0, 0, 0)),
            vec(DIFF_HEAD_DIM), vec(DIFF_HEAD_DIM), vec(DIFF_HEAD_DIM), vec(DIFF_HEAD_DIM),
            vec(V_HEAD_DIM),
        ],
        out_specs=pl.BlockSpec((None, n_q * t, V_HEAD_DIM), lambda b_, h, i: (b_, i, h)),
        out_shape=jax.ShapeDtypeStruct((b, s, n_heads * V_HEAD_DIM), BF16),
        scratch_shapes=[pltpu.VMEM((n_q * 2 * t, V_HEAD_DIM), BF16),
                        pltpu.VMEM((s, 2 * V_HEAD_DIM), BF16)],
        compiler_params=_params("parallel", "parallel", "arbitrary"), name="diff_attention",
    )(qkv, qkv, qkv, bias_tiles, lq1.reshape(1, -1), lk1.reshape(1, -1),
      lq2.reshape(1, -1), lk2.reshape(1, -1), subln.reshape(1, -1))


def _dft_tables(n_hi):
    n = n_hi * N_LO
    half = n_hi // 2
    hi = np.arange(n_hi)
    lo = np.arange(N_LO)
    f_hi = np.exp(-2j * np.pi * ((np.outer(hi, hi) % n_hi) / n_hi))
    fr, fi = f_hi.real, f_hi.imag
    s1_pair = np.block([[fr[:, :half], -fi[:, :half]], [fi[:, :half], fr[:, :half]]])
    s1_real = np.concatenate([fr[:, :half], fi[:, :half]], axis=0)
    ph = (hi[:, None, None] * lo[None, None, :] + n_hi * lo[None, :, None] * lo[None, None, :]) % n
    g = np.exp(-2j * np.pi * ph / n)
    g2 = np.concatenate([np.concatenate([g.real, -g.imag], axis=2),
                         np.concatenate([g.imag, g.real], axis=2)], axis=1)
    gh = np.stack([g2, np.swapaxes(g2, 1, 2)], axis=1)
    s4 = np.block([[fr[:half], fi[:half]], [-fi[:half], fr[:half]]]) / n
    cast = lambda a: jnp.asarray(a.astype(np.float32)).astype(BF16)
    return cast(s1_pair), cast(s1_real), cast(gh), cast(s4)


def _hdot(a, b, dims=(((1,), (0,)), ((), ()))):
    (ah, al), (bh, bl) = _split(a), _split(b)
    dg = functools.partial(lax.dot_general, dimension_numbers=dims, preferred_element_type=F32)
    return dg(ah, bh) + dg(ah, bl) + dg(al, bh)


def _filter_kernel(zt_ref, t_ref, w1_ref, b1_ref, w2_ref, b2_ref, w3_ref, b3_ref, w4_ref, fr_ref, dl_ref,
                   hf_ref, hb_ref, *, c):
    i = pl.program_id(0)
    fr = fr_ref[...]
    a = jnp.sin(fr * (_hdot(w1_ref[...], zt_ref[...]) + b1_ref[...]))
    a = jnp.sin(fr * (_hdot(w2_ref[...], a) + b2_ref[...]))
    a = jnp.sin(fr * (_hdot(w3_ref[...], a) + b3_ref[...]))
    hh = _hdot(a, w4_ref[...], (((0,), (0,)), ((), ())))
    decay = jnp.exp(-t_ref[...] * jnp.abs(dl_ref[...]))
    hf_ref[...] = (hh[:, 0:c] * decay).astype(hf_ref.dtype)
    r = lax.broadcasted_iota(jnp.int32, (hh.shape[0], c), 0)
    lag0 = jnp.logical_and(i == 0, r == 0)
    hb_ref[...] = jnp.where(lag0, 0.0, hh[:, c:2 * c] * decay).astype(hb_ref.dtype)


def _filter_taps(z, w1, b1, w2, b2, w3, b3, w4, freq, deltas, *, tb=512):
    s, emb = z.shape
    c = deltas.shape[0]
    assert s % tb == 0
    col = lambda v: v.reshape(-1, 1)
    args = [jnp.pad(z, ((0, 0), (0, LANES - emb))).T, z[:, 0:1],
            jnp.pad(w1, ((0, LANES - emb), (0, 0))).T, col(b1), w2.T, col(b2), w3.T, col(b3), w4,
            col(freq), deltas.reshape(1, c)]
    specs = [pl.BlockSpec((LANES, tb), lambda i: (0, i)), pl.BlockSpec((tb, 1), lambda i: (i, 0))] + [
        pl.BlockSpec(a.shape, lambda i: (0, 0)) for a in args[2:]]
    out = pl.BlockSpec((tb, c), lambda i: (i, 0))
    shp = jax.ShapeDtypeStruct((s, c), BF16)
    return pl.pallas_call(
        functools.partial(_filter_kernel, c=c), grid=(s // tb,),
        in_specs=specs, out_specs=[out, out], out_shape=[shp, shp],
        compiler_params=_params("parallel"), name="hyena_filter_taps",
    )(*args)


def _lo_major(ref):
    return jnp.swapaxes(ref[...], 0, 1)


def _dft1_kernel(x0_ref, x1_ref, f_ref, a_ref):
    n_out = a_ref.shape[0]
    x0, x1 = _lo_major(x0_ref), _lo_major(x1_ref)
    a = jnp.stack([_dot(f_ref[...], jnp.concatenate([x0[j], x1[j]], axis=0)).astype(a_ref.dtype)
                   for j in range(x0.shape[0])], axis=0)
    a = jnp.swapaxes(a, 0, 1)
    a_ref[:, 0] = a[0:n_out]
    a_ref[:, 1] = a[n_out:2 * n_out]


def _dft1_pair(xv, f_mat, *, n_hi, t_lo=BF16_ROWS):
    b, half, _, c = xv.shape
    spec = lambda off: pl.BlockSpec((None, half, t_lo, c), lambda p, j: (2 * p + off, 0, j, 0))
    return pl.pallas_call(
        _dft1_kernel, grid=(b // 2, N_LO // t_lo),
        in_specs=[spec(0), spec(1), pl.BlockSpec(f_mat.shape, lambda p, j: (0, 0))],
        out_specs=pl.BlockSpec((None, n_hi, 2, t_lo, c), lambda p, j: (p, 0, 0, j, 0)),
        out_shape=jax.ShapeDtypeStruct((b // 2, n_hi, 2, N_LO, c), BF16),
        compiler_params=_params("parallel", "parallel"), name="hyena_dft1",
    )(xv, xv, f_mat)


def _dft1_filter_kernel(hf_ref, hb_ref, f_ref, fa_ref):
    n_out = fa_ref.shape[0]
    c = hf_ref.shape[2]
    hf, hb = _lo_major(hf_ref), _lo_major(hb_ref)
    a = jnp.stack([_dot(f_ref[...], jnp.concatenate([hf[j], hb[j]], axis=1)).astype(fa_ref.dtype)
                   for j in range(hf.shape[0])], axis=0)
    a = jnp.swapaxes(a, 0, 1)
    for part, (r0, c0) in enumerate(((0, 0), (n_out, 0), (0, c), (n_out, c))):
        fa_ref[:, part] = a[r0:r0 + n_out, :, c0:c0 + c]


def _dft1_filter(hf, hb, f_mat, *, n_hi, t_lo=BF16_ROWS):
    half, _, c = hf.shape
    blk = pl.BlockSpec((half, t_lo, c), lambda j: (0, j, 0))
    return pl.pallas_call(
        _dft1_filter_kernel, grid=(N_LO // t_lo,),
        in_specs=[blk, blk, pl.BlockSpec(f_mat.shape, lambda j: (0, 0))],
        out_specs=pl.BlockSpec((n_hi, 4, t_lo, c), lambda j: (0, 0, j, 0)),
        out_shape=jax.ShapeDtypeStruct((n_hi, 4, N_LO, c), BF16),
        compiler_params=_params("parallel"), name="hyena_filter_dft1",
    )(hf, hb, f_mat)


def _spectral_kernel(a_ref, fa_ref, gh_ref, cc_ref):
    c = a_ref.shape[-1]
    for u in range(a_ref.shape[0]):
        a = jnp.concatenate([a_ref[u].reshape(2 * N_LO, c), fa_ref[u, 0:2].reshape(2 * N_LO, c),
                             fa_ref[u, 2:4].reshape(2 * N_LO, c)], axis=1)
        x = _dot(gh_ref[u, 0], a)
        xr, xi = x[0:N_LO, 0:c], x[N_LO:2 * N_LO, 0:c]
        kr = x[0:N_LO, c:2 * c] + x[0:N_LO, 2 * c:3 * c]
        ki = x[N_LO:2 * N_LO, c:2 * c] - x[N_LO:2 * N_LO, 2 * c:3 * c]
        y = jnp.concatenate([xr * kr - xi * ki, xr * ki + xi * kr], axis=0).astype(BF16)
        cc_ref[u] = _dot(gh_ref[u, 1], y).astype(cc_ref.dtype).reshape(2, N_LO, c)


def _spectral(a, fa, gh, *, n_hi, kb=8):
    p, _, _, _, c = a.shape
    assert n_hi % kb == 0
    blk = pl.BlockSpec((None, kb, 2, N_LO, c), lambda q, k: (q, k, 0, 0, 0))
    return pl.pallas_call(
        _spectral_kernel, grid=(p, n_hi // kb),
        in_specs=[blk, pl.BlockSpec((kb, 4, N_LO, c), lambda q, k: (k, 0, 0, 0)),
                  pl.BlockSpec((kb, 2, 2 * N_LO, 2 * N_LO), lambda q, k: (k, 0, 0, 0))],
        out_specs=blk, out_shape=jax.ShapeDtypeStruct(a.shape, BF16),
        compiler_params=_params("parallel", "parallel"), name="hyena_spectral",
    )(a, fa, gh)


def _idft_kernel(cc_ref, m_ref, o_ref):
    half = o_ref.shape[1]
    cr, ci = jnp.swapaxes(cc_ref[:, 0], 0, 1), jnp.swapaxes(cc_ref[:, 1], 0, 1)
    y = jnp.stack([_dot(m_ref[...], jnp.concatenate([cr[j], ci[j]], axis=0)) for j in range(cr.shape[0])],
                  axis=0)
    y = jnp.swapaxes(y, 0, 1)
    o_ref[0] = y[0:half]
    o_ref[1] = y[half:2 * half]


def _idft(cc, s4, *, n_hi, t_lo=BF16_ROWS):
    p, _, _, _, c = cc.shape
    half = n_hi // 2
    return pl.pallas_call(
        _idft_kernel, grid=(p, N_LO // t_lo),
        in_specs=[pl.BlockSpec((None, n_hi, 2, t_lo, c), lambda q, j: (q, 0, 0, j, 0)),
                  pl.BlockSpec(s4.shape, lambda q, j: (0, 0))],
        out_specs=pl.BlockSpec((2, half, t_lo, c), lambda q, j: (q, 0, j, 0)),
        out_shape=jax.ShapeDtypeStruct((2 * p, half, N_LO, c), F32),
        compiler_params=_params("parallel", "parallel"), name="hyena_idft",
    )(cc, s4)


def _filter_positions(s):
    t = jnp.linspace(0.0, 1.0, s, dtype=F32)[:, None]
    bands = (HYENA_EMB_DIM - 1) // 2
    w = 2.0 * math.pi * jnp.arange(s, dtype=F32)[:, None] / s
    f = jnp.linspace(1e-4, bands - 1, bands, dtype=F32)[None, :]
    fw = f * w
    return jnp.concatenate([t, jnp.cos(fw), -jnp.sin(fw)], axis=-1)


def _hyena_conv(vv, w1, b1, w2, b2, w3, b3, w4, freq):
    b, s, c = vv.shape
    n_hi = 2 * s // N_LO
    half = n_hi // 2
    assert b % 2 == 0 and n_hi * N_LO == 2 * s
    s1_pair, s1_real, gh, s4 = _dft_tables(n_hi)

    max_decay = math.log(HYENA_DECAY_TARGET) / HYENA_FAST_DECAY
    min_decay = math.log(HYENA_DECAY_TARGET) / HYENA_SLOW_DECAY
    deltas = jnp.linspace(min_decay, max_decay, c, dtype=F32)
    hf, hb = _filter_taps(_filter_positions(s), w1, b1, w2, b2, w3, b3, w4, freq, deltas)
    fa = _dft1_filter(hf.reshape(half, N_LO, c), hb.reshape(half, N_LO, c), s1_real, n_hi=n_hi)

    a = _dft1_pair(vv.reshape(b, half, N_LO, c), s1_pair, n_hi=n_hi)
    cc = _spectral(a, fa, gh, n_hi=n_hi)
    return _idft(cc, s4, n_hi=n_hi).reshape(b, s, c)


def kernel(x, rel_bias, ffn1_norm, ffn1_w_gate, ffn1_w_up, ffn1_w_down, mix_norm, w_in,
           lambda_q1, lambda_k1, lambda_q2, lambda_k2, diff_subln,
           hy_conv_w, hy_conv_b, hy_f_w1, hy_f_b1, hy_f_w2, hy_f_b2, hy_f_w3, hy_f_b3,
           hy_f_w4, hy_f_freq, hy_bias, hy_out_norm, w_out,
           ffn2_norm, ffn2_w_gate, ffn2_w_up, ffn2_w_down, final_norm):
    b, s, d = x.shape
    depth = w_in.shape[0]
    d_att = diff_subln.shape[1] * rel_bias.shape[1]
    n_heads = rel_bias.shape[1]
    attn_tile = min(512, s)
    bf = lambda a: a.astype(BF16)

    xf = x.reshape(b * s, d)
    for l in range(depth):
        last = l == depth - 1
        xf = _ffn(xf, ffn1_norm[l], ffn1_w_gate[l], ffn1_w_up[l], ffn1_w_down[l])
        qkv, vv, x0 = _mix(xf, mix_norm[l], w_in[l], hy_conv_w[l], hy_conv_b[l], d_att=d_att, seq=s)
        lambda_init = 0.8 - 0.6 * math.exp(-0.3 * l)
        att = _attention(qkv.reshape(b, s, -1), _bias_tiles(rel_bias, attn_tile),
                         lambda_q1[l], lambda_k1[l], lambda_q2[l], lambda_k2[l], diff_subln[l],
                         n_heads=n_heads, lambda_init=lambda_init, t=attn_tile, n_chain=4,
                         n_q=2 if (s // attn_tile) % 2 == 0 else 1)
        y = _hyena_conv(vv.reshape(b, s, -1), hy_f_w1[l], hy_f_b1[l], hy_f_w2[l], hy_f_b2[l],
                        hy_f_w3[l], hy_f_b3[l], hy_f_w4[l], hy_f_freq[l])
        hy_parts = (y.reshape(b * s, -1), vv, x0, hy_bias[l], hy_out_norm[l])
        wo = bf(w_out[l])
        xf = _ffn(xf, ffn2_norm[l], ffn2_w_gate[l], ffn2_w_up[l], ffn2_w_down[l],
                  pre=(att.reshape(b * s, -1), wo[:d_att], wo[d_att:], hy_parts),
                  final_g=final_norm if last else None)
    if depth == 0:
        raise ValueError("depth must be positive")
    return xf.reshape(b, s, d)
```

```python
import functools
import math

import numpy as np
import jax
import jax.numpy as jnp
from jax import lax
from jax.experimental import pallas as pl
from jax.experimental.pallas import tpu as pltpu

F32 = jnp.float32
BF16 = jnp.bfloat16

RMS_EPS = 1e-6
LANES = 128
BF16_ROWS = 16
DIFF_HEAD_DIM = 64
V_HEAD_DIM = 2 * DIFF_HEAD_DIM
REL_BUCKETS = 32
REL_MAX_DIST = 128
N_HYENA_GROUPS = 8
HYENA_EMB_DIM = 33
HYENA_DECAY_TARGET = 1e-2
HYENA_FAST_DECAY = 0.3
HYENA_SLOW_DECAY = 1.5
N_LO = 128
LOG2_E = math.log2(math.e)
VMEM_LIMIT = 56 * 1024 * 1024


def _params(*sem):
    return pltpu.CompilerParams(dimension_semantics=sem, vmem_limit_bytes=VMEM_LIMIT)


def _resident(shape):
    return pl.BlockSpec(shape, lambda *_: (0,) * len(shape), pipeline_mode=pl.Buffered(1))


def _rms(x, g):
    return x * lax.rsqrt(jnp.mean(x * x, axis=-1, keepdims=True) + RMS_EPS) * g


def _dot(a, b):
    return jnp.dot(a, b, preferred_element_type=F32)


def _split(a):
    hi = a.astype(BF16)
    return hi, (a - hi.astype(F32)).astype(BF16)


def _hyena_gate_norm(y_ref, vv_ref, x0_ref, fb_ref, og_ref, mg_ref):
    vv = vv_ref[...].astype(F32)
    y = (y_ref[...] + vv * fb_ref[...]) * x0_ref[...].astype(F32)
    hi, lo = _split(y * y)
    ms = _dot(hi, mg_ref[...]) + _dot(lo, mg_ref[...])
    return y * lax.rsqrt(ms + RMS_EPS) * og_ref[...]


N_STAGE = 4


def _stage_weights(srcs, dsts, stages, sem):
    jobs = [(src, dst, stage, c) for src, dst, stage in zip(srcs, dsts, stages) for c in range(N_STAGE)]

    def copy(k):
        src, _, stage, c = jobs[k]
        rows = stage.shape[1]
        return pltpu.make_async_copy(src.at[pl.ds(c * rows, rows), :], stage.at[k % 2], sem.at[k % 2])

    copy(0).start()
    for k, (_, dst, stage, c) in enumerate(jobs):
        if k + 1 < len(jobs):
            copy(k + 1).start()
        copy(k).wait()
        rows = stage.shape[1]
        dst[pl.ds(c * rows, rows), :] = stage[k % 2].astype(dst.dtype)


def _ffn_kernel(*refs, ff_chunk, n_chunk, pre, post):
    if pre:
        x_ref, att_ref, woa_ref, woh_ref = refs[:4]
        hy_refs = refs[4:10]
        refs = refs[10:]
    else:
        x_ref = refs[0]
        refs = refs[1:]
    g_ref, wg_hbm, wu_hbm, wd_hbm = refs[:4]
    refs = refs[4:]
    if post:
        fg_ref, o_ref = refs[:2]
        refs = refs[2:]
    else:
        o_ref = refs[0]
        refs = refs[1:]
    wg_ref, wu_ref, wd_ref, stage_in, stage_out, sem = refs

    @pl.when(pl.program_id(0) == 0)
    def _():
        _stage_weights((wg_hbm, wu_hbm, wd_hbm), (wg_ref, wu_ref, wd_ref),
                       (stage_in, stage_in, stage_out), sem)

    x = x_ref[...]
    if pre:
        hy = _hyena_gate_norm(*hy_refs).astype(BF16)
        x = x + _dot(att_ref[...], woa_ref[...]) + _dot(hy, woh_ref[...])
    xn = _rms(x, g_ref[...]).astype(BF16)
    acc = jnp.zeros(x.shape, F32)
    for c in range(n_chunk):
        sl = slice(c * ff_chunk, (c + 1) * ff_chunk)
        gate = _dot(xn, wg_ref[:, sl])
        up = _dot(xn, wu_ref[:, sl])
        h = (jax.nn.silu(gate) * up).astype(BF16)
        acc = acc + _dot(h, wd_ref[sl, :])
    y = x + 0.5 * acc
    if post:
        y = _rms(y, fg_ref[...])
    o_ref[...] = y


def _ffn(x, norm_g, wg, wu, wd, *, pre=None, final_g=None, tm=None, ff_chunk=256):
    m, d = x.shape
    dff = wg.shape[1]
    n_chunk = dff // ff_chunk
    if tm is None:
        tm = 1024 if pre is None and m % 1024 == 0 else 512
    assert n_chunk * ff_chunk == dff and m % tm == 0
    assert d % (BF16_ROWS * N_STAGE) == 0 and dff % (BF16_ROWS * N_STAGE) == 0
    row = lambda w: pl.BlockSpec((tm, w), lambda i: (i, 0))
    args, specs = [x], [row(d)]
    if pre is not None:
        att, woa, woh, (y, vv, x0, filt_bias, out_g) = pre
        c = y.shape[1]
        gdim = c // N_HYENA_GROUPS
        assert gdim & (gdim - 1) == 0
        grp = np.arange(c) // gdim
        mg = jnp.asarray((grp[:, None] == grp[None, :]).astype(np.float32) / gdim).astype(BF16)
        args += [att, woa, woh, y, vv, x0, filt_bias.reshape(1, c), out_g.reshape(1, c), mg]
        specs += [row(att.shape[1]), _resident(woa.shape), _resident(woh.shape), row(c), row(c), row(c),
                  _resident((1, c)), _resident((1, c)), _resident((c, c))]
    hbm = pl.BlockSpec(memory_space=pl.ANY)
    args += [norm_g.reshape(1, d), wg, wu, wd]
    specs += [_resident((1, d)), hbm, hbm, hbm]
    if final_g is not None:
        args.append(final_g.reshape(1, d))
        specs.append(_resident((1, d)))
    kern = functools.partial(_ffn_kernel, ff_chunk=ff_chunk, n_chunk=n_chunk,
                             pre=pre is not None, post=final_g is not None)
    return pl.pallas_call(
        kern, grid=(m // tm,), in_specs=specs, out_specs=row(d),
        out_shape=jax.ShapeDtypeStruct((m, d), F32),
        scratch_shapes=[pltpu.VMEM((d, dff), BF16), pltpu.VMEM((d, dff), BF16), pltpu.VMEM((dff, d), BF16),
                        pltpu.VMEM((2, d // N_STAGE, dff), F32), pltpu.VMEM((2, dff // N_STAGE, d), F32),
                        pltpu.SemaphoreType.DMA((2,))],
        compiler_params=_params("arbitrary"), name="ffn_pre" if pre is not None else "ffn",
    )(*args)


HALO = 8


def _mix_kernel(x_ref, xp_ref, xn_ref, g_ref, w_hbm, cw_ref, cb_ref, qkv_ref, vv_ref, x0_ref,
                w_ref, stage, sem, *, d_att, c, scale, tiles_per_seq, hy_chunk):
    i = pl.program_id(0)
    n_att = 3 * d_att

    @pl.when(i == 0)
    def _():
        _stage_weights((w_hbm,), (w_ref,), (stage,), sem)

    xn = _rms(x_ref[...], g_ref[...]).astype(BF16)
    halo = _rms(jnp.concatenate([xp_ref[...], xn_ref[...]], axis=0), g_ref[...]).astype(BF16)
    first = i % tiles_per_seq == 0
    last = i % tiles_per_seq == tiles_per_seq - 1
    rows = x_ref.shape[0]
    cw = hy_chunk
    r = lax.broadcasted_iota(jnp.int32, (HALO, cw), 0)
    head, tail = slice(0, HALO), slice(rows - HALO, rows)

    def conv_cols(part, k0):
        cs = slice(n_att + part * c + k0, n_att + part * c + k0 + cw)
        ws = slice(part * c + k0, part * c + k0 + cw)
        z = _dot(xn, w_ref[:, cs])
        zh = _dot(halo, w_ref[:, cs])
        prev_row = jnp.where(first, 0.0, zh[HALO - 1:HALO])
        next_row = jnp.where(last, 0.0, zh[HALO:HALO + 1])
        z_m1 = pltpu.roll(z, 1, 0)
        z_p1 = pltpu.roll(z, rows - 1, 0)
        conv = lambda zm, zc, zp: (cb_ref[:, ws] + zm * cw_ref[0:1, ws] + zc * cw_ref[1:2, ws]
                                   + zp * cw_ref[2:3, ws])
        mid = conv(z_m1, z, z_p1)
        top = conv(jnp.where(r == 0, prev_row, z_m1[head]), z[head], z_p1[head])
        bot = conv(z_m1[tail], z[tail], jnp.where(r == HALO - 1, next_row, z_p1[tail]))
        return jnp.concatenate([top, mid[HALO:rows - HALO], bot], axis=0)

    for k0 in range(0, c, cw):
        vv_ref[:, k0:k0 + cw] = (conv_cols(2, k0) * conv_cols(1, k0)).astype(vv_ref.dtype)
        x0_ref[:, k0:k0 + cw] = conv_cols(0, k0).astype(x0_ref.dtype)

    for c0 in range(0, n_att, d_att):
        p = _dot(xn, w_ref[:, c0:c0 + d_att])
        if c0 == 0:
            p = p * scale
        qkv_ref[:, c0:c0 + d_att] = p.astype(BF16)


def _mix(x, norm_g, w_in, conv_w, conv_b, *, d_att, seq, tm=512, hy_chunk=256):
    m, d = x.shape
    n_att = 3 * d_att
    c = (w_in.shape[1] - n_att) // 3
    assert m % tm == 0 and seq % tm == 0 and tm % HALO == 0 and d % (BF16_ROWS * N_STAGE) == 0
    kern = functools.partial(_mix_kernel, d_att=d_att, c=c, scale=DIFF_HEAD_DIM ** -0.5 * LOG2_E,
                             tiles_per_seq=seq // tm, hy_chunk=hy_chunk)
    sub = tm // HALO
    last_blk = m // HALO - 1
    row = lambda w: pl.BlockSpec((tm, w), lambda i: (i, 0))
    return pl.pallas_call(
        kern, grid=(m // tm,),
        in_specs=[row(d),
                  pl.BlockSpec((HALO, d), lambda i: (jnp.maximum(i * sub - 1, 0), 0)),
                  pl.BlockSpec((HALO, d), lambda i: (jnp.minimum((i + 1) * sub, last_blk), 0)),
                  _resident((1, d)), pl.BlockSpec(memory_space=pl.ANY), _resident(conv_w.shape),
                  _resident((1, 3 * c))],
        out_specs=[row(n_att), row(c), row(c)],
        out_shape=[jax.ShapeDtypeStruct((m, n_att), BF16), jax.ShapeDtypeStruct((m, c), BF16),
                   jax.ShapeDtypeStruct((m, c), BF16)],
        scratch_shapes=[pltpu.VMEM(w_in.shape, BF16), pltpu.VMEM((2, d // N_STAGE, w_in.shape[1]), F32),
                        pltpu.SemaphoreType.DMA((2,))],
        compiler_params=_params("arbitrary"), name="mix",
    )(x, x, x, norm_g.reshape(1, d), w_in, conv_w, conv_b.reshape(1, 3 * c))


N_BIAS_TILES = 5


def _bias_kernel(tab_ref, o_ref, *, t):
    h = pl.program_id(0)
    d = pl.program_id(1) - N_BIAS_TILES // 2
    half = REL_BUCKETS // 2
    max_exact = half // 2
    rel = lax.broadcasted_iota(jnp.int32, (8, 2 * t), 1) + (d - 1) * t
    ret = jnp.where(rel > 0, half, 0)
    n = jnp.abs(rel)
    nf = jnp.maximum(n, 1).astype(F32)
    large = max_exact + (jnp.log(nf / max_exact) / math.log(REL_MAX_DIST / max_exact)
                         * (half - max_exact)).astype(jnp.int32)
    large = jnp.minimum(large, half - 1)
    bucket = ret + jnp.where(n < max_exact, n, large)
    row = jnp.zeros(rel.shape, F32)
    for b in range(REL_BUCKETS):
        row = jnp.where(bucket == b, tab_ref[b, h], row)
    rows = jnp.broadcast_to(row[0:1, :] * LOG2_E, (t, 2 * t))
    o_ref[...] = pltpu.roll(rows, 0, 1, stride=1, stride_axis=0)[:, t:2 * t]


def _bias_tiles(rel_bias, t):
    nh = rel_bias.shape[1]
    return pl.pallas_call(
        functools.partial(_bias_kernel, t=t), grid=(nh, N_BIAS_TILES),
        in_specs=[pl.BlockSpec(memory_space=pltpu.SMEM)],
        out_specs=pl.BlockSpec((None, None, t, t), lambda h, d: (h, d, 0, 0)),
        out_shape=jax.ShapeDtypeStruct((nh, N_BIAS_TILES, t, t), F32),
        compiler_params=_params("parallel", "parallel"), name="rel_bias_tiles",
    )(rel_bias)


def _attn_kernel(q_ref, k_ref, v_ref, bt_ref, lq1_ref, lk1_ref, lq2_ref, lk2_ref, sg_ref,
                 o_ref, q2_ref, vx_ref, *, t, nk, n_q, n_chain, lambda_init):
    g = pl.program_id(2)
    dv = v_ref.shape[1]

    @pl.when(g == 0)
    def _():
        vx_ref[:, 0:dv] = v_ref[...]
        ones_lane = lax.broadcasted_iota(jnp.int32, (v_ref.shape[0], dv), 1) == 0
        vx_ref[:, dv:2 * dv] = jnp.where(ones_lane, 1.0, 0.0).astype(vx_ref.dtype)

    far = N_BIAS_TILES // 2
    rc = 2 * t // n_chain
    lam = (jnp.exp(jnp.sum(lq1_ref[...] * lk1_ref[...])) - jnp.exp(jnp.sum(lq2_ref[...] * lk2_ref[...]))
           + lambda_init)

    def scores(i, base, d, rows, n_keys=t):
        wrapped = i + d >= nk
        j = jnp.where(wrapped, i + d - nk, i + d)
        k = k_ref[pl.ds(pl.multiple_of(j * t, t), n_keys), :]
        q2 = q2_ref[base + rows.start:base + rows.stop, :]
        s = lax.dot_general(q2, k, (((1,), (1,)), ((), ())), preferred_element_type=F32)
        cols = [s[:, c0:c0 + LANES] for c0 in range(0, n_keys, LANES)]
        if d <= 1 or d >= nk - 1:
            qr = slice(rows.start % t, rows.start % t + rows.stop - rows.start)
            bias = bt_ref[jnp.clip(j - i, -far, far) + far, qr, :]
            return j, [c + bias[:, n * LANES:(n + 1) * LANES] for n, c in enumerate(cols)], None
        side = jnp.where(wrapped, bt_ref[0, 0:1, 0:LANES], bt_ref[2 * far, 0:1, 0:LANES])
        return j, cols, side

    def tile(i, base, d, rows, m_prev, acc_prev):
        j, cols, side = scores(i, base, d, rows)
        shift = m_prev if side is None else m_prev - side
        rel = [(col - shift).astype(BF16) for col in cols]
        rm = jnp.max(functools.reduce(jnp.maximum, rel), axis=1, keepdims=True)
        delta = jnp.maximum(rm, 0.0)
        p = jnp.concatenate([jnp.exp2(x - delta) for x in rel], axis=1)
        d32 = jnp.broadcast_to(delta.astype(F32), m_prev.shape)
        alpha = jnp.exp2(-d32)
        pv = _dot(p, vx_ref[pl.ds(pl.multiple_of(j * t, t), t), :])
        acc_new = jnp.concatenate([alpha * acc_prev[:, 0:dv] + pv[:, 0:dv],
                                   alpha * acc_prev[:, dv:2 * dv] + pv[:, dv:2 * dv]], axis=1)
        return m_prev + d32, acc_new

    streams = []
    for a in range(n_q):
        i, base = g * n_q + a, a * 2 * t
        q = q_ref[a * t:(a + 1) * t, :]
        lane = lax.broadcasted_iota(jnp.int32, q.shape, 1)
        zero = jnp.zeros_like(q)
        q2_ref[base:base + t, :] = jnp.where(lane < DIFF_HEAD_DIM, q, zero)
        q2_ref[base + t:base + 2 * t, :] = jnp.where(lane >= DIFF_HEAD_DIM, q, zero)
        for r0 in range(0, 2 * t, rc):
            streams.append((i, base, slice(r0, r0 + rc)))
    state = []
    for i, base, rows in streams:
        m0 = jnp.max(scores(i, base, 0, rows, LANES)[1][0], axis=1, keepdims=True)
        state.append((jnp.broadcast_to(m0, (rc, LANES)), jnp.zeros((rc, 2 * dv), F32)))
    for d in range(nk):
        state = [tile(i, base, d, rows, *st) for (i, base, rows), st in zip(streams, state)]

    for a in range(n_q):
        acc = jnp.concatenate([acc for _, acc in state[a * n_chain:(a + 1) * n_chain]], axis=0)
        o = acc[:, 0:dv] / acc[:, dv:dv + 1]
        o = o[0:t] - lam * o[t:2 * t]
        o = _rms(o, sg_ref[...]) * (1.0 - lambda_init)
        o_ref[a * t:(a + 1) * t, :] = o.astype(o_ref.dtype)


def _attention(qkv, bias_tiles, lq1, lk1, lq2, lk2, subln, *, n_heads, lambda_init, t, n_chain, n_q):
    b, s, _ = qkv.shape
    nk = s // t
    assert nk * t == s and t % LANES == 0 and t + 1 >= 91 and (2 * t) % n_chain == 0 and nk % n_q == 0
    kern = functools.partial(_attn_kernel, t=t, nk=nk, n_q=n_q, n_chain=n_chain, lambda_init=lambda_init)
    vec = lambda n: pl.BlockSpec((1, n), lambda b_, h, i: (0, 0))
    return pl.pallas_call(
        kern, grid=(b, n_heads, nk // n_q),
        in_specs=[
            pl.BlockSpec((None, n_q * t, V_HEAD_DIM), lambda b_, h, i: (b_, i, h)),
            pl.BlockSpec((None, s, V_HEAD_DIM), lambda b_, h, i: (b_, 0, n_heads + h)),
            pl.BlockSpec((None, s, V_HEAD_DIM), lambda b_, h, i: (b_, 0, 2 * n_heads + h)),
            pl.BlockSpec((None, N_BIAS_TILES, t, t), lambda b_, h, i: (h, 0, 0, 0)),
            vec(DIFF_HEAD_DIM), vec(DIFF_HEAD_DIM), vec(DIFF_HEAD_DIM), vec(DIFF_HEAD_DIM),
            vec(V_HEAD_DIM),
        ],
        out_specs=pl.BlockSpec((None, n_q * t, V_HEAD_DIM), lambda b_, h, i: (b_, i, h)),
        out_shape=jax.ShapeDtypeStruct((b, s, n_heads * V_HEAD_DIM), BF16),
        scratch_shapes=[pltpu.VMEM((n_q * 2 * t, V_HEAD_DIM), BF16),
                        pltpu.VMEM((s, 2 * V_HEAD_DIM), BF16)],
        compiler_params=_params("parallel", "parallel", "arbitrary"), name="diff_attention",
    )(qkv, qkv, qkv, bias_tiles, lq1.reshape(1, -1), lk1.reshape(1, -1),
      lq2.reshape(1, -1), lk2.reshape(1, -1), subln.reshape(1, -1))


def _dft_tables(n_hi):
    n = n_hi * N_LO
    half = n_hi // 2
    hi = np.arange(n_hi)
    lo = np.arange(N_LO)
    f_hi = np.exp(-2j * np.pi * ((np.outer(hi, hi) % n_hi) / n_hi))
    fr, fi = f_hi.real, f_hi.imag
    s1_pair = np.block([[fr[:, :half], -fi[:, :half]], [fi[:, :half], fr[:, :half]]])
    s1_real = np.concatenate([fr[:, :half], fi[:, :half]], axis=0)
    ph = (hi[:, None, None] * lo[None, None, :] + n_hi * lo[None, :, None] * lo[None, None, :]) % n
    g = np.exp(-2j * np.pi * ph / n)
    g2 = np.concatenate([np.concatenate([g.real, -g.imag], axis=2),
                         np.concatenate([g.imag, g.real], axis=2)], axis=1)
    gh = np.stack([g2, np.swapaxes(g2, 1, 2)], axis=1)
    s4 = np.block([[fr[:half], fi[:half]], [-fi[:half], fr[:half]]]) / n
    cast = lambda a: jnp.asarray(a.astype(np.float32)).astype(BF16)
    return cast(s1_pair), cast(s1_real), cast(gh), cast(s4)


def _hdot(a, b, dims=(((1,), (0,)), ((), ()))):
    (ah, al), (bh, bl) = _split(a), _split(b)
    dg = functools.partial(lax.dot_general, dimension_numbers=dims, preferred_element_type=F32)
    return dg(ah, bh) + dg(ah, bl) + dg(al, bh)


def _filter_kernel(zt_ref, t_ref, w1_ref, b1_ref, w2_ref, b2_ref, w3_ref, b3_ref, w4_ref, fr_ref, dl_ref,
                   hf_ref, hb_ref, *, c):
    i = pl.program_id(0)
    fr = fr_ref[...]
    a = jnp.sin(fr * (_hdot(w1_ref[...], zt_ref[...]) + b1_ref[...]))
    a = jnp.sin(fr * (_hdot(w2_ref[...], a) + b2_ref[...]))
    a = jnp.sin(fr * (_hdot(w3_ref[...], a) + b3_ref[...]))
    hh = _hdot(a, w4_ref[...], (((0,), (0,)), ((), ())))
    decay = jnp.exp(-t_ref[...] * jnp.abs(dl_ref[...]))
    hf_ref[...] = (hh[:, 0:c] * decay).astype(hf_ref.dtype)
    r = lax.broadcasted_iota(jnp.int32, (hh.shape[0], c), 0)
    lag0 = jnp.logical_and(i == 0, r == 0)
    hb_ref[...] = jnp.where(lag0, 0.0, hh[:, c:2 * c] * decay).astype(hb_ref.dtype)


def _filter_taps(z, w1, b1, w2, b2, w3, b3, w4, freq, deltas, *, tb=512):
    s, emb = z.shape
    c = deltas.shape[0]
    assert s % tb == 0
    col = lambda v: v.reshape(-1, 1)
    args = [jnp.pad(z, ((0, 0), (0, LANES - emb))).T, z[:, 0:1],
            jnp.pad(w1, ((0, LANES - emb), (0, 0))).T, col(b1), w2.T, col(b2), w3.T, col(b3), w4,
            col(freq), deltas.reshape(1, c)]
    specs = [pl.BlockSpec((LANES, tb), lambda i: (0, i)), pl.BlockSpec((tb, 1), lambda i: (i, 0))] + [
        pl.BlockSpec(a.shape, lambda i: (0, 0)) for a in args[2:]]
    out = pl.BlockSpec((tb, c), lambda i: (i, 0))
    shp = jax.ShapeDtypeStruct((s, c), BF16)
    return pl.pallas_call(
        functools.partial(_filter_kernel, c=c), grid=(s // tb,),
        in_specs=specs, out_specs=[out, out], out_shape=[shp, shp],
        compiler_params=_params("parallel"), name="hyena_filter_taps",
    )(*args)


def _lo_major(ref):
    return jnp.swapaxes(ref[...], 0, 1)


def _dft1_kernel(x0_ref, x1_ref, f_ref, a_ref):
    n_out = a_ref.shape[0]
    x0, x1 = _lo_major(x0_ref), _lo_major(x1_ref)
    a = jnp.stack([_dot(f_ref[...], jnp.concatenate([x0[j], x1[j]], axis=0)).astype(a_ref.dtype)
                   for j in range(x0.shape[0])], axis=0)
    a = jnp.swapaxes(a, 0, 1)
    a_ref[:, 0] = a[0:n_out]
    a_ref[:, 1] = a[n_out:2 * n_out]


def _dft1_pair(xv, f_mat, *, n_hi, t_lo=BF16_ROWS):
    b, half, _, c = xv.shape
    spec = lambda off: pl.BlockSpec((None, half, t_lo, c), lambda p, j: (2 * p + off, 0, j, 0))
    return pl.pallas_call(
        _dft1_kernel, grid=(b // 2, N_LO // t_lo),
        in_specs=[spec(0), spec(1), pl.BlockSpec(f_mat.shape, lambda p, j: (0, 0))],
        out_specs=pl.BlockSpec((None, n_hi, 2, t_lo, c), lambda p, j: (p, 0, 0, j, 0)),
        out_shape=jax.ShapeDtypeStruct((b // 2, n_hi, 2, N_LO, c), BF16),
        compiler_params=_params("parallel", "parallel"), name="hyena_dft1",
    )(xv, xv, f_mat)


def _dft1_filter_kernel(hf_ref, hb_ref, f_ref, fa_ref):
    n_out = fa_ref.shape[0]
    c = hf_ref.shape[2]
    hf, hb = _lo_major(hf_ref), _lo_major(hb_ref)
    a = jnp.stack([_dot(f_ref[...], jnp.concatenate([hf[j], hb[j]], axis=1)).astype(fa_ref.dtype)
                   for j in range(hf.shape[0])], axis=0)
    a = jnp.swapaxes(a, 0, 1)
    for part, (r0, c0) in enumerate(((0, 0), (n_out, 0), (0, c), (n_out, c))):
        fa_ref[:, part] = a[r0:r0 + n_out, :, c0:c0 + c]


def _dft1_filter(hf, hb, f_mat, *, n_hi, t_lo=BF16_ROWS):
    half, _, c = hf.shape
    blk = pl.BlockSpec((half, t_lo, c), lambda j: (0, j, 0))
    return pl.pallas_call(
        _dft1_filter_kernel, grid=(N_LO // t_lo,),
        in_specs=[blk, blk, pl.BlockSpec(f_mat.shape, lambda j: (0, 0))],
        out_specs=pl.BlockSpec((n_hi, 4, t_lo, c), lambda j: (0, 0, j, 0)),
        out_shape=jax.ShapeDtypeStruct((n_hi, 4, N_LO, c), BF16),
        compiler_params=_params("parallel"), name="hyena_filter_dft1",
    )(hf, hb, f_mat)


def _spectral_kernel(a_ref, fa_ref, gh_ref, cc_ref):
    c = a_ref.shape[-1]
    for u in range(a_ref.shape[0]):
        a = jnp.concatenate([a_ref[u].reshape(2 * N_LO, c), fa_ref[u, 0:2].reshape(2 * N_LO, c),
                             fa_ref[u, 2:4].reshape(2 * N_LO, c)], axis=1)
        x = _dot(gh_ref[u, 0], a)
        xr, xi = x[0:N_LO, 0:c], x[N_LO:2 * N_LO, 0:c]
        kr = x[0:N_LO, c:2 * c] + x[0:N_LO, 2 * c:3 * c]
        ki = x[N_LO:2 * N_LO, c:2 * c] - x[N_LO:2 * N_LO, 2 * c:3 * c]
        y = jnp.concatenate([xr * kr - xi * ki, xr * ki + xi * kr], axis=0).astype(BF16)
        cc_ref[u] = _dot(gh_ref[u, 1], y).astype(cc_ref.dtype).reshape(2, N_LO, c)


def _spectral(a, fa, gh, *, n_hi, kb=8):
    p, _, _, _, c = a.shape
    assert n_hi % kb == 0
    blk = pl.BlockSpec((None, kb, 2, N_LO, c), lambda q, k: (q, k, 0, 0, 0))
    return pl.pallas_call(
        _spectral_kernel, grid=(p, n_hi // kb),
        in_specs=[blk, pl.BlockSpec((kb, 4, N_LO, c), lambda q, k: (k, 0, 0, 0)),
                  pl.BlockSpec((kb, 2, 2 * N_LO, 2 * N_LO), lambda q, k: (k, 0, 0, 0))],
        out_specs=blk, out_shape=jax.ShapeDtypeStruct(a.shape, BF16),
        compiler_params=_params("parallel", "parallel"), name="hyena_spectral",
    )(a, fa, gh)


def _idft_kernel(cc_ref, m_ref, o_ref):
    half = o_ref.shape[1]
    cr, ci = jnp.swapaxes(cc_ref[:, 0], 0, 1), jnp.swapaxes(cc_ref[:, 1], 0, 1)
    y = jnp.stack([_dot(m_ref[...], jnp.concatenate([cr[j], ci[j]], axis=0)) for j in range(cr.shape[0])],
                  axis=0)
    y = jnp.swapaxes(y, 0, 1)
    o_ref[0] = y[0:half]
    o_ref[1] = y[half:2 * half]


def _idft(cc, s4, *, n_hi, t_lo=BF16_ROWS):
    p, _, _, _, c = cc.shape
    half = n_hi // 2
    return pl.pallas_call(
        _idft_kernel, grid=(p, N_LO // t_lo),
        in_specs=[pl.BlockSpec((None, n_hi, 2, t_lo, c), lambda q, j: (q, 0, 0, j, 0)),
                  pl.BlockSpec(s4.shape, lambda q, j: (0, 0))],
        out_specs=pl.BlockSpec((2, half, t_lo, c), lambda q, j: (q, 0, j, 0)),
        out_shape=jax.ShapeDtypeStruct((2 * p, half, N_LO, c), F32),
        compiler_params=_params("parallel", "parallel"), name="hyena_idft",
    )(cc, s4)


def _filter_positions(s):
    t = jnp.linspace(0.0, 1.0, s, dtype=F32)[:, None]
    bands = (HYENA_EMB_DIM - 1) // 2
    w = 2.0 * math.pi * jnp.arange(s, dtype=F32)[:, None] / s
    f = jnp.linspace(1e-4, bands - 1, bands, dtype=F32)[None, :]
    fw = f * w
    return jnp.concatenate([t, jnp.cos(fw), -jnp.sin(fw)], axis=-1)


def _hyena_conv(vv, w1, b1, w2, b2, w3, b3, w4, freq):
    b, s, c = vv.shape
    n_hi = 2 * s // N_LO
    half = n_hi // 2
    assert b % 2 == 0 and n_hi * N_LO == 2 * s
    s1_pair, s1_real, gh, s4 = _dft_tables(n_hi)

    max_decay = math.log(HYENA_DECAY_TARGET) / HYENA_FAST_DECAY
    min_decay = math.log(HYENA_DECAY_TARGET) / HYENA_SLOW_DECAY
    deltas = jnp.linspace(min_decay, max_decay, c, dtype=F32)
    hf, hb = _filter_taps(_filter_positions(s), w1, b1, w2, b2, w3, b3, w4, freq, deltas)
    fa = _dft1_filter(hf.reshape(half, N_LO, c), hb.reshape(half, N_LO, c), s1_real, n_hi=n_hi)

    a = _dft1_pair(vv.reshape(b, half, N_LO, c), s1_pair, n_hi=n_hi)
    cc = _spectral(a, fa, gh, n_hi=n_hi)
    return _idft(cc, s4, n_hi=n_hi).reshape(b, s, c)


def kernel(x, rel_bias, ffn1_norm, ffn1_w_gate, ffn1_w_up, ffn1_w_down, mix_norm, w_in,
           lambda_q1, lambda_k1, lambda_q2, lambda_k2, diff_subln,
           hy_conv_w, hy_conv_b, hy_f_w1, hy_f_b1, hy_f_w2, hy_f_b2, hy_f_w3, hy_f_b3,
           hy_f_w4, hy_f_freq, hy_bias, hy_out_norm, w_out,
           ffn2_norm, ffn2_w_gate, ffn2_w_up, ffn2_w_down, final_norm):
    b, s, d = x.shape
    depth = w_in.shape[0]
    d_att = diff_subln.shape[1] * rel_bias.shape[1]
    n_heads = rel_bias.shape[1]
    attn_tile = min(512, s)
    bf = lambda a: a.astype(BF16)

    xf = x.reshape(b * s, d)
    for l in range(depth):
        last = l == depth - 1
        xf = _ffn(xf, ffn1_norm[l], ffn1_w_gate[l], ffn1_w_up[l], ffn1_w_down[l])
        qkv, vv, x0 = _mix(xf, mix_norm[l], w_in[l], hy_conv_w[l], hy_conv_b[l], d_att=d_att, seq=s)
        lambda_init = 0.8 - 0.6 * math.exp(-0.3 * l)
        att = _attention(qkv.reshape(b, s, -1), _bias_tiles(rel_bias, attn_tile),
                         lambda_q1[l], lambda_k1[l], lambda_q2[l], lambda_k2[l], diff_subln[l],
                         n_heads=n_heads, lambda_init=lambda_init, t=attn_tile, n_chain=4,
                         n_q=2 if (s // attn_tile) % 2 == 0 else 1)
        y = _hyena_conv(vv.reshape(b, s, -1), hy_f_w1[l], hy_f_b1[l], hy_f_w2[l], hy_f_b2[l],
                        hy_f_w3[l], hy_f_b3[l], hy_f_w4[l], hy_f_freq[l])
        hy_parts = (y.reshape(b * s, -1), vv, x0, hy_bias[l], hy_out_norm[l])
        wo = bf(w_out[l])
        xf = _ffn(xf, ffn2_norm[l], ffn2_w_gate[l], ffn2_w_up[l], ffn2_w_down[l],
                  pre=(att.reshape(b * s, -1), wo[:d_att], wo[d_att:], hy_parts),
                  final_g=final_norm if last else None)
    if depth == 0:
        raise ValueError("depth must be positive")
    return xf.reshape(b, s, d)
```

```python
import functools
import math

import numpy as np
import jax
import jax.numpy as jnp
from jax import lax
from jax.experimental import pallas as pl
from jax.experimental.pallas import tpu as pltpu

F32 = jnp.float32
BF16 = jnp.bfloat16

RMS_EPS = 1e-6
LANES = 128
BF16_ROWS = 16
DIFF_HEAD_DIM = 64
V_HEAD_DIM = 2 * DIFF_HEAD_DIM
REL_BUCKETS = 32
REL_MAX_DIST = 128
N_HYENA_GROUPS = 8
HYENA_EMB_DIM = 33
HYENA_DECAY_TARGET = 1e-2
HYENA_FAST_DECAY = 0.3
HYENA_SLOW_DECAY = 1.5
N_LO = 128
LOG2_E = math.log2(math.e)
VMEM_LIMIT = 56 * 1024 * 1024


def _params(*sem):
    return pltpu.CompilerParams(dimension_semantics=sem, vmem_limit_bytes=VMEM_LIMIT)


def _resident(shape):
    return pl.BlockSpec(shape, lambda *_: (0,) * len(shape), pipeline_mode=pl.Buffered(1))


def _rms(x, g):
    return x * lax.rsqrt(jnp.mean(x * x, axis=-1, keepdims=True) + RMS_EPS) * g


def _dot(a, b):
    return jnp.dot(a, b, preferred_element_type=F32)


def _split(a):
    hi = a.astype(BF16)
    return hi, (a - hi.astype(F32)).astype(BF16)


def _hyena_gate_norm(y_ref, vv_ref, x0_ref, fb_ref, og_ref, mg_ref):
    vv = vv_ref[...].astype(F32)
    y = (y_ref[...] + vv * fb_ref[...]) * x0_ref[...].astype(F32)
    hi, lo = _split(y * y)
    ms = _dot(hi, mg_ref[...]) + _dot(lo, mg_ref[...])
    return y * lax.rsqrt(ms + RMS_EPS) * og_ref[...]


N_STAGE = 4


def _stage_weights(srcs, dsts, stages, sem):
    jobs = [(src, dst, stage, c) for src, dst, stage in zip(srcs, dsts, stages) for c in range(N_STAGE)]

    def copy(k):
        src, _, stage, c = jobs[k]
        rows = stage.shape[1]
        return pltpu.make_async_copy(src.at[pl.ds(c * rows, rows), :], stage.at[k % 2], sem.at[k % 2])

    copy(0).start()
    for k, (_, dst, stage, c) in enumerate(jobs):
        if k + 1 < len(jobs):
            copy(k + 1).start()
        copy(k).wait()
        rows = stage.shape[1]
        dst[pl.ds(c * rows, rows), :] = stage[k % 2].astype(dst.dtype)


def _ffn_kernel(*refs, ff_chunk, n_chunk, pre, post):
    if pre:
        x_ref, att_ref, woa_ref, woh_ref = refs[:4]
        hy_refs = refs[4:10]
        refs = refs[10:]
    else:
        x_ref = refs[0]
        refs = refs[1:]
    g_ref, wg_hbm, wu_hbm, wd_hbm = refs[:4]
    refs = refs[4:]
    if post:
        fg_ref, o_ref = refs[:2]
        refs = refs[2:]
    else:
        o_ref = refs[0]
        refs = refs[1:]
    wg_ref, wu_ref, wd_ref, stage_in, stage_out, sem = refs

    @pl.when(pl.program_id(0) == 0)
    def _():
        _stage_weights((wg_hbm, wu_hbm, wd_hbm), (wg_ref, wu_ref, wd_ref),
                       (stage_in, stage_in, stage_out), sem)

    x = x_ref[...]
    if pre:
        hy = _hyena_gate_norm(*hy_refs).astype(BF16)
        x = x + _dot(att_ref[...], woa_ref[...]) + _dot(hy, woh_ref[...])
    xn = _rms(x, g_ref[...]).astype(BF16)
    acc = jnp.zeros(x.shape, F32)
    for c in range(n_chunk):
        sl = slice(c * ff_chunk, (c + 1) * ff_chunk)
        gate = _dot(xn, wg_ref[:, sl])
        up = _dot(xn, wu_ref[:, sl])
        h = (jax.nn.silu(gate) * up).astype(BF16)
        acc = acc + _dot(h, wd_ref[sl, :])
    y = x + 0.5 * acc
    if post:
        y = _rms(y, fg_ref[...])
    o_ref[...] = y


def _ffn(x, norm_g, wg, wu, wd, *, pre=None, final_g=None, tm=None, ff_chunk=256):
    m, d = x.shape
    dff = wg.shape[1]
    n_chunk = dff // ff_chunk
    if tm is None:
        tm = 1024 if pre is None and m % 1024 == 0 else 512
    assert n_chunk * ff_chunk == dff and m % tm == 0
    assert d % (BF16_ROWS * N_STAGE) == 0 and dff % (BF16_ROWS * N_STAGE) == 0
    row = lambda w: pl.BlockSpec((tm, w), lambda i: (i, 0))
    args, specs = [x], [row(d)]
    if pre is not None:
        att, woa, woh, (y, vv, x0, filt_bias, out_g) = pre
        c = y.shape[1]
        gdim = c // N_HYENA_GROUPS
        assert gdim & (gdim - 1) == 0
        grp = np.arange(c) // gdim
        mg = jnp.asarray((grp[:, None] == grp[None, :]).astype(np.float32) / gdim).astype(BF16)
        args += [att, woa, woh, y, vv, x0, filt_bias.reshape(1, c), out_g.reshape(1, c), mg]
        specs += [row(att.shape[1]), _resident(woa.shape), _resident(woh.shape), row(c), row(c), row(c),
                  _resident((1, c)), _resident((1, c)), _resident((c, c))]
    hbm = pl.BlockSpec(memory_space=pl.ANY)
    args += [norm_g.reshape(1, d), wg, wu, wd]
    specs += [_resident((1, d)), hbm, hbm, hbm]
    if final_g is not None:
        args.append(final_g.reshape(1, d))
        specs.append(_resident((1, d)))
    kern = functools.partial(_ffn_kernel, ff_chunk=ff_chunk, n_chunk=n_chunk,
                             pre=pre is not None, post=final_g is not None)
    return pl.pallas_call(
        kern, grid=(m // tm,), in_specs=specs, out_specs=row(d),
        out_shape=jax.ShapeDtypeStruct((m, d), F32),
        scratch_shapes=[pltpu.VMEM((d, dff), BF16), pltpu.VMEM((d, dff), BF16), pltpu.VMEM((dff, d), BF16),
                        pltpu.VMEM((2, d // N_STAGE, dff), F32), pltpu.VMEM((2, dff // N_STAGE, d), F32),
                        pltpu.SemaphoreType.DMA((2,))],
        compiler_params=_params("arbitrary"), name="ffn_pre" if pre is not None else "ffn",
    )(*args)


HALO = 8


def _mix_kernel(x_ref, xp_ref, xn_ref, g_ref, w_hbm, cw_ref, cb_ref, qkv_ref, vv_ref, x0_ref,
                w_ref, stage, sem, *, d_att, c, scale, tiles_per_seq, hy_chunk):
    i = pl.program_id(0)
    n_att = 3 * d_att

    @pl.when(i == 0)
    def _():
        _stage_weights((w_hbm,), (w_ref,), (stage,), sem)

    xn = _rms(x_ref[...], g_ref[...]).astype(BF16)
    halo = _rms(jnp.concatenate([xp_ref[...], xn_ref[...]], axis=0), g_ref[...]).astype(BF16)
    first = i % tiles_per_seq == 0
    last = i % tiles_per_seq == tiles_per_seq - 1
    rows = x_ref.shape[0]
    cw = hy_chunk
    r = lax.broadcasted_iota(jnp.int32, (HALO, cw), 0)
    head, tail = slice(0, HALO), slice(rows - HALO, rows)

    def conv_cols(part, k0):
        cs = slice(n_att + part * c + k0, n_att + part * c + k0 + cw)
        ws = slice(part * c + k0, part * c + k0 + cw)
        z = _dot(xn, w_ref[:, cs])
        zh = _dot(halo, w_ref[:, cs])
        prev_row = jnp.where(first, 0.0, zh[HALO - 1:HALO])
        next_row = jnp.where(last, 0.0, zh[HALO:HALO + 1])
        z_m1 = pltpu.roll(z, 1, 0)
        z_p1 = pltpu.roll(z, rows - 1, 0)
        conv = lambda zm, zc, zp: (cb_ref[:, ws] + zm * cw_ref[0:1, ws] + zc * cw_ref[1:2, ws]
                                   + zp * cw_ref[2:3, ws])
        mid = conv(z_m1, z, z_p1)
        top = conv(jnp.where(r == 0, prev_row, z_m1[head]), z[head], z_p1[head])
        bot = conv(z_m1[tail], z[tail], jnp.where(r == HALO - 1, next_row, z_p1[tail]))
        return jnp.concatenate([top, mid[HALO:rows - HALO], bot], axis=0)

    for k0 in range(0, c, cw):
        vv_ref[:, k0:k0 + cw] = (conv_cols(2, k0) * conv_cols(1, k0)).astype(vv_ref.dtype)
        x0_ref[:, k0:k0 + cw] = conv_cols(0, k0).astype(x0_ref.dtype)

    for c0 in range(0, n_att, d_att):
        p = _dot(xn, w_ref[:, c0:c0 + d_att])
        if c0 == 0:
            p = p * scale
        qkv_ref[:, c0:c0 + d_att] = p.astype(BF16)


def _mix(x, norm_g, w_in, conv_w, conv_b, *, d_att, seq, tm=None, hy_chunk=256):
    m, d = x.shape
    if tm is None:
        tm = 1024 if seq % 1024 == 0 else 512
    n_att = 3 * d_att
    c = (w_in.shape[1] - n_att) // 3
    assert m % tm == 0 and seq % tm == 0 and tm % HALO == 0 and d % (BF16_ROWS * N_STAGE) == 0
    kern = functools.partial(_mix_kernel, d_att=d_att, c=c, scale=DIFF_HEAD_DIM ** -0.5 * LOG2_E,
                             tiles_per_seq=seq // tm, hy_chunk=hy_chunk)
    sub = tm // HALO
    last_blk = m // HALO - 1
    row = lambda w: pl.BlockSpec((tm, w), lambda i: (i, 0))
    return pl.pallas_call(
        kern, grid=(m // tm,),
        in_specs=[row(d),
                  pl.BlockSpec((HALO, d), lambda i: (jnp.maximum(i * sub - 1, 0), 0)),
                  pl.BlockSpec((HALO, d), lambda i: (jnp.minimum((i + 1) * sub, last_blk), 0)),
                  _resident((1, d)), pl.BlockSpec(memory_space=pl.ANY), _resident(conv_w.shape),
                  _resident((1, 3 * c))],
        out_specs=[row(n_att), row(c), row(c)],
        out_shape=[jax.ShapeDtypeStruct((m, n_att), BF16), jax.ShapeDtypeStruct((m, c), BF16),
                   jax.ShapeDtypeStruct((m, c), BF16)],
        scratch_shapes=[pltpu.VMEM(w_in.shape, BF16), pltpu.VMEM((2, d // N_STAGE, w_in.shape[1]), F32),
                        pltpu.SemaphoreType.DMA((2,))],
        compiler_params=_params("arbitrary"), name="mix",
    )(x, x, x, norm_g.reshape(1, d), w_in, conv_w, conv_b.reshape(1, 3 * c))


N_BIAS_TILES = 5


def _bias_kernel(tab_ref, o_ref, *, t):
    h = pl.program_id(0)
    d = pl.program_id(1) - N_BIAS_TILES // 2
    half = REL_BUCKETS // 2
    max_exact = half // 2
    rel = lax.broadcasted_iota(jnp.int32, (8, 2 * t), 1) + (d - 1) * t
    ret = jnp.where(rel > 0, half, 0)
    n = jnp.abs(rel)
    nf = jnp.maximum(n, 1).astype(F32)
    large = max_exact + (jnp.log(nf / max_exact) / math.log(REL_MAX_DIST / max_exact)
                         * (half - max_exact)).astype(jnp.int32)
    large = jnp.minimum(large, half - 1)
    bucket = ret + jnp.where(n < max_exact, n, large)
    row = jnp.zeros(rel.shape, F32)
    for b in range(REL_BUCKETS):
        row = jnp.where(bucket == b, tab_ref[b, h], row)
    rows = jnp.broadcast_to(row[0:1, :] * LOG2_E, (t, 2 * t))
    o_ref[...] = pltpu.roll(rows, 0, 1, stride=1, stride_axis=0)[:, t:2 * t]


def _bias_tiles(rel_bias, t):
    nh = rel_bias.shape[1]
    return pl.pallas_call(
        functools.partial(_bias_kernel, t=t), grid=(nh, N_BIAS_TILES),
        in_specs=[pl.BlockSpec(memory_space=pltpu.SMEM)],
        out_specs=pl.BlockSpec((None, None, t, t), lambda h, d: (h, d, 0, 0)),
        out_shape=jax.ShapeDtypeStruct((nh, N_BIAS_TILES, t, t), F32),
        compiler_params=_params("parallel", "parallel"), name="rel_bias_tiles",
    )(rel_bias)


def _attn_kernel(q_ref, k_ref, v_ref, bt_ref, lq1_ref, lk1_ref, lq2_ref, lk2_ref, sg_ref,
                 o_ref, q2_ref, vx_ref, *, t, nk, n_q, n_chain, lambda_init):
    g = pl.program_id(2)
    dv = v_ref.shape[1]

    @pl.when(g == 0)
    def _():
        vx_ref[:, 0:dv] = v_ref[...]
        ones_lane = lax.broadcasted_iota(jnp.int32, (v_ref.shape[0], dv), 1) == 0
        vx_ref[:, dv:2 * dv] = jnp.where(ones_lane, 1.0, 0.0).astype(vx_ref.dtype)

    far = N_BIAS_TILES // 2
    rc = 2 * t // n_chain
    lam = (jnp.exp(jnp.sum(lq1_ref[...] * lk1_ref[...])) - jnp.exp(jnp.sum(lq2_ref[...] * lk2_ref[...]))
           + lambda_init)

    def scores(i, base, d, rows, n_keys=t):
        wrapped = i + d >= nk
        j = jnp.where(wrapped, i + d - nk, i + d)
        k = k_ref[pl.ds(pl.multiple_of(j * t, t), n_keys), :]
        q2 = q2_ref[base + rows.start:base + rows.stop, :]
        s = lax.dot_general(q2, k, (((1,), (1,)), ((), ())), preferred_element_type=F32)
        cols = [s[:, c0:c0 + LANES] for c0 in range(0, n_keys, LANES)]
        if d <= 1 or d >= nk - 1:
            qr = slice(rows.start % t, rows.start % t + rows.stop - rows.start)
            bias = bt_ref[jnp.clip(j - i, -far, far) + far, qr, :]
            return j, [c + bias[:, n * LANES:(n + 1) * LANES] for n, c in enumerate(cols)], None
        side = jnp.where(wrapped, bt_ref[0, 0:1, 0:LANES], bt_ref[2 * far, 0:1, 0:LANES])
        return j, cols, side

    def tile(i, base, d, rows, m_prev, acc_prev):
        j, cols, side = scores(i, base, d, rows)
        shift = m_prev if side is None else m_prev - side
        rel = [(col - shift).astype(BF16) for col in cols]
        rm = jnp.max(functools.reduce(jnp.maximum, rel), axis=1, keepdims=True)
        delta = jnp.maximum(rm, 0.0)
        p = jnp.concatenate([jnp.exp2(x - delta) for x in rel], axis=1)
        d32 = jnp.broadcast_to(delta.astype(F32), m_prev.shape)
        alpha = jnp.exp2(-d32)
        pv = _dot(p, vx_ref[pl.ds(pl.multiple_of(j * t, t), t), :])
        acc_new = jnp.concatenate([alpha * acc_prev[:, 0:dv] + pv[:, 0:dv],
                                   alpha * acc_prev[:, dv:2 * dv] + pv[:, dv:2 * dv]], axis=1)
        return m_prev + d32, acc_new

    streams = []
    for a in range(n_q):
        i, base = g * n_q + a, a * 2 * t
        q = q_ref[a * t:(a + 1) * t, :]
        lane = lax.broadcasted_iota(jnp.int32, q.shape, 1)
        zero = jnp.zeros_like(q)
        q2_ref[base:base + t, :] = jnp.where(lane < DIFF_HEAD_DIM, q, zero)
        q2_ref[base + t:base + 2 * t, :] = jnp.where(lane >= DIFF_HEAD_DIM, q, zero)
        for r0 in range(0, 2 * t, rc):
            streams.append((i, base, slice(r0, r0 + rc)))
    state = []
    for i, base, rows in streams:
        m0 = jnp.max(scores(i, base, 0, rows, LANES)[1][0], axis=1, keepdims=True)
        state.append((jnp.broadcast_to(m0, (rc, LANES)), jnp.zeros((rc, 2 * dv), F32)))
    for d in range(nk):
        state = [tile(i, base, d, rows, *st) for (i, base, rows), st in zip(streams, state)]

    for a in range(n_q):
        acc = jnp.concatenate([acc for _, acc in state[a * n_chain:(a + 1) * n_chain]], axis=0)
        o = acc[:, 0:dv] / acc[:, dv:dv + 1]
        o = o[0:t] - lam * o[t:2 * t]
        o = _rms(o, sg_ref[...]) * (1.0 - lambda_init)
        o_ref[a * t:(a + 1) * t, :] = o.astype(o_ref.dtype)


def _attention(qkv, bias_tiles, lq1, lk1, lq2, lk2, subln, *, n_heads, lambda_init, t, n_chain, n_q):
    b, s, _ = qkv.shape
    nk = s // t
    assert nk * t == s and t % LANES == 0 and t + 1 >= 91 and (2 * t) % n_chain == 0 and nk % n_q == 0
    kern = functools.partial(_attn_kernel, t=t, nk=nk, n_q=n_q, n_chain=n_chain, lambda_init=lambda_init)
    vec = lambda n: pl.BlockSpec((1, n), lambda b_, h, i: (0, 0))
    return pl.pallas_call(
        kern, grid=(b, n_heads, nk // n_q),
        in_specs=[
            pl.BlockSpec((None, n_q * t, V_HEAD_DIM), lambda b_, h, i: (b_, i, h)),
            pl.BlockSpec((None, s, V_HEAD_DIM), lambda b_, h, i: (b_, 0, n_heads + h)),
            pl.BlockSpec((None, s, V_HEAD_DIM), lambda b_, h, i: (b_, 0, 2 * n_heads + h)),
            pl.BlockSpec((None, N_BIAS_TILES, t, t), lambda b_, h, i: (h, 0, 0, 0)),
            vec(DIFF_HEAD_DIM), vec(DIFF_HEAD_DIM), vec(DIFF_HEAD_DIM), vec(DIFF_HEAD_DIM),
            vec(V_HEAD_DIM),
        ],
        out_specs=pl.BlockSpec((None, n_q * t, V_HEAD_DIM), lambda b_, h, i: (b_, i, h)),
        out_shape=jax.ShapeDtypeStruct((b, s, n_heads * V_HEAD_DIM), BF16),
        scratch_shapes=[pltpu.VMEM((n_q * 2 * t, V_HEAD_DIM), BF16),
                        pltpu.VMEM((s, 2 * V_HEAD_DIM), BF16)],
        compiler_params=_params("parallel", "parallel", "arbitrary"), name="diff_attention",
    )(qkv, qkv, qkv, bias_tiles, lq1.reshape(1, -1), lk1.reshape(1, -1),
      lq2.reshape(1, -1), lk2.reshape(1, -1), subln.reshape(1, -1))


def _dft_tables(n_hi):
    n = n_hi * N_LO
    half = n_hi // 2
    hi = np.arange(n_hi)
    lo = np.arange(N_LO)
    f_hi = np.exp(-2j * np.pi * ((np.outer(hi, hi) % n_hi) / n_hi))
    fr, fi = f_hi.real, f_hi.imag
    s1_pair = np.block([[fr[:, :half], -fi[:, :half]], [fi[:, :half], fr[:, :half]]])
    s1_real = np.concatenate([fr[:, :half], fi[:, :half]], axis=0)
    ph = (hi[:, None, None] * lo[None, None, :] + n_hi * lo[None, :, None] * lo[None, None, :]) % n
    g = np.exp(-2j * np.pi * ph / n)
    g2 = np.concatenate([np.concatenate([g.real, -g.imag], axis=2),
                         np.concatenate([g.imag, g.real], axis=2)], axis=1)
    gh = np.stack([g2, np.swapaxes(g2, 1, 2)], axis=1)
    s4 = np.block([[fr[:half], fi[:half]], [-fi[:half], fr[:half]]]) / n
    cast = lambda a: jnp.asarray(a.astype(np.float32)).astype(BF16)
    return cast(s1_pair), cast(s1_real), cast(gh), cast(s4)


def _hdot(a, b, dims=(((1,), (0,)), ((), ()))):
    (ah, al), (bh, bl) = _split(a), _split(b)
    dg = functools.partial(lax.dot_general, dimension_numbers=dims, preferred_element_type=F32)
    return dg(ah, bh) + dg(ah, bl) + dg(al, bh)


def _filter_kernel(zt_ref, t_ref, w1_ref, b1_ref, w2_ref, b2_ref, w3_ref, b3_ref, w4_ref, fr_ref, dl_ref,
                   hf_ref, hb_ref, *, c):
    i = pl.program_id(0)
    fr = fr_ref[...]
    a = jnp.sin(fr * (_hdot(w1_ref[...], zt_ref[...]) + b1_ref[...]))
    a = jnp.sin(fr * (_hdot(w2_ref[...], a) + b2_ref[...]))
    a = jnp.sin(fr * (_hdot(w3_ref[...], a) + b3_ref[...]))
    hh = _hdot(a, w4_ref[...], (((0,), (0,)), ((), ())))
    decay = jnp.exp(-t_ref[...] * jnp.abs(dl_ref[...]))
    hf_ref[...] = (hh[:, 0:c] * decay).astype(hf_ref.dtype)
    r = lax.broadcasted_iota(jnp.int32, (hh.shape[0], c), 0)
    lag0 = jnp.logical_and(i == 0, r == 0)
    hb_ref[...] = jnp.where(lag0, 0.0, hh[:, c:2 * c] * decay).astype(hb_ref.dtype)


def _filter_taps(z, w1, b1, w2, b2, w3, b3, w4, freq, deltas, *, tb=None):
    s, emb = z.shape
    if tb is None:
        tb = 1024 if s % 1024 == 0 else 512
    c = deltas.shape[0]
    assert s % tb == 0
    col = lambda v: v.reshape(-1, 1)
    args = [jnp.pad(z, ((0, 0), (0, LANES - emb))).T, z[:, 0:1],
            jnp.pad(w1, ((0, LANES - emb), (0, 0))).T, col(b1), w2.T, col(b2), w3.T, col(b3), w4,
            col(freq), deltas.reshape(1, c)]
    specs = [pl.BlockSpec((LANES, tb), lambda i: (0, i)), pl.BlockSpec((tb, 1), lambda i: (i, 0))] + [
        pl.BlockSpec(a.shape, lambda i: (0, 0)) for a in args[2:]]
    out = pl.BlockSpec((tb, c), lambda i: (i, 0))
    shp = jax.ShapeDtypeStruct((s, c), BF16)
    return pl.pallas_call(
        functools.partial(_filter_kernel, c=c), grid=(s // tb,),
        in_specs=specs, out_specs=[out, out], out_shape=[shp, shp],
        compiler_params=_params("parallel"), name="hyena_filter_taps",
    )(*args)


def _lo_major(ref):
    return jnp.swapaxes(ref[...], 0, 1)


def _dft1_kernel(x0_ref, x1_ref, f_ref, a_ref):
    n_out = a_ref.shape[0]
    x0, x1 = _lo_major(x0_ref), _lo_major(x1_ref)
    a = jnp.stack([_dot(f_ref[...], jnp.concatenate([x0[j], x1[j]], axis=0)).astype(a_ref.dtype)
                   for j in range(x0.shape[0])], axis=0)
    a = jnp.swapaxes(a, 0, 1)
    a_ref[:, 0] = a[0:n_out]
    a_ref[:, 1] = a[n_out:2 * n_out]


def _dft1_pair(xv, f_mat, *, n_hi, t_lo=BF16_ROWS):
    b, half, _, c = xv.shape
    spec = lambda off: pl.BlockSpec((None, half, t_lo, c), lambda p, j: (2 * p + off, 0, j, 0))
    return pl.pallas_call(
        _dft1_kernel, grid=(b // 2, N_LO // t_lo),
        in_specs=[spec(0), spec(1), pl.BlockSpec(f_mat.shape, lambda p, j: (0, 0))],
        out_specs=pl.BlockSpec((None, n_hi, 2, t_lo, c), lambda p, j: (p, 0, 0, j, 0)),
        out_shape=jax.ShapeDtypeStruct((b // 2, n_hi, 2, N_LO, c), BF16),
        compiler_params=_params("parallel", "parallel"), name="hyena_dft1",
    )(xv, xv, f_mat)


def _dft1_filter_kernel(hf_ref, hb_ref, f_ref, fa_ref):
    n_out = fa_ref.shape[0]
    c = hf_ref.shape[2]
    hf, hb = _lo_major(hf_ref), _lo_major(hb_ref)
    a = jnp.stack([_dot(f_ref[...], jnp.concatenate([hf[j], hb[j]], axis=1)).astype(fa_ref.dtype)
                   for j in range(hf.shape[0])], axis=0)
    a = jnp.swapaxes(a, 0, 1)
    for part, (r0, c0) in enumerate(((0, 0), (n_out, 0), (0, c), (n_out, c))):
        fa_ref[:, part] = a[r0:r0 + n_out, :, c0:c0 + c]


def _dft1_filter(hf, hb, f_mat, *, n_hi, t_lo=BF16_ROWS):
    half, _, c = hf.shape
    blk = pl.BlockSpec((half, t_lo, c), lambda j: (0, j, 0))
    return pl.pallas_call(
        _dft1_filter_kernel, grid=(N_LO // t_lo,),
        in_specs=[blk, blk, pl.BlockSpec(f_mat.shape, lambda j: (0, 0))],
        out_specs=pl.BlockSpec((n_hi, 4, t_lo, c), lambda j: (0, 0, j, 0)),
        out_shape=jax.ShapeDtypeStruct((n_hi, 4, N_LO, c), BF16),
        compiler_params=_params("parallel"), name="hyena_filter_dft1",
    )(hf, hb, f_mat)


def _spectral_kernel(a_ref, fa_ref, gh_ref, cc_ref):
    c = a_ref.shape[-1]
    for u in range(a_ref.shape[0]):
        a = jnp.concatenate([a_ref[u].reshape(2 * N_LO, c), fa_ref[u, 0:2].reshape(2 * N_LO, c),
                             fa_ref[u, 2:4].reshape(2 * N_LO, c)], axis=1)
        x = _dot(gh_ref[u, 0], a)
        xr, xi = x[0:N_LO, 0:c], x[N_LO:2 * N_LO, 0:c]
        kr = x[0:N_LO, c:2 * c] + x[0:N_LO, 2 * c:3 * c]
        ki = x[N_LO:2 * N_LO, c:2 * c] - x[N_LO:2 * N_LO, 2 * c:3 * c]
        y = jnp.concatenate([xr * kr - xi * ki, xr * ki + xi * kr], axis=0).astype(BF16)
        cc_ref[u] = _dot(gh_ref[u, 1], y).astype(cc_ref.dtype).reshape(2, N_LO, c)


def _spectral(a, fa, gh, *, n_hi, kb=8):
    p, _, _, _, c = a.shape
    assert n_hi % kb == 0
    blk = pl.BlockSpec((None, kb, 2, N_LO, c), lambda q, k: (q, k, 0, 0, 0))
    return pl.pallas_call(
        _spectral_kernel, grid=(p, n_hi // kb),
        in_specs=[blk, pl.BlockSpec((kb, 4, N_LO, c), lambda q, k: (k, 0, 0, 0)),
                  pl.BlockSpec((kb, 2, 2 * N_LO, 2 * N_LO), lambda q, k: (k, 0, 0, 0))],
        out_specs=blk, out_shape=jax.ShapeDtypeStruct(a.shape, BF16),
        compiler_params=_params("parallel", "parallel"), name="hyena_spectral",
    )(a, fa, gh)


def _idft_kernel(cc_ref, m_ref, o_ref):
    half = o_ref.shape[1]
    cr, ci = jnp.swapaxes(cc_ref[:, 0], 0, 1), jnp.swapaxes(cc_ref[:, 1], 0, 1)
    y = jnp.stack([_dot(m_ref[...], jnp.concatenate([cr[j], ci[j]], axis=0)) for j in range(cr.shape[0])],
                  axis=0)
    y = jnp.swapaxes(y, 0, 1)
    o_ref[0] = y[0:half]
    o_ref[1] = y[half:2 * half]


def _idft(cc, s4, *, n_hi, t_lo=BF16_ROWS):
    p, _, _, _, c = cc.shape
    half = n_hi // 2
    return pl.pallas_call(
        _idft_kernel, grid=(p, N_LO // t_lo),
        in_specs=[pl.BlockSpec((None, n_hi, 2, t_lo, c), lambda q, j: (q, 0, 0, j, 0)),
                  pl.BlockSpec(s4.shape, lambda q, j: (0, 0))],
        out_specs=pl.BlockSpec((2, half, t_lo, c), lambda q, j: (q, 0, j, 0)),
        out_shape=jax.ShapeDtypeStruct((2 * p, half, N_LO, c), F32),
        compiler_params=_params("parallel", "parallel"), name="hyena_idft",
    )(cc, s4)


def _filter_positions(s):
    t = jnp.linspace(0.0, 1.0, s, dtype=F32)[:, None]
    bands = (HYENA_EMB_DIM - 1) // 2
    w = 2.0 * math.pi * jnp.arange(s, dtype=F32)[:, None] / s
    f = jnp.linspace(1e-4, bands - 1, bands, dtype=F32)[None, :]
    fw = f * w
    return jnp.concatenate([t, jnp.cos(fw), -jnp.sin(fw)], axis=-1)


def _hyena_conv(vv, w1, b1, w2, b2, w3, b3, w4, freq):
    b, s, c = vv.shape
    n_hi = 2 * s // N_LO
    half = n_hi // 2
    assert b % 2 == 0 and n_hi * N_LO == 2 * s
    s1_pair, s1_real, gh, s4 = _dft_tables(n_hi)

    max_decay = math.log(HYENA_DECAY_TARGET) / HYENA_FAST_DECAY
    min_decay = math.log(HYENA_DECAY_TARGET) / HYENA_SLOW_DECAY
    deltas = jnp.linspace(min_decay, max_decay, c, dtype=F32)
    hf, hb = _filter_taps(_filter_positions(s), w1, b1, w2, b2, w3, b3, w4, freq, deltas)
    fa = _dft1_filter(hf.reshape(half, N_LO, c), hb.reshape(half, N_LO, c), s1_real, n_hi=n_hi)

    a = _dft1_pair(vv.reshape(b, half, N_LO, c), s1_pair, n_hi=n_hi)
    cc = _spectral(a, fa, gh, n_hi=n_hi)
    return _idft(cc, s4, n_hi=n_hi).reshape(b, s, c)


def kernel(x, rel_bias, ffn1_norm, ffn1_w_gate, ffn1_w_up, ffn1_w_down, mix_norm, w_in,
           lambda_q1, lambda_k1, lambda_q2, lambda_k2, diff_subln,
           hy_conv_w, hy_conv_b, hy_f_w1, hy_f_b1, hy_f_w2, hy_f_b2, hy_f_w3, hy_f_b3,
           hy_f_w4, hy_f_freq, hy_bias, hy_out_norm, w_out,
           ffn2_norm, ffn2_w_gate, ffn2_w_up, ffn2_w_down, final_norm):
    b, s, d = x.shape
    depth = w_in.shape[0]
    d_att = diff_subln.shape[1] * rel_bias.shape[1]
    n_heads = rel_bias.shape[1]
    attn_tile = min(512, s)
    bf = lambda a: a.astype(BF16)

    xf = x.reshape(b * s, d)
    for l in range(depth):
        last = l == depth - 1
        xf = _ffn(xf, ffn1_norm[l], ffn1_w_gate[l], ffn1_w_up[l], ffn1_w_down[l])
        qkv, vv, x0 = _mix(xf, mix_norm[l], w_in[l], hy_conv_w[l], hy_conv_b[l], d_att=d_att, seq=s)
        lambda_init = 0.8 - 0.6 * math.exp(-0.3 * l)
        att = _attention(qkv.reshape(b, s, -1), _bias_tiles(rel_bias, attn_tile),
                         lambda_q1[l], lambda_k1[l], lambda_q2[l], lambda_k2[l], diff_subln[l],
                         n_heads=n_heads, lambda_init=lambda_init, t=attn_tile, n_chain=4,
                         n_q=2 if (s // attn_tile) % 2 == 0 else 1)
        y = _hyena_conv(vv.reshape(b, s, -1), hy_f_w1[l], hy_f_b1[l], hy_f_w2[l], hy_f_b2[l],
                        hy_f_w3[l], hy_f_b3[l], hy_f_w4[l], hy_f_freq[l])
        hy_parts = (y.reshape(b * s, -1), vv, x0, hy_bias[l], hy_out_norm[l])
        wo = bf(w_out[l])
        xf = _ffn(xf, ffn2_norm[l], ffn2_w_gate[l], ffn2_w_up[l], ffn2_w_down[l],
                  pre=(att.reshape(b * s, -1), wo[:d_att], wo[d_att:], hy_parts),
                  final_g=final_norm if last else None)
    if depth == 0:
        raise ValueError("depth must be positive")
    return xf.reshape(b, s, d)
```

```python
import functools
import math

import numpy as np
import jax
import jax.numpy as jnp
from jax import lax
from jax.experimental import pallas as pl
from jax.experimental.pallas import tpu as pltpu

F32 = jnp.float32
BF16 = jnp.bfloat16

RMS_EPS = 1e-6
LANES = 128
BF16_ROWS = 16
DIFF_HEAD_DIM = 64
V_HEAD_DIM = 2 * DIFF_HEAD_DIM
REL_BUCKETS = 32
REL_MAX_DIST = 128
N_HYENA_GROUPS = 8
HYENA_EMB_DIM = 33
HYENA_DECAY_TARGET = 1e-2
HYENA_FAST_DECAY = 0.3
HYENA_SLOW_DECAY = 1.5
N_LO = 128
LOG2_E = math.log2(math.e)
VMEM_LIMIT = 56 * 1024 * 1024


def _params(*sem):
    return pltpu.CompilerParams(dimension_semantics=sem, vmem_limit_bytes=VMEM_LIMIT)


def _resident(shape):
    return pl.BlockSpec(shape, lambda *_: (0,) * len(shape), pipeline_mode=pl.Buffered(1))


def _rms(x, g):
    return x * lax.rsqrt(jnp.mean(x * x, axis=-1, keepdims=True) + RMS_EPS) * g


def _dot(a, b):
    return jnp.dot(a, b, preferred_element_type=F32)


def _split(a):
    hi = a.astype(BF16)
    return hi, (a - hi.astype(F32)).astype(BF16)


def _hyena_gate_norm(y_ref, vv_ref, x0_ref, fb_ref, og_ref, mg_ref):
    vv = vv_ref[...].astype(F32)
    y = (y_ref[...] + vv * fb_ref[...]) * x0_ref[...].astype(F32)
    hi, lo = _split(y * y)
    ms = _dot(hi, mg_ref[...]) + _dot(lo, mg_ref[...])
    return y * lax.rsqrt(ms + RMS_EPS) * og_ref[...]


N_STAGE = 4


def _stage_weights(srcs, dsts, stages, sem):
    jobs = [(src, dst, stage, c) for src, dst, stage in zip(srcs, dsts, stages) for c in range(N_STAGE)]

    def copy(k):
        src, _, stage, c = jobs[k]
        rows = stage.shape[1]
        return pltpu.make_async_copy(src.at[pl.ds(c * rows, rows), :], stage.at[k % 2], sem.at[k % 2])

    copy(0).start()
    for k, (_, dst, stage, c) in enumerate(jobs):
        if k + 1 < len(jobs):
            copy(k + 1).start()
        copy(k).wait()
        rows = stage.shape[1]
        dst[pl.ds(c * rows, rows), :] = stage[k % 2].astype(dst.dtype)


def _ffn_kernel(*refs, ff_chunk, n_chunk, pre, post):
    if pre:
        x_ref, att_ref, woa_ref, woh_ref = refs[:4]
        hy_refs = refs[4:10]
        refs = refs[10:]
    else:
        x_ref = refs[0]
        refs = refs[1:]
    g_ref, wg_hbm, wu_hbm, wd_hbm = refs[:4]
    refs = refs[4:]
    if post:
        fg_ref, o_ref = refs[:2]
        refs = refs[2:]
    else:
        o_ref = refs[0]
        refs = refs[1:]
    wg_ref, wu_ref, wd_ref, stage_in, stage_out, sem = refs

    @pl.when(pl.program_id(0) == 0)
    def _():
        _stage_weights((wg_hbm, wu_hbm, wd_hbm), (wg_ref, wu_ref, wd_ref),
                       (stage_in, stage_in, stage_out), sem)

    x = x_ref[...]
    if pre:
        hy = _hyena_gate_norm(*hy_refs).astype(BF16)
        x = x + _dot(att_ref[...], woa_ref[...]) + _dot(hy, woh_ref[...])
    xn = _rms(x, g_ref[...]).astype(BF16)
    acc = jnp.zeros(x.shape, F32)
    for c in range(n_chunk):
        sl = slice(c * ff_chunk, (c + 1) * ff_chunk)
        gate = _dot(xn, wg_ref[:, sl])
        up = _dot(xn, wu_ref[:, sl])
        h = (jax.nn.silu(gate) * up).astype(BF16)
        acc = acc + _dot(h, wd_ref[sl, :])
    y = x + 0.5 * acc
    if post:
        y = _rms(y, fg_ref[...])
    o_ref[...] = y


def _ffn(x, norm_g, wg, wu, wd, *, pre=None, final_g=None, tm=None, ff_chunk=256):
    m, d = x.shape
    dff = wg.shape[1]
    n_chunk = dff // ff_chunk
    if tm is None:
        tm = 1024 if pre is None and m % 1024 == 0 else 512
    assert n_chunk * ff_chunk == dff and m % tm == 0
    assert d % (BF16_ROWS * N_STAGE) == 0 and dff % (BF16_ROWS * N_STAGE) == 0
    row = lambda w: pl.BlockSpec((tm, w), lambda i: (i, 0))
    args, specs = [x], [row(d)]
    if pre is not None:
        att, woa, woh, (y, vv, x0, filt_bias, out_g) = pre
        c = y.shape[1]
        gdim = c // N_HYENA_GROUPS
        assert gdim & (gdim - 1) == 0
        grp = np.arange(c) // gdim
        mg = jnp.asarray((grp[:, None] == grp[None, :]).astype(np.float32) / gdim).astype(BF16)
        args += [att, woa, woh, y, vv, x0, filt_bias.reshape(1, c), out_g.reshape(1, c), mg]
        specs += [row(att.shape[1]), _resident(woa.shape), _resident(woh.shape), row(c), row(c), row(c),
                  _resident((1, c)), _resident((1, c)), _resident((c, c))]
    hbm = pl.BlockSpec(memory_space=pl.ANY)
    args += [norm_g.reshape(1, d), wg, wu, wd]
    specs += [_resident((1, d)), hbm, hbm, hbm]
    if final_g is not None:
        args.append(final_g.reshape(1, d))
        specs.append(_resident((1, d)))
    kern = functools.partial(_ffn_kernel, ff_chunk=ff_chunk, n_chunk=n_chunk,
                             pre=pre is not None, post=final_g is not None)
    return pl.pallas_call(
        kern, grid=(m // tm,), in_specs=specs, out_specs=row(d),
        out_shape=jax.ShapeDtypeStruct((m, d), F32),
        scratch_shapes=[pltpu.VMEM((d, dff), BF16), pltpu.VMEM((d, dff), BF16), pltpu.VMEM((dff, d), BF16),
                        pltpu.VMEM((2, d // N_STAGE, dff), F32), pltpu.VMEM((2, dff // N_STAGE, d), F32),
                        pltpu.SemaphoreType.DMA((2,))],
        compiler_params=_params("arbitrary"), name="ffn_pre" if pre is not None else "ffn",
    )(*args)


HALO = 8


def _mix_kernel(x_ref, xp_ref, xn_ref, g_ref, w_hbm, cw_ref, cb_ref, qkv_ref, vv_ref, x0_ref,
                w_ref, stage, sem, *, d_att, c, scale, tiles_per_seq, hy_chunk):
    i = pl.program_id(0)
    n_att = 3 * d_att

    @pl.when(i == 0)
    def _():
        _stage_weights((w_hbm,), (w_ref,), (stage,), sem)

    xn = _rms(x_ref[...], g_ref[...]).astype(BF16)
    halo = _rms(jnp.concatenate([xp_ref[...], xn_ref[...]], axis=0), g_ref[...]).astype(BF16)
    first = i % tiles_per_seq == 0
    last = i % tiles_per_seq == tiles_per_seq - 1
    rows = x_ref.shape[0]
    cw = hy_chunk
    r = lax.broadcasted_iota(jnp.int32, (HALO, cw), 0)
    head, tail = slice(0, HALO), slice(rows - HALO, rows)

    def conv_cols(part, k0):
        cs = slice(n_att + part * c + k0, n_att + part * c + k0 + cw)
        ws = slice(part * c + k0, part * c + k0 + cw)
        z = _dot(xn, w_ref[:, cs])
        zh = _dot(halo, w_ref[:, cs])
        prev_row = jnp.where(first, 0.0, zh[HALO - 1:HALO])
        next_row = jnp.where(last, 0.0, zh[HALO:HALO + 1])
        z_m1 = pltpu.roll(z, 1, 0)
        z_p1 = pltpu.roll(z, rows - 1, 0)
        conv = lambda zm, zc, zp: (cb_ref[:, ws] + zm * cw_ref[0:1, ws] + zc * cw_ref[1:2, ws]
                                   + zp * cw_ref[2:3, ws])
        mid = conv(z_m1, z, z_p1)
        top = conv(jnp.where(r == 0, prev_row, z_m1[head]), z[head], z_p1[head])
        bot = conv(z_m1[tail], z[tail], jnp.where(r == HALO - 1, next_row, z_p1[tail]))
        return jnp.concatenate([top, mid[HALO:rows - HALO], bot], axis=0)

    for k0 in range(0, c, cw):
        vv_ref[:, k0:k0 + cw] = (conv_cols(2, k0) * conv_cols(1, k0)).astype(vv_ref.dtype)
        x0_ref[:, k0:k0 + cw] = conv_cols(0, k0).astype(x0_ref.dtype)

    for c0 in range(0, n_att, d_att):
        p = _dot(xn, w_ref[:, c0:c0 + d_att])
        if c0 == 0:
            p = p * scale
        qkv_ref[:, c0:c0 + d_att] = p.astype(BF16)


def _mix(x, norm_g, w_in, conv_w, conv_b, *, d_att, seq, tm=None, hy_chunk=256):
    m, d = x.shape
    if tm is None:
        tm = 1024 if seq % 1024 == 0 else 512
    n_att = 3 * d_att
    c = (w_in.shape[1] - n_att) // 3
    assert m % tm == 0 and seq % tm == 0 and tm % HALO == 0 and d % (BF16_ROWS * N_STAGE) == 0
    kern = functools.partial(_mix_kernel, d_att=d_att, c=c, scale=DIFF_HEAD_DIM ** -0.5 * LOG2_E,
                             tiles_per_seq=seq // tm, hy_chunk=hy_chunk)
    sub = tm // HALO
    last_blk = m // HALO - 1
    row = lambda w: pl.BlockSpec((tm, w), lambda i: (i, 0))
    return pl.pallas_call(
        kern, grid=(m // tm,),
        in_specs=[row(d),
                  pl.BlockSpec((HALO, d), lambda i: (jnp.maximum(i * sub - 1, 0), 0)),
                  pl.BlockSpec((HALO, d), lambda i: (jnp.minimum((i + 1) * sub, last_blk), 0)),
                  _resident((1, d)), pl.BlockSpec(memory_space=pl.ANY), _resident(conv_w.shape),
                  _resident((1, 3 * c))],
        out_specs=[row(n_att), row(c), row(c)],
        out_shape=[jax.ShapeDtypeStruct((m, n_att), BF16), jax.ShapeDtypeStruct((m, c), BF16),
                   jax.ShapeDtypeStruct((m, c), BF16)],
        scratch_shapes=[pltpu.VMEM(w_in.shape, BF16), pltpu.VMEM((2, d // N_STAGE, w_in.shape[1]), F32),
                        pltpu.SemaphoreType.DMA((2,))],
        compiler_params=_params("arbitrary"), name="mix",
    )(x, x, x, norm_g.reshape(1, d), w_in, conv_w, conv_b.reshape(1, 3 * c))


N_BIAS_TILES = 5


def _bias_kernel(tab_ref, o_ref, *, t):
    h = pl.program_id(0)
    d = pl.program_id(1) - N_BIAS_TILES // 2
    half = REL_BUCKETS // 2
    max_exact = half // 2
    rel = lax.broadcasted_iota(jnp.int32, (8, 2 * t), 1) + (d - 1) * t
    ret = jnp.where(rel > 0, half, 0)
    n = jnp.abs(rel)
    nf = jnp.maximum(n, 1).astype(F32)
    large = max_exact + (jnp.log(nf / max_exact) / math.log(REL_MAX_DIST / max_exact)
                         * (half - max_exact)).astype(jnp.int32)
    large = jnp.minimum(large, half - 1)
    bucket = ret + jnp.where(n < max_exact, n, large)
    row = jnp.zeros(rel.shape, F32)
    for b in range(REL_BUCKETS):
        row = jnp.where(bucket == b, tab_ref[b, h], row)
    rows = jnp.broadcast_to(row[0:1, :] * LOG2_E, (t, 2 * t))
    o_ref[...] = pltpu.roll(rows, 0, 1, stride=1, stride_axis=0)[:, t:2 * t]


def _bias_tiles(rel_bias, t):
    nh = rel_bias.shape[1]
    return pl.pallas_call(
        functools.partial(_bias_kernel, t=t), grid=(nh, N_BIAS_TILES),
        in_specs=[pl.BlockSpec(memory_space=pltpu.SMEM)],
        out_specs=pl.BlockSpec((None, None, t, t), lambda h, d: (h, d, 0, 0)),
        out_shape=jax.ShapeDtypeStruct((nh, N_BIAS_TILES, t, t), F32),
        compiler_params=_params("parallel", "parallel"), name="rel_bias_tiles",
    )(rel_bias)


def _attn_kernel(q_ref, k_ref, v_ref, bt_ref, lq1_ref, lk1_ref, lq2_ref, lk2_ref, sg_ref,
                 o_ref, q2_ref, vx_ref, *, t, nk, n_q, n_chain, lambda_init):
    g = pl.program_id(2)
    dv = v_ref.shape[1]

    @pl.when(g == 0)
    def _():
        vx_ref[:, 0:dv] = v_ref[...]
        ones_lane = lax.broadcasted_iota(jnp.int32, (v_ref.shape[0], dv), 1) == 0
        vx_ref[:, dv:2 * dv] = jnp.where(ones_lane, 1.0, 0.0).astype(vx_ref.dtype)

    far = N_BIAS_TILES // 2
    rc = 2 * t // n_chain
    lam = (jnp.exp(jnp.sum(lq1_ref[...] * lk1_ref[...])) - jnp.exp(jnp.sum(lq2_ref[...] * lk2_ref[...]))
           + lambda_init)

    def scores(i, base, d, rows, n_keys=t):
        wrapped = i + d >= nk
        j = jnp.where(wrapped, i + d - nk, i + d)
        k = k_ref[pl.ds(pl.multiple_of(j * t, t), n_keys), :]
        q2 = q2_ref[base + rows.start:base + rows.stop, :]
        s = lax.dot_general(q2, k, (((1,), (1,)), ((), ())), preferred_element_type=F32)
        cols = [s[:, c0:c0 + LANES] for c0 in range(0, n_keys, LANES)]
        if d <= 1 or d >= nk - 1:
            qr = slice(rows.start % t, rows.start % t + rows.stop - rows.start)
            bias = bt_ref[jnp.clip(j - i, -far, far) + far, qr, :]
            return j, [c + bias[:, n * LANES:(n + 1) * LANES] for n, c in enumerate(cols)], None
        side = jnp.where(wrapped, bt_ref[0, 0:1, 0:LANES], bt_ref[2 * far, 0:1, 0:LANES])
        return j, cols, side

    def tile(i, base, d, rows, m_prev, acc_prev):
        j, cols, side = scores(i, base, d, rows)
        shift = m_prev if side is None else m_prev - side
        rel = [(col - shift).astype(BF16) for col in cols]
        rm = jnp.max(functools.reduce(jnp.maximum, rel), axis=1, keepdims=True)
        delta = jnp.maximum(rm, 0.0)
        p = jnp.concatenate([jnp.exp2(x - delta) for x in rel], axis=1)
        d32 = jnp.broadcast_to(delta.astype(F32), m_prev.shape)
        alpha = jnp.exp2(-d32)
        pv = _dot(p, vx_ref[pl.ds(pl.multiple_of(j * t, t), t), :])
        acc_new = jnp.concatenate([alpha * acc_prev[:, 0:dv] + pv[:, 0:dv],
                                   alpha * acc_prev[:, dv:2 * dv] + pv[:, dv:2 * dv]], axis=1)
        return m_prev + d32, acc_new

    streams = []
    for a in range(n_q):
        i, base = g * n_q + a, a * 2 * t
        q = q_ref[a * t:(a + 1) * t, :]
        lane = lax.broadcasted_iota(jnp.int32, q.shape, 1)
        zero = jnp.zeros_like(q)
        q2_ref[base:base + t, :] = jnp.where(lane < DIFF_HEAD_DIM, q, zero)
        q2_ref[base + t:base + 2 * t, :] = jnp.where(lane >= DIFF_HEAD_DIM, q, zero)
        for r0 in range(0, 2 * t, rc):
            streams.append((i, base, slice(r0, r0 + rc)))
    state = []
    for i, base, rows in streams:
        m0 = jnp.max(scores(i, base, 0, rows, LANES)[1][0], axis=1, keepdims=True)
        state.append((jnp.broadcast_to(m0, (rc, LANES)), jnp.zeros((rc, 2 * dv), F32)))
    for d in range(nk):
        state = [tile(i, base, d, rows, *st) for (i, base, rows), st in zip(streams, state)]

    for a in range(n_q):
        acc = jnp.concatenate([acc for _, acc in state[a * n_chain:(a + 1) * n_chain]], axis=0)
        o = acc[:, 0:dv] / acc[:, dv:dv + 1]
        o = o[0:t] - lam * o[t:2 * t]
        o = _rms(o, sg_ref[...]) * (1.0 - lambda_init)
        o_ref[a * t:(a + 1) * t, :] = o.astype(o_ref.dtype)


def _attention(qkv, bias_tiles, lq1, lk1, lq2, lk2, subln, *, n_heads, lambda_init, t, n_chain, n_q):
    b, s, _ = qkv.shape
    nk = s // t
    assert nk * t == s and t % LANES == 0 and t + 1 >= 91 and (2 * t) % n_chain == 0 and nk % n_q == 0
    kern = functools.partial(_attn_kernel, t=t, nk=nk, n_q=n_q, n_chain=n_chain, lambda_init=lambda_init)
    vec = lambda n: pl.BlockSpec((1, n), lambda b_, h, i: (0, 0))
    return pl.pallas_call(
        kern, grid=(b, n_heads, nk // n_q),
        in_specs=[
            pl.BlockSpec((None, n_q * t, V_HEAD_DIM), lambda b_, h, i: (b_, i, h)),
            pl.BlockSpec((None, s, V_HEAD_DIM), lambda b_, h, i: (b_, 0, n_heads + h)),
            pl.BlockSpec((None, s, V_HEAD_DIM), lambda b_, h, i: (b_, 0, 2 * n_heads + h)),
            pl.BlockSpec((None, N_BIAS_TILES, t, t), lambda b_, h, i: (h, 0, 0, 0)),
            vec(DIFF_HEAD_DIM), vec(DIFF_HEAD_DIM), vec(DIFF_HEAD_DIM), vec(DIFF_HEAD_DIM),
            vec(V_HEAD_DIM),
        ],
        out_specs=pl.BlockSpec((None, n_q * t, V_HEAD_DIM), lambda b_, h, i: (b_, i, h)),
        out_shape=jax.ShapeDtypeStruct((b, s, n_heads * V_HEAD_DIM), BF16),
        scratch_shapes=[pltpu.VMEM((n_q * 2 * t, V_HEAD_DIM), BF16),
                        pltpu.VMEM((s, 2 * V_HEAD_DIM), BF16)],
        compiler_params=_params("parallel", "parallel", "arbitrary"), name="diff_attention",
    )(qkv, qkv, qkv, bias_tiles, lq1.reshape(1, -1), lk1.reshape(1, -1),
      lq2.reshape(1, -1), lk2.reshape(1, -1), subln.reshape(1, -1))


def _dft_tables(n_hi):
    n = n_hi * N_LO
    half = n_hi // 2
    hi = np.arange(n_hi)
    lo = np.arange(N_LO)
    f_hi = np.exp(-2j * np.pi * ((np.outer(hi, hi) % n_hi) / n_hi))
    fr, fi = f_hi.real, f_hi.imag
    s1_pair = np.block([[fr[:, :half], -fi[:, :half]], [fi[:, :half], fr[:, :half]]])
    s1_real = np.concatenate([fr[:, :half], fi[:, :half]], axis=0)
    ph = (hi[:, None, None] * lo[None, None, :] + n_hi * lo[None, :, None] * lo[None, None, :]) % n
    g = np.exp(-2j * np.pi * ph / n)
    g2 = np.concatenate([np.concatenate([g.real, -g.imag], axis=2),
                         np.concatenate([g.imag, g.real], axis=2)], axis=1)
    gh = np.stack([g2, np.swapaxes(g2, 1, 2)], axis=1)
    s4 = np.block([[fr[:half], fi[:half]], [-fi[:half], fr[:half]]]) / n
    cast = lambda a: jnp.asarray(a.astype(np.float32)).astype(BF16)
    return cast(s1_pair), cast(s1_real), cast(gh), cast(s4)


def _hdot(a, b, dims=(((1,), (0,)), ((), ()))):
    (ah, al), (bh, bl) = _split(a), _split(b)
    dg = functools.partial(lax.dot_general, dimension_numbers=dims, preferred_element_type=F32)
    return dg(ah, bh) + dg(ah, bl) + dg(al, bh)


def _filter_kernel(zt_ref, t_ref, w1_ref, b1_ref, w2_ref, b2_ref, w3_ref, b3_ref, w4_ref, fr_ref, dl_ref,
                   hf_ref, hb_ref, *, c):
    i = pl.program_id(0)
    fr = fr_ref[...]
    a = jnp.sin(fr * (_hdot(w1_ref[...], zt_ref[...]) + b1_ref[...]))
    a = jnp.sin(fr * (_hdot(w2_ref[...], a) + b2_ref[...]))
    a = jnp.sin(fr * (_hdot(w3_ref[...], a) + b3_ref[...]))
    hh = _hdot(a, w4_ref[...], (((0,), (0,)), ((), ())))
    decay = jnp.exp(-t_ref[...] * jnp.abs(dl_ref[...]))
    hf_ref[...] = (hh[:, 0:c] * decay).astype(hf_ref.dtype)
    r = lax.broadcasted_iota(jnp.int32, (hh.shape[0], c), 0)
    lag0 = jnp.logical_and(i == 0, r == 0)
    hb_ref[...] = jnp.where(lag0, 0.0, hh[:, c:2 * c] * decay).astype(hb_ref.dtype)


def _filter_taps(z, w1, b1, w2, b2, w3, b3, w4, freq, deltas, *, tb=None):
    s, emb = z.shape
    if tb is None:
        tb = 1024 if s % 1024 == 0 else 512
    c = deltas.shape[0]
    assert s % tb == 0
    col = lambda v: v.reshape(-1, 1)
    args = [jnp.pad(z, ((0, 0), (0, LANES - emb))).T, z[:, 0:1],
            jnp.pad(w1, ((0, LANES - emb), (0, 0))).T, col(b1), w2.T, col(b2), w3.T, col(b3), w4,
            col(freq), deltas.reshape(1, c)]
    specs = [pl.BlockSpec((LANES, tb), lambda i: (0, i)), pl.BlockSpec((tb, 1), lambda i: (i, 0))] + [
        pl.BlockSpec(a.shape, lambda i: (0, 0)) for a in args[2:]]
    out = pl.BlockSpec((tb, c), lambda i: (i, 0))
    shp = jax.ShapeDtypeStruct((s, c), BF16)
    return pl.pallas_call(
        functools.partial(_filter_kernel, c=c), grid=(s // tb,),
        in_specs=specs, out_specs=[out, out], out_shape=[shp, shp],
        compiler_params=_params("parallel"), name="hyena_filter_taps",
    )(*args)


def _lo_major(ref):
    return jnp.swapaxes(ref[...], 0, 1)


def _dft1_kernel(x0_ref, x1_ref, f_ref, a_ref):
    n_out = a_ref.shape[0]
    x0, x1 = _lo_major(x0_ref), _lo_major(x1_ref)
    a = jnp.stack([_dot(f_ref[...], jnp.concatenate([x0[j], x1[j]], axis=0)).astype(a_ref.dtype)
                   for j in range(x0.shape[0])], axis=0)
    a = jnp.swapaxes(a, 0, 1)
    a_ref[:, 0] = a[0:n_out]
    a_ref[:, 1] = a[n_out:2 * n_out]


def _dft1_pair(xv, f_mat, *, n_hi, t_lo=2 * BF16_ROWS):
    b, half, _, c = xv.shape
    spec = lambda off: pl.BlockSpec((None, half, t_lo, c), lambda p, j: (2 * p + off, 0, j, 0))
    return pl.pallas_call(
        _dft1_kernel, grid=(b // 2, N_LO // t_lo),
        in_specs=[spec(0), spec(1), pl.BlockSpec(f_mat.shape, lambda p, j: (0, 0))],
        out_specs=pl.BlockSpec((None, n_hi, 2, t_lo, c), lambda p, j: (p, 0, 0, j, 0)),
        out_shape=jax.ShapeDtypeStruct((b // 2, n_hi, 2, N_LO, c), BF16),
        compiler_params=_params("parallel", "parallel"), name="hyena_dft1",
    )(xv, xv, f_mat)


def _dft1_filter_kernel(hf_ref, hb_ref, f_ref, fa_ref):
    n_out = fa_ref.shape[0]
    c = hf_ref.shape[2]
    hf, hb = _lo_major(hf_ref), _lo_major(hb_ref)
    a = jnp.stack([_dot(f_ref[...], jnp.concatenate([hf[j], hb[j]], axis=1)).astype(fa_ref.dtype)
                   for j in range(hf.shape[0])], axis=0)
    a = jnp.swapaxes(a, 0, 1)
    for part, (r0, c0) in enumerate(((0, 0), (n_out, 0), (0, c), (n_out, c))):
        fa_ref[:, part] = a[r0:r0 + n_out, :, c0:c0 + c]


def _dft1_filter(hf, hb, f_mat, *, n_hi, t_lo=2 * BF16_ROWS):
    half, _, c = hf.shape
    blk = pl.BlockSpec((half, t_lo, c), lambda j: (0, j, 0))
    return pl.pallas_call(
        _dft1_filter_kernel, grid=(N_LO // t_lo,),
        in_specs=[blk, blk, pl.BlockSpec(f_mat.shape, lambda j: (0, 0))],
        out_specs=pl.BlockSpec((n_hi, 4, t_lo, c), lambda j: (0, 0, j, 0)),
        out_shape=jax.ShapeDtypeStruct((n_hi, 4, N_LO, c), BF16),
        compiler_params=_params("parallel"), name="hyena_filter_dft1",
    )(hf, hb, f_mat)


def _spectral_kernel(a_ref, fa_ref, gh_ref, cc_ref):
    c = a_ref.shape[-1]
    for u in range(a_ref.shape[0]):
        a = jnp.concatenate([a_ref[u].reshape(2 * N_LO, c), fa_ref[u, 0:2].reshape(2 * N_LO, c),
                             fa_ref[u, 2:4].reshape(2 * N_LO, c)], axis=1)
        x = _dot(gh_ref[u, 0], a)
        xr, xi = x[0:N_LO, 0:c], x[N_LO:2 * N_LO, 0:c]
        kr = x[0:N_LO, c:2 * c] + x[0:N_LO, 2 * c:3 * c]
        ki = x[N_LO:2 * N_LO, c:2 * c] - x[N_LO:2 * N_LO, 2 * c:3 * c]
        y = jnp.concatenate([xr * kr - xi * ki, xr * ki + xi * kr], axis=0).astype(BF16)
        cc_ref[u] = _dot(gh_ref[u, 1], y).astype(cc_ref.dtype).reshape(2, N_LO, c)


def _spectral(a, fa, gh, *, n_hi, kb=8):
    p, _, _, _, c = a.shape
    assert n_hi % kb == 0
    blk = pl.BlockSpec((None, kb, 2, N_LO, c), lambda q, k: (q, k, 0, 0, 0))
    return pl.pallas_call(
        _spectral_kernel, grid=(p, n_hi // kb),
        in_specs=[blk, pl.BlockSpec((kb, 4, N_LO, c), lambda q, k: (k, 0, 0, 0)),
                  pl.BlockSpec((kb, 2, 2 * N_LO, 2 * N_LO), lambda q, k: (k, 0, 0, 0))],
        out_specs=blk, out_shape=jax.ShapeDtypeStruct(a.shape, BF16),
        compiler_params=_params("parallel", "parallel"), name="hyena_spectral",
    )(a, fa, gh)


def _idft_kernel(cc_ref, m_ref, o_ref):
    half = o_ref.shape[1]
    cr, ci = jnp.swapaxes(cc_ref[:, 0], 0, 1), jnp.swapaxes(cc_ref[:, 1], 0, 1)
    y = jnp.stack([_dot(m_ref[...], jnp.concatenate([cr[j], ci[j]], axis=0)) for j in range(cr.shape[0])],
                  axis=0)
    y = jnp.swapaxes(y, 0, 1)
    o_ref[0] = y[0:half]
    o_ref[1] = y[half:2 * half]


def _idft(cc, s4, *, n_hi, t_lo=2 * BF16_ROWS):
    p, _, _, _, c = cc.shape
    half = n_hi // 2
    return pl.pallas_call(
        _idft_kernel, grid=(p, N_LO // t_lo),
        in_specs=[pl.BlockSpec((None, n_hi, 2, t_lo, c), lambda q, j: (q, 0, 0, j, 0)),
                  pl.BlockSpec(s4.shape, lambda q, j: (0, 0))],
        out_specs=pl.BlockSpec((2, half, t_lo, c), lambda q, j: (q, 0, j, 0)),
        out_shape=jax.ShapeDtypeStruct((2 * p, half, N_LO, c), F32),
        compiler_params=_params("parallel", "parallel"), name="hyena_idft",
    )(cc, s4)


def _filter_positions(s):
    t = jnp.linspace(0.0, 1.0, s, dtype=F32)[:, None]
    bands = (HYENA_EMB_DIM - 1) // 2
    w = 2.0 * math.pi * jnp.arange(s, dtype=F32)[:, None] / s
    f = jnp.linspace(1e-4, bands - 1, bands, dtype=F32)[None, :]
    fw = f * w
    return jnp.concatenate([t, jnp.cos(fw), -jnp.sin(fw)], axis=-1)


def _hyena_conv(vv, w1, b1, w2, b2, w3, b3, w4, freq):
    b, s, c = vv.shape
    n_hi = 2 * s // N_LO
    half = n_hi // 2
    assert b % 2 == 0 and n_hi * N_LO == 2 * s
    s1_pair, s1_real, gh, s4 = _dft_tables(n_hi)

    max_decay = math.log(HYENA_DECAY_TARGET) / HYENA_FAST_DECAY
    min_decay = math.log(HYENA_DECAY_TARGET) / HYENA_SLOW_DECAY
    deltas = jnp.linspace(min_decay, max_decay, c, dtype=F32)
    hf, hb = _filter_taps(_filter_positions(s), w1, b1, w2, b2, w3, b3, w4, freq, deltas)
    fa = _dft1_filter(hf.reshape(half, N_LO, c), hb.reshape(half, N_LO, c), s1_real, n_hi=n_hi)

    a = _dft1_pair(vv.reshape(b, half, N_LO, c), s1_pair, n_hi=n_hi)
    cc = _spectral(a, fa, gh, n_hi=n_hi)
    return _idft(cc, s4, n_hi=n_hi).reshape(b, s, c)


def kernel(x, rel_bias, ffn1_norm, ffn1_w_gate, ffn1_w_up, ffn1_w_down, mix_norm, w_in,
           lambda_q1, lambda_k1, lambda_q2, lambda_k2, diff_subln,
           hy_conv_w, hy_conv_b, hy_f_w1, hy_f_b1, hy_f_w2, hy_f_b2, hy_f_w3, hy_f_b3,
           hy_f_w4, hy_f_freq, hy_bias, hy_out_norm, w_out,
           ffn2_norm, ffn2_w_gate, ffn2_w_up, ffn2_w_down, final_norm):
    b, s, d = x.shape
    depth = w_in.shape[0]
    d_att = diff_subln.shape[1] * rel_bias.shape[1]
    n_heads = rel_bias.shape[1]
    attn_tile = min(512, s)
    bf = lambda a: a.astype(BF16)

    xf = x.reshape(b * s, d)
    for l in range(depth):
        last = l == depth - 1
        xf = _ffn(xf, ffn1_norm[l], ffn1_w_gate[l], ffn1_w_up[l], ffn1_w_down[l])
        qkv, vv, x0 = _mix(xf, mix_norm[l], w_in[l], hy_conv_w[l], hy_conv_b[l], d_att=d_att, seq=s)
        lambda_init = 0.8 - 0.6 * math.exp(-0.3 * l)
        att = _attention(qkv.reshape(b, s, -1), _bias_tiles(rel_bias, attn_tile),
                         lambda_q1[l], lambda_k1[l], lambda_q2[l], lambda_k2[l], diff_subln[l],
                         n_heads=n_heads, lambda_init=lambda_init, t=attn_tile, n_chain=4,
                         n_q=2 if (s // attn_tile) % 2 == 0 else 1)
        y = _hyena_conv(vv.reshape(b, s, -1), hy_f_w1[l], hy_f_b1[l], hy_f_w2[l], hy_f_b2[l],
                        hy_f_w3[l], hy_f_b3[l], hy_f_w4[l], hy_f_freq[l])
        hy_parts = (y.reshape(b * s, -1), vv, x0, hy_bias[l], hy_out_norm[l])
        wo = bf(w_out[l])
        xf = _ffn(xf, ffn2_norm[l], ffn2_w_gate[l], ffn2_w_up[l], ffn2_w_down[l],
                  pre=(att.reshape(b * s, -1), wo[:d_att], wo[d_att:], hy_parts),
                  final_g=final_norm if last else None)
    if depth == 0:
        raise ValueError("depth must be positive")
    return xf.reshape(b, s, d)
```

```python
import functools
import math

import numpy as np
import jax
import jax.numpy as jnp
from jax import lax
from jax.experimental import pallas as pl
from jax.experimental.pallas import tpu as pltpu

F32 = jnp.float32
BF16 = jnp.bfloat16

RMS_EPS = 1e-6
LANES = 128
BF16_ROWS = 16
DIFF_HEAD_DIM = 64
V_HEAD_DIM = 2 * DIFF_HEAD_DIM
REL_BUCKETS = 32
REL_MAX_DIST = 128
N_HYENA_GROUPS = 8
HYENA_EMB_DIM = 33
HYENA_DECAY_TARGET = 1e-2
HYENA_FAST_DECAY = 0.3
HYENA_SLOW_DECAY = 1.5
N_LO = 128
LOG2_E = math.log2(math.e)
VMEM_LIMIT = 56 * 1024 * 1024


def _params(*sem):
    return pltpu.CompilerParams(dimension_semantics=sem, vmem_limit_bytes=VMEM_LIMIT)


def _resident(shape):
    return pl.BlockSpec(shape, lambda *_: (0,) * len(shape), pipeline_mode=pl.Buffered(1))


def _rms(x, g):
    return x * lax.rsqrt(jnp.mean(x * x, axis=-1, keepdims=True) + RMS_EPS) * g


def _dot(a, b):
    return jnp.dot(a, b, preferred_element_type=F32)


def _split(a):
    hi = a.astype(BF16)
    return hi, (a - hi.astype(F32)).astype(BF16)


def _hyena_gate_norm(y_ref, vv_ref, x0_ref, fb_ref, og_ref, mg_ref):
    vv = vv_ref[...].astype(F32)
    y = (y_ref[...] + vv * fb_ref[...]) * x0_ref[...].astype(F32)
    hi, lo = _split(y * y)
    ms = _dot(hi, mg_ref[...]) + _dot(lo, mg_ref[...])
    return y * lax.rsqrt(ms + RMS_EPS) * og_ref[...]


N_STAGE = 4


def _stage_weights(srcs, dsts, stages, sem):
    jobs = [(src, dst, stage, c) for src, dst, stage in zip(srcs, dsts, stages) for c in range(N_STAGE)]

    def copy(k):
        src, _, stage, c = jobs[k]
        rows = stage.shape[1]
        return pltpu.make_async_copy(src.at[pl.ds(c * rows, rows), :], stage.at[k % 2], sem.at[k % 2])

    copy(0).start()
    for k, (_, dst, stage, c) in enumerate(jobs):
        if k + 1 < len(jobs):
            copy(k + 1).start()
        copy(k).wait()
        rows = stage.shape[1]
        dst[pl.ds(c * rows, rows), :] = stage[k % 2].astype(dst.dtype)


def _ffn_kernel(*refs, ff_chunk, n_chunk, pre, post):
    if pre:
        x_ref, att_ref, woa_ref, woh_ref = refs[:4]
        hy_refs = refs[4:10]
        refs = refs[10:]
    else:
        x_ref = refs[0]
        refs = refs[1:]
    g_ref, wg_hbm, wu_hbm, wd_hbm = refs[:4]
    refs = refs[4:]
    if post:
        fg_ref, o_ref = refs[:2]
        refs = refs[2:]
    else:
        o_ref = refs[0]
        refs = refs[1:]
    wg_ref, wu_ref, wd_ref, stage_in, stage_out, sem = refs

    @pl.when(pl.program_id(0) == 0)
    def _():
        _stage_weights((wg_hbm, wu_hbm, wd_hbm), (wg_ref, wu_ref, wd_ref),
                       (stage_in, stage_in, stage_out), sem)

    x = x_ref[...]
    if pre:
        hy = _hyena_gate_norm(*hy_refs).astype(BF16)
        x = x + _dot(att_ref[...], woa_ref[...]) + _dot(hy, woh_ref[...])
    xn = _rms(x, g_ref[...]).astype(BF16)
    acc = jnp.zeros(x.shape, F32)
    for c in range(n_chunk):
        sl = slice(c * ff_chunk, (c + 1) * ff_chunk)
        gate = _dot(xn, wg_ref[:, sl])
        up = _dot(xn, wu_ref[:, sl])
        h = (jax.nn.silu(gate) * up).astype(BF16)
        acc = acc + _dot(h, wd_ref[sl, :])
    y = x + 0.5 * acc
    if post:
        y = _rms(y, fg_ref[...])
    o_ref[...] = y


def _ffn(x, norm_g, wg, wu, wd, *, pre=None, final_g=None, tm=None, ff_chunk=256):
    m, d = x.shape
    dff = wg.shape[1]
    n_chunk = dff // ff_chunk
    if tm is None:
        tm = 1024 if pre is None and m % 1024 == 0 else 512
    assert n_chunk * ff_chunk == dff and m % tm == 0
    assert d % (BF16_ROWS * N_STAGE) == 0 and dff % (BF16_ROWS * N_STAGE) == 0
    row = lambda w: pl.BlockSpec((tm, w), lambda i: (i, 0))
    args, specs = [x], [row(d)]
    if pre is not None:
        att, woa, woh, (y, vv, x0, filt_bias, out_g) = pre
        c = y.shape[1]
        gdim = c // N_HYENA_GROUPS
        assert gdim & (gdim - 1) == 0
        grp = np.arange(c) // gdim
        mg = jnp.asarray((grp[:, None] == grp[None, :]).astype(np.float32) / gdim).astype(BF16)
        args += [att, woa, woh, y, vv, x0, filt_bias.reshape(1, c), out_g.reshape(1, c), mg]
        specs += [row(att.shape[1]), _resident(woa.shape), _resident(woh.shape), row(c), row(c), row(c),
                  _resident((1, c)), _resident((1, c)), _resident((c, c))]
    hbm = pl.BlockSpec(memory_space=pl.ANY)
    args += [norm_g.reshape(1, d), wg, wu, wd]
    specs += [_resident((1, d)), hbm, hbm, hbm]
    if final_g is not None:
        args.append(final_g.reshape(1, d))
        specs.append(_resident((1, d)))
    kern = functools.partial(_ffn_kernel, ff_chunk=ff_chunk, n_chunk=n_chunk,
                             pre=pre is not None, post=final_g is not None)
    return pl.pallas_call(
        kern, grid=(m // tm,), in_specs=specs, out_specs=row(d),
        out_shape=jax.ShapeDtypeStruct((m, d), F32),
        scratch_shapes=[pltpu.VMEM((d, dff), BF16), pltpu.VMEM((d, dff), BF16), pltpu.VMEM((dff, d), BF16),
                        pltpu.VMEM((2, d // N_STAGE, dff), F32), pltpu.VMEM((2, dff // N_STAGE, d), F32),
                        pltpu.SemaphoreType.DMA((2,))],
        compiler_params=_params("arbitrary"), name="ffn_pre" if pre is not None else "ffn",
    )(*args)


HALO = 8


def _mix_kernel(x_ref, xp_ref, xn_ref, g_ref, w_hbm, cw_ref, cb_ref, qkv_ref, vv_ref, x0_ref,
                w_ref, stage, sem, *, d_att, c, scale, tiles_per_seq, hy_chunk):
    i = pl.program_id(0)
    n_att = 3 * d_att

    @pl.when(i == 0)
    def _():
        _stage_weights((w_hbm,), (w_ref,), (stage,), sem)

    xn = _rms(x_ref[...], g_ref[...]).astype(BF16)
    halo = _rms(jnp.concatenate([xp_ref[...], xn_ref[...]], axis=0), g_ref[...]).astype(BF16)
    first = i % tiles_per_seq == 0
    last = i % tiles_per_seq == tiles_per_seq - 1
    rows = x_ref.shape[0]
    cw = hy_chunk
    r = lax.broadcasted_iota(jnp.int32, (HALO, cw), 0)
    head, tail = slice(0, HALO), slice(rows - HALO, rows)

    def conv_cols(part, k0):
        cs = slice(n_att + part * c + k0, n_att + part * c + k0 + cw)
        ws = slice(part * c + k0, part * c + k0 + cw)
        z = _dot(xn, w_ref[:, cs])
        zh = _dot(halo, w_ref[:, cs])
        prev_row = jnp.where(first, 0.0, zh[HALO - 1:HALO])
        next_row = jnp.where(last, 0.0, zh[HALO:HALO + 1])
        z_m1 = pltpu.roll(z, 1, 0)
        z_p1 = pltpu.roll(z, rows - 1, 0)
        conv = lambda zm, zc, zp: (cb_ref[:, ws] + zm * cw_ref[0:1, ws] + zc * cw_ref[1:2, ws]
                                   + zp * cw_ref[2:3, ws])
        mid = conv(z_m1, z, z_p1)
        top = conv(jnp.where(r == 0, prev_row, z_m1[head]), z[head], z_p1[head])
        bot = conv(z_m1[tail], z[tail], jnp.where(r == HALO - 1, next_row, z_p1[tail]))
        return jnp.concatenate([top, mid[HALO:rows - HALO], bot], axis=0)

    for k0 in range(0, c, cw):
        vv_ref[:, k0:k0 + cw] = (conv_cols(2, k0) * conv_cols(1, k0)).astype(vv_ref.dtype)
        x0_ref[:, k0:k0 + cw] = conv_cols(0, k0).astype(x0_ref.dtype)

    for c0 in range(0, n_att, d_att):
        p = _dot(xn, w_ref[:, c0:c0 + d_att])
        if c0 == 0:
            p = p * scale
        qkv_ref[:, c0:c0 + d_att] = p.astype(BF16)


def _mix(x, norm_g, w_in, conv_w, conv_b, *, d_att, seq, tm=None, hy_chunk=256):
    m, d = x.shape
    if tm is None:
        tm = 1024 if seq % 1024 == 0 else 512
    n_att = 3 * d_att
    c = (w_in.shape[1] - n_att) // 3
    assert m % tm == 0 and seq % tm == 0 and tm % HALO == 0 and d % (BF16_ROWS * N_STAGE) == 0
    kern = functools.partial(_mix_kernel, d_att=d_att, c=c, scale=DIFF_HEAD_DIM ** -0.5 * LOG2_E,
                             tiles_per_seq=seq // tm, hy_chunk=hy_chunk)
    sub = tm // HALO
    last_blk = m // HALO - 1
    row = lambda w: pl.BlockSpec((tm, w), lambda i: (i, 0))
    return pl.pallas_call(
        kern, grid=(m // tm,),
        in_specs=[row(d),
                  pl.BlockSpec((HALO, d), lambda i: (jnp.maximum(i * sub - 1, 0), 0)),
                  pl.BlockSpec((HALO, d), lambda i: (jnp.minimum((i + 1) * sub, last_blk), 0)),
                  _resident((1, d)), pl.BlockSpec(memory_space=pl.ANY), _resident(conv_w.shape),
                  _resident((1, 3 * c))],
        out_specs=[row(n_att), row(c), row(c)],
        out_shape=[jax.ShapeDtypeStruct((m, n_att), BF16), jax.ShapeDtypeStruct((m, c), BF16),
                   jax.ShapeDtypeStruct((m, c), BF16)],
        scratch_shapes=[pltpu.VMEM(w_in.shape, BF16), pltpu.VMEM((2, d // N_STAGE, w_in.shape[1]), F32),
                        pltpu.SemaphoreType.DMA((2,))],
        compiler_params=_params("arbitrary"), name="mix",
    )(x, x, x, norm_g.reshape(1, d), w_in, conv_w, conv_b.reshape(1, 3 * c))


N_BIAS_TILES = 5


def _bias_kernel(tab_ref, o_ref, *, t):
    h = pl.program_id(0)
    d = pl.program_id(1) - N_BIAS_TILES // 2
    half = REL_BUCKETS // 2
    max_exact = half // 2
    rel = lax.broadcasted_iota(jnp.int32, (8, 2 * t), 1) + (d - 1) * t
    ret = jnp.where(rel > 0, half, 0)
    n = jnp.abs(rel)
    nf = jnp.maximum(n, 1).astype(F32)
    large = max_exact + (jnp.log(nf / max_exact) / math.log(REL_MAX_DIST / max_exact)
                         * (half - max_exact)).astype(jnp.int32)
    large = jnp.minimum(large, half - 1)
    bucket = ret + jnp.where(n < max_exact, n, large)
    row = jnp.zeros(rel.shape, F32)
    for b in range(REL_BUCKETS):
        row = jnp.where(bucket == b, tab_ref[b, h], row)
    rows = jnp.broadcast_to(row[0:1, :] * LOG2_E, (t, 2 * t))
    o_ref[...] = pltpu.roll(rows, 0, 1, stride=1, stride_axis=0)[:, t:2 * t]


def _bias_tiles(rel_bias, t):
    nh = rel_bias.shape[1]
    return pl.pallas_call(
        functools.partial(_bias_kernel, t=t), grid=(nh, N_BIAS_TILES),
        in_specs=[pl.BlockSpec(memory_space=pltpu.SMEM)],
        out_specs=pl.BlockSpec((None, None, t, t), lambda h, d: (h, d, 0, 0)),
        out_shape=jax.ShapeDtypeStruct((nh, N_BIAS_TILES, t, t), F32),
        compiler_params=_params("parallel", "parallel"), name="rel_bias_tiles",
    )(rel_bias)


def _attn_kernel(q_ref, k_ref, v_ref, bt_ref, lq1_ref, lk1_ref, lq2_ref, lk2_ref, sg_ref,
                 o_ref, q2_ref, vx_ref, *, t, nk, n_q, n_chain, lambda_init):
    g = pl.program_id(2)
    dv = v_ref.shape[1]

    @pl.when(g == 0)
    def _():
        vx_ref[:, 0:dv] = v_ref[...]
        ones_lane = lax.broadcasted_iota(jnp.int32, (v_ref.shape[0], dv), 1) == 0
        vx_ref[:, dv:2 * dv] = jnp.where(ones_lane, 1.0, 0.0).astype(vx_ref.dtype)

    far = N_BIAS_TILES // 2
    rc = 2 * t // n_chain
    lam = (jnp.exp(jnp.sum(lq1_ref[...] * lk1_ref[...])) - jnp.exp(jnp.sum(lq2_ref[...] * lk2_ref[...]))
           + lambda_init)

    def scores(i, base, d, rows, n_keys=t):
        wrapped = i + d >= nk
        j = jnp.where(wrapped, i + d - nk, i + d)
        k = k_ref[pl.ds(pl.multiple_of(j * t, t), n_keys), :]
        q2 = q2_ref[base + rows.start:base + rows.stop, :]
        s = lax.dot_general(q2, k, (((1,), (1,)), ((), ())), preferred_element_type=F32)
        cols = [s[:, c0:c0 + LANES] for c0 in range(0, n_keys, LANES)]
        if d <= 1 or d >= nk - 1:
            qr = slice(rows.start % t, rows.start % t + rows.stop - rows.start)
            bias = bt_ref[jnp.clip(j - i, -far, far) + far, qr, :]
            return j, [c + bias[:, n * LANES:(n + 1) * LANES] for n, c in enumerate(cols)], None
        side = jnp.where(wrapped, bt_ref[0, 0:1, 0:LANES], bt_ref[2 * far, 0:1, 0:LANES])
        return j, cols, side

    def tile(i, base, d, rows, m_prev, acc_prev):
        j, cols, side = scores(i, base, d, rows)
        shift = m_prev if side is None else m_prev - side
        rel = [(col - shift).astype(BF16) for col in cols]
        rm = jnp.max(functools.reduce(jnp.maximum, rel), axis=1, keepdims=True)
        delta = jnp.maximum(rm, 0.0)
        p = jnp.concatenate([jnp.exp2(x - delta) for x in rel], axis=1)
        d32 = jnp.broadcast_to(delta.astype(F32), m_prev.shape)
        alpha = jnp.exp2(-d32)
        pv = _dot(p, vx_ref[pl.ds(pl.multiple_of(j * t, t), t), :])
        acc_new = jnp.concatenate([alpha * acc_prev[:, 0:dv] + pv[:, 0:dv],
                                   alpha * acc_prev[:, dv:2 * dv] + pv[:, dv:2 * dv]], axis=1)
        return m_prev + d32, acc_new

    streams = []
    for a in range(n_q):
        i, base = g * n_q + a, a * 2 * t
        q = q_ref[a * t:(a + 1) * t, :]
        lane = lax.broadcasted_iota(jnp.int32, q.shape, 1)
        zero = jnp.zeros_like(q)
        q2_ref[base:base + t, :] = jnp.where(lane < DIFF_HEAD_DIM, q, zero)
        q2_ref[base + t:base + 2 * t, :] = jnp.where(lane >= DIFF_HEAD_DIM, q, zero)
        for r0 in range(0, 2 * t, rc):
            streams.append((i, base, slice(r0, r0 + rc)))
    state = []
    for i, base, rows in streams:
        m0 = jnp.max(scores(i, base, 0, rows, LANES)[1][0], axis=1, keepdims=True)
        state.append((jnp.broadcast_to(m0, (rc, LANES)), jnp.zeros((rc, 2 * dv), F32)))
    for d in range(nk):
        state = [tile(i, base, d, rows, *st) for (i, base, rows), st in zip(streams, state)]

    for a in range(n_q):
        acc = jnp.concatenate([acc for _, acc in state[a * n_chain:(a + 1) * n_chain]], axis=0)
        o = acc[:, 0:dv] / acc[:, dv:dv + 1]
        o = o[0:t] - lam * o[t:2 * t]
        o = _rms(o, sg_ref[...]) * (1.0 - lambda_init)
        o_ref[a * t:(a + 1) * t, :] = o.astype(o_ref.dtype)


def _attention(qkv, bias_tiles, lq1, lk1, lq2, lk2, subln, *, n_heads, lambda_init, t, n_chain, n_q):
    b, s, _ = qkv.shape
    nk = s // t
    assert nk * t == s and t % LANES == 0 and t + 1 >= 91 and (2 * t) % n_chain == 0 and nk % n_q == 0
    kern = functools.partial(_attn_kernel, t=t, nk=nk, n_q=n_q, n_chain=n_chain, lambda_init=lambda_init)
    vec = lambda n: pl.BlockSpec((1, n), lambda b_, h, i: (0, 0))
    return pl.pallas_call(
        kern, grid=(b, n_heads, nk // n_q),
        in_specs=[
            pl.BlockSpec((None, n_q * t, V_HEAD_DIM), lambda b_, h, i: (b_, i, h)),
            pl.BlockSpec((None, s, V_HEAD_DIM), lambda b_, h, i: (b_, 0, n_heads + h)),
            pl.BlockSpec((None, s, V_HEAD_DIM), lambda b_, h, i: (b_, 0, 2 * n_heads + h)),
            pl.BlockSpec((None, N_BIAS_TILES, t, t), lambda b_, h, i: (h, 0, 0, 0)),
            vec(DIFF_HEAD_DIM), vec(DIFF_HEAD_DIM), vec(DIFF_HEAD_DIM), vec(DIFF_HEAD_DIM),
            vec(V_HEAD_DIM),
        ],
        out_specs=pl.BlockSpec((None, n_q * t, V_HEAD_DIM), lambda b_, h, i: (b_, i, h)),
        out_shape=jax.ShapeDtypeStruct((b, s, n_heads * V_HEAD_DIM), BF16),
        scratch_shapes=[pltpu.VMEM((n_q * 2 * t, V_HEAD_DIM), BF16),
                        pltpu.VMEM((s, 2 * V_HEAD_DIM), BF16)],
        compiler_params=_params("parallel", "parallel", "arbitrary"), name="diff_attention",
    )(qkv, qkv, qkv, bias_tiles, lq1.reshape(1, -1), lk1.reshape(1, -1),
      lq2.reshape(1, -1), lk2.reshape(1, -1), subln.reshape(1, -1))


def _dft_tables(n_hi):
    n = n_hi * N_LO
    half = n_hi // 2
    hi = np.arange(n_hi)
    lo = np.arange(N_LO)
    f_hi = np.exp(-2j * np.pi * ((np.outer(hi, hi) % n_hi) / n_hi))
    fr, fi = f_hi.real, f_hi.imag
    s1_pair = np.block([[fr[:, :half], -fi[:, :half]], [fi[:, :half], fr[:, :half]]])
    s1_real = np.concatenate([fr[:, :half], fi[:, :half]], axis=0)
    ph = (hi[:, None, None] * lo[None, None, :] + n_hi * lo[None, :, None] * lo[None, None, :]) % n
    g = np.exp(-2j * np.pi * ph / n)
    g2 = np.concatenate([np.concatenate([g.real, -g.imag], axis=2),
                         np.concatenate([g.imag, g.real], axis=2)], axis=1)
    gh = np.stack([g2, np.swapaxes(g2, 1, 2)], axis=1)
    s4 = np.block([[fr[:half], fi[:half]], [-fi[:half], fr[:half]]]) / n
    cast = lambda a: jnp.asarray(a.astype(np.float32)).astype(BF16)
    return cast(s1_pair), cast(s1_real), cast(gh), cast(s4)


def _hdot(a, b, dims=(((1,), (0,)), ((), ()))):
    (ah, al), (bh, bl) = _split(a), _split(b)
    dg = functools.partial(lax.dot_general, dimension_numbers=dims, preferred_element_type=F32)
    return dg(ah, bh) + dg(ah, bl) + dg(al, bh)


def _filter_kernel(zt_ref, t_ref, w1_ref, b1_ref, w2_ref, b2_ref, w3_ref, b3_ref, w4_ref, fr_ref, dl_ref,
                   hf_ref, hb_ref, *, c):
    i = pl.program_id(0)
    fr = fr_ref[...]
    a = jnp.sin(fr * (_hdot(w1_ref[...], zt_ref[...]) + b1_ref[...]))
    a = jnp.sin(fr * (_hdot(w2_ref[...], a) + b2_ref[...]))
    a = jnp.sin(fr * (_hdot(w3_ref[...], a) + b3_ref[...]))
    hh = _hdot(a, w4_ref[...], (((0,), (0,)), ((), ())))
    decay = jnp.exp(-t_ref[...] * jnp.abs(dl_ref[...]))
    hf_ref[...] = (hh[:, 0:c] * decay).astype(hf_ref.dtype)
    r = lax.broadcasted_iota(jnp.int32, (hh.shape[0], c), 0)
    lag0 = jnp.logical_and(i == 0, r == 0)
    hb_ref[...] = jnp.where(lag0, 0.0, hh[:, c:2 * c] * decay).astype(hb_ref.dtype)


def _filter_taps(z, w1, b1, w2, b2, w3, b3, w4, freq, deltas, *, tb=None):
    s, emb = z.shape
    if tb is None:
        tb = 1024 if s % 1024 == 0 else 512
    c = deltas.shape[0]
    assert s % tb == 0
    col = lambda v: v.reshape(-1, 1)
    args = [jnp.pad(z, ((0, 0), (0, LANES - emb))).T, z[:, 0:1],
            jnp.pad(w1, ((0, LANES - emb), (0, 0))).T, col(b1), w2.T, col(b2), w3.T, col(b3), w4,
            col(freq), deltas.reshape(1, c)]
    specs = [pl.BlockSpec((LANES, tb), lambda i: (0, i)), pl.BlockSpec((tb, 1), lambda i: (i, 0))] + [
        pl.BlockSpec(a.shape, lambda i: (0, 0)) for a in args[2:]]
    out = pl.BlockSpec((tb, c), lambda i: (i, 0))
    shp = jax.ShapeDtypeStruct((s, c), BF16)
    return pl.pallas_call(
        functools.partial(_filter_kernel, c=c), grid=(s // tb,),
        in_specs=specs, out_specs=[out, out], out_shape=[shp, shp],
        compiler_params=_params("parallel"), name="hyena_filter_taps",
    )(*args)


def _lo_major(ref):
    return jnp.swapaxes(ref[...], 0, 1)


def _dft1_kernel(x0_ref, x1_ref, f_ref, a_ref):
    n_out = a_ref.shape[0]
    x0, x1 = _lo_major(x0_ref), _lo_major(x1_ref)
    a = jnp.stack([_dot(f_ref[...], jnp.concatenate([x0[j], x1[j]], axis=0)).astype(a_ref.dtype)
                   for j in range(x0.shape[0])], axis=0)
    a = jnp.swapaxes(a, 0, 1)
    a_ref[:, 0] = a[0:n_out]
    a_ref[:, 1] = a[n_out:2 * n_out]


def _dft1_pair(xv, f_mat, *, n_hi, t_lo=BF16_ROWS):
    b, half, _, c = xv.shape
    spec = lambda off: pl.BlockSpec((None, half, t_lo, c), lambda p, j: (2 * p + off, 0, j, 0))
    return pl.pallas_call(
        _dft1_kernel, grid=(b // 2, N_LO // t_lo),
        in_specs=[spec(0), spec(1), pl.BlockSpec(f_mat.shape, lambda p, j: (0, 0))],
        out_specs=pl.BlockSpec((None, n_hi, 2, t_lo, c), lambda p, j: (p, 0, 0, j, 0)),
        out_shape=jax.ShapeDtypeStruct((b // 2, n_hi, 2, N_LO, c), BF16),
        compiler_params=_params("parallel", "parallel"), name="hyena_dft1",
    )(xv, xv, f_mat)


def _dft1_filter_kernel(hf_ref, hb_ref, f_ref, fa_ref):
    n_out = fa_ref.shape[0]
    c = hf_ref.shape[2]
    hf, hb = _lo_major(hf_ref), _lo_major(hb_ref)
    a = jnp.stack([_dot(f_ref[...], jnp.concatenate([hf[j], hb[j]], axis=1)).astype(fa_ref.dtype)
                   for j in range(hf.shape[0])], axis=0)
    a = jnp.swapaxes(a, 0, 1)
    for part, (r0, c0) in enumerate(((0, 0), (n_out, 0), (0, c), (n_out, c))):
        fa_ref[:, part] = a[r0:r0 + n_out, :, c0:c0 + c]


def _dft1_filter(hf, hb, f_mat, *, n_hi, t_lo=BF16_ROWS):
    half, _, c = hf.shape
    blk = pl.BlockSpec((half, t_lo, c), lambda j: (0, j, 0))
    return pl.pallas_call(
        _dft1_filter_kernel, grid=(N_LO // t_lo,),
        in_specs=[blk, blk, pl.BlockSpec(f_mat.shape, lambda j: (0, 0))],
        out_specs=pl.BlockSpec((n_hi, 4, t_lo, c), lambda j: (0, 0, j, 0)),
        out_shape=jax.ShapeDtypeStruct((n_hi, 4, N_LO, c), BF16),
        compiler_params=_params("parallel"), name="hyena_filter_dft1",
    )(hf, hb, f_mat)


def _spectral_kernel(a_ref, fa_ref, gh_ref, cc_ref):
    c = a_ref.shape[-1]
    for u in range(a_ref.shape[0]):
        a = jnp.concatenate([a_ref[u].reshape(2 * N_LO, c), fa_ref[u, 0:2].reshape(2 * N_LO, c),
                             fa_ref[u, 2:4].reshape(2 * N_LO, c)], axis=1)
        x = _dot(gh_ref[u, 0], a)
        xr, xi = x[0:N_LO, 0:c], x[N_LO:2 * N_LO, 0:c]
        kr = x[0:N_LO, c:2 * c] + x[0:N_LO, 2 * c:3 * c]
        ki = x[N_LO:2 * N_LO, c:2 * c] - x[N_LO:2 * N_LO, 2 * c:3 * c]
        y = jnp.concatenate([xr * kr - xi * ki, xr * ki + xi * kr], axis=0).astype(BF16)
        cc_ref[u] = _dot(gh_ref[u, 1], y).astype(cc_ref.dtype).reshape(2, N_LO, c)


def _spectral(a, fa, gh, *, n_hi, kb=8):
    p, _, _, _, c = a.shape
    assert n_hi % kb == 0
    deep = pl.Buffered(3)
    in_specs = [pl.BlockSpec((kb, 2, N_LO, c), lambda k: (k, 0, 0, 0), pipeline_mode=deep),
                pl.BlockSpec((kb, 4, N_LO, c), lambda k: (k, 0, 0, 0), pipeline_mode=deep),
                pl.BlockSpec((kb, 2, 2 * N_LO, 2 * N_LO), lambda k: (k, 0, 0, 0), pipeline_mode=deep)]
    out_specs = [pl.BlockSpec((kb, 2, N_LO, c), lambda k: (k, 0, 0, 0))]

    def outer(a_hbm, fa_hbm, gh_hbm, cc_hbm):
        for q in range(p):
            pltpu.emit_pipeline(_spectral_kernel, grid=(n_hi // kb,), in_specs=in_specs,
                                out_specs=out_specs)(a_hbm.at[q], fa_hbm, gh_hbm, cc_hbm.at[q])

    hbm = pl.BlockSpec(memory_space=pl.ANY)
    return pl.pallas_call(
        outer, in_specs=[hbm, hbm, hbm], out_specs=hbm, out_shape=jax.ShapeDtypeStruct(a.shape, BF16),
        compiler_params=pltpu.CompilerParams(vmem_limit_bytes=VMEM_LIMIT), name="hyena_spectral",
    )(a, fa, gh)


def _idft_kernel(cc_ref, m_ref, o_ref):
    half = o_ref.shape[1]
    cr, ci = jnp.swapaxes(cc_ref[:, 0], 0, 1), jnp.swapaxes(cc_ref[:, 1], 0, 1)
    y = jnp.stack([_dot(m_ref[...], jnp.concatenate([cr[j], ci[j]], axis=0)) for j in range(cr.shape[0])],
                  axis=0)
    y = jnp.swapaxes(y, 0, 1)
    o_ref[0] = y[0:half]
    o_ref[1] = y[half:2 * half]


def _idft(cc, s4, *, n_hi, t_lo=BF16_ROWS):
    p, _, _, _, c = cc.shape
    half = n_hi // 2
    return pl.pallas_call(
        _idft_kernel, grid=(p, N_LO // t_lo),
        in_specs=[pl.BlockSpec((None, n_hi, 2, t_lo, c), lambda q, j: (q, 0, 0, j, 0)),
                  pl.BlockSpec(s4.shape, lambda q, j: (0, 0))],
        out_specs=pl.BlockSpec((2, half, t_lo, c), lambda q, j: (q, 0, j, 0)),
        out_shape=jax.ShapeDtypeStruct((2 * p, half, N_LO, c), F32),
        compiler_params=_params("parallel", "parallel"), name="hyena_idft",
    )(cc, s4)


def _filter_positions(s):
    t = jnp.linspace(0.0, 1.0, s, dtype=F32)[:, None]
    bands = (HYENA_EMB_DIM - 1) // 2
    w = 2.0 * math.pi * jnp.arange(s, dtype=F32)[:, None] / s
    f = jnp.linspace(1e-4, bands - 1, bands, dtype=F32)[None, :]
    fw = f * w
    return jnp.concatenate([t, jnp.cos(fw), -jnp.sin(fw)], axis=-1)


def _hyena_conv(vv, w1, b1, w2, b2, w3, b3, w4, freq):
    b, s, c = vv.shape
    n_hi = 2 * s // N_LO
    half = n_hi // 2
    assert b % 2 == 0 and n_hi * N_LO == 2 * s
    s1_pair, s1_real, gh, s4 = _dft_tables(n_hi)

    max_decay = math.log(HYENA_DECAY_TARGET) / HYENA_FAST_DECAY
    min_decay = math.log(HYENA_DECAY_TARGET) / HYENA_SLOW_DECAY
    deltas = jnp.linspace(min_decay, max_decay, c, dtype=F32)
    hf, hb = _filter_taps(_filter_positions(s), w1, b1, w2, b2, w3, b3, w4, freq, deltas)
    fa = _dft1_filter(hf.reshape(half, N_LO, c), hb.reshape(half, N_LO, c), s1_real, n_hi=n_hi)

    a = _dft1_pair(vv.reshape(b, half, N_LO, c), s1_pair, n_hi=n_hi)
    cc = _spectral(a, fa, gh, n_hi=n_hi)
    return _idft(cc, s4, n_hi=n_hi).reshape(b, s, c)


def kernel(x, rel_bias, ffn1_norm, ffn1_w_gate, ffn1_w_up, ffn1_w_down, mix_norm, w_in,
           lambda_q1, lambda_k1, lambda_q2, lambda_k2, diff_subln,
           hy_conv_w, hy_conv_b, hy_f_w1, hy_f_b1, hy_f_w2, hy_f_b2, hy_f_w3, hy_f_b3,
           hy_f_w4, hy_f_freq, hy_bias, hy_out_norm, w_out,
           ffn2_norm, ffn2_w_gate, ffn2_w_up, ffn2_w_down, final_norm):
    b, s, d = x.shape
    depth = w_in.shape[0]
    d_att = diff_subln.shape[1] * rel_bias.shape[1]
    n_heads = rel_bias.shape[1]
    attn_tile = min(512, s)
    bf = lambda a: a.astype(BF16)

    xf = x.reshape(b * s, d)
    for l in range(depth):
        last = l == depth - 1
        xf = _ffn(xf, ffn1_norm[l], ffn1_w_gate[l], ffn1_w_up[l], ffn1_w_down[l])
        qkv, vv, x0 = _mix(xf, mix_norm[l], w_in[l], hy_conv_w[l], hy_conv_b[l], d_att=d_att, seq=s)
        lambda_init = 0.8 - 0.6 * math.exp(-0.3 * l)
        att = _attention(qkv.reshape(b, s, -1), _bias_tiles(rel_bias, attn_tile),
                         lambda_q1[l], lambda_k1[l], lambda_q2[l], lambda_k2[l], diff_subln[l],
                         n_heads=n_heads, lambda_init=lambda_init, t=attn_tile, n_chain=4,
                         n_q=2 if (s // attn_tile) % 2 == 0 else 1)
        y = _hyena_conv(vv.reshape(b, s, -1), hy_f_w1[l], hy_f_b1[l], hy_f_w2[l], hy_f_b2[l],
                        hy_f_w3[l], hy_f_b3[l], hy_f_w4[l], hy_f_freq[l])
        hy_parts = (y.reshape(b * s, -1), vv, x0, hy_bias[l], hy_out_norm[l])
        wo = bf(w_out[l])
        xf = _ffn(xf, ffn2_norm[l], ffn2_w_gate[l], ffn2_w_up[l], ffn2_w_down[l],
                  pre=(att.reshape(b * s, -1), wo[:d_att], wo[d_att:], hy_parts),
                  final_g=final_norm if last else None)
    if depth == 0:
        raise ValueError("depth must be positive")
    return xf.reshape(b, s, d)
```
